```python
import math
import jax
import jax.numpy as jnp
from jax import lax
import numpy as np

D_MODEL = 2048
BATCH = 8
SEQ = 2048
DEPTH = 2
DEC_BATCH = 128
DEC_SEQ = 8
PAST_LEN = 2048
PAGE_SIZE = 128

D_MIX = D_MODEL
W_GROUP = D_MIX // 4
W_A = W_GROUP
W_B = W_GROUP
W_C = W_GROUP
W_D = W_GROUP
H_A = 4
DH_A = W_A // (2 * H_A)
DV_A = 2 * DH_A
Q_BLOCK = 128
CONV_B = 3
H_C = 4
DK_C = W_C // H_C
DV_C = W_C // H_C
CONV_C = 4
DN_CHUNK = 64
H_D = 4
BW_D = W_D // H_D
CONV_D = 4
RG_C = 8.0
D_FF = 5632
N_EXPERTS = 8
TOP_K = 2
D_FF_EXPERT = 7168
N_DENSE_LAYERS = (DEPTH + 1) // 2
N_MOE_LAYERS = DEPTH // 2
N_MOD = 6
EPS = 1e-6
N_A = 3 * W_A
N_B = 3 * W_B
N_C = 4 * W_C + 2 * H_C
N_D = 2 * W_D
N_IN = N_A + N_B + N_C + N_D

kernel_name = 'hybrid_parallel_heads_decode_step'


def rmsnorm(x, w):
    xf = x.astype(jnp.float32)
    y = xf * lax.rsqrt(jnp.mean(xf * xf, axis=-1, keepdims=True) + EPS)
    return (y * w.astype(jnp.float32)).astype(x.dtype)


def l2norm(x):
    return x * lax.rsqrt(jnp.sum(x * x, axis=-1, keepdims=True) + EPS)


def causal_conv(u, buf, w):
    width = w.shape[0]
    t = u.shape[1]
    full = jnp.concatenate([buf.astype(u.dtype), u], axis=1)
    out = sum(full[:, j:j + t] * w[j] for j in range(width))
    return out, full[:, t:]


def diff_attention(q, k, v, qpos, kpos, lam, slopes):
    b, t = q.shape[:2]
    qb = math.gcd(t, Q_BLOCK)
    nb = t // qb
    q_blocks = jnp.moveaxis(q.reshape(b, nb, qb, H_A, 2, DH_A), 1, 0)
    p_blocks = qpos.reshape(nb, qb)
    scale = DH_A ** -0.5

    def one_block(args):
        qblk, pblk = args
        s = jnp.einsum('bqhmd,bshmd->bhmqs', qblk, k).astype(jnp.float32) * scale
        dist = (pblk[:, None] - kpos[None, :]).astype(jnp.float32)
        s = s - slopes[None, :, None, None, None] * dist[None, None, None]
        s = jnp.where((dist >= 0)[None, None, None], s, -1e30)
        probs = jax.nn.softmax(s, axis=-1)
        amap = probs[:, :, 0] - lam * probs[:, :, 1]
        return jnp.einsum('bhqs,bshe->bqhe', amap.astype(v.dtype), v)

    o = lax.map(one_block, (q_blocks, p_blocks))
    return jnp.moveaxis(o, 0, 1).reshape(b, t, H_A, DV_A)


def gated_delta_rule(q, k, v, g, beta, s0):
    b, t, h, dk = q.shape
    dv = v.shape[-1]
    c = math.gcd(t, DN_CHUNK)
    n = t // c

    def to_chunks(a):
        a = jnp.moveaxis(a, 2, 1)
        a = a.reshape(b, h, n, c, *a.shape[3:])
        return jnp.moveaxis(a, 2, 0)

    qc, kc, vc, gc, bc = (to_chunks(a) for a in (q, k, v, g, beta))
    gcum = jnp.cumsum(gc, axis=-1)
    causal = jnp.tril(jnp.ones((c, c), bool))
    strict = jnp.tril(jnp.ones((c, c), bool), -1)
    diff = gcum[..., :, None] - gcum[..., None, :]
    decay = jnp.where(causal, jnp.exp(jnp.where(causal, diff, 0.0)), 0.0)
    kb = kc * bc[..., None]
    lmat = jnp.where(strict, jnp.einsum('...id,...jd->...ij', kb, kc) * decay, 0.0)
    amat = lmat + jnp.eye(c, dtype=lmat.dtype)
    rhs = jnp.concatenate([vc * bc[..., None], kb * jnp.exp(gcum)[..., None]], axis=-1)
    sol = lax.linalg.triangular_solve(amat, rhs, left_side=True, lower=True, unit_diagonal=True)
    u, w = sol[..., :dv], sol[..., dv:]
    qk = jnp.einsum('...id,...jd->...ij', qc, kc) * decay

    def step(s, xs):
        q_i, k_i, u_i, w_i, qk_i, g_i = xs
        v_new = u_i - jnp.einsum('bhcd,bhde->bhce', w_i, s)
        o_i = (jnp.einsum('bhcd,bhde->bhce', q_i * jnp.exp(g_i)[..., None], s)
               + jnp.einsum('bhij,bhje->bhie', qk_i, v_new))
        g_last = g_i[..., -1:]
        s = (s * jnp.exp(g_last)[..., None]
             + jnp.einsum('bhcd,bhce->bhde', k_i * jnp.exp(g_last - g_i)[..., None], v_new))
        return s, o_i

    s_final, o = lax.scan(step, s0, (qc, kc, u, w, qk, gcum))
    o = jnp.moveaxis(o, 0, 2).reshape(b, h, t, dv)
    return jnp.moveaxis(o, 1, 2), s_final


def linear_scan(a, bx, h0):
    bx = bx.at[:, 0].add(a[:, 0] * h0)

    def combine(lhs, rhs):
        return lhs[0] * rhs[0], rhs[0] * lhs[1] + rhs[1]

    _, hs = lax.associative_scan(combine, (a, bx), axis=1)
    return hs


def mixer_block(h, l, p, past_k, past_v, conv_b_buf, dn_buf, dn_s, lru_buf, lru_h):
    b, t, _ = h.shape
    n_past = past_k.shape[1]
    f32 = jnp.float32
    proj = h @ p['w_in'][l]
    pa, pb, pc, pd = jnp.split(proj, [N_A, N_A + N_B, N_A + N_B + N_C], axis=-1)

    qa, ka, va = jnp.split(pa, 3, axis=-1)
    qa = qa.reshape(b, t, H_A, 2, DH_A)
    ka = ka.reshape(b, t, H_A, 2, DH_A)
    va = va.reshape(b, t, H_A, DV_A)
    k_all = jnp.concatenate([past_k.astype(ka.dtype), ka], axis=1)
    v_all = jnp.concatenate([past_v.astype(va.dtype), va], axis=1)
    kpos = jnp.arange(n_past + t)
    qpos = n_past + jnp.arange(t)
    slopes = jnp.exp2(-8.0 * jnp.arange(1, H_A + 1, dtype=f32) / H_A)
    lam_init = 0.8 - 0.6 * math.exp(-0.3 * l)
    lam = (jnp.exp(jnp.sum(p['lam_q1'][l] * p['lam_k1'][l]).astype(f32))
           - jnp.exp(jnp.sum(p['lam_q2'][l] * p['lam_k2'][l]).astype(f32)) + lam_init)
    oa = diff_attention(qa, k_all, v_all, qpos, kpos, lam, slopes)
    ya = (rmsnorm(oa, p['subln_w'][l]) * (1.0 - lam_init)).reshape(b, t, W_A)

    bg, cg, xt = jnp.split(pb, 3, axis=-1)
    conv_out, new_conv_b = causal_conv(cg * xt, conv_b_buf, p['conv_b_w'][l])
    yb = bg * conv_out

    qkv = pc[..., :3 * W_C]
    z = pc[..., 3 * W_C:4 * W_C]
    a_raw = pc[..., 4 * W_C:4 * W_C + H_C].astype(f32)
    b_raw = pc[..., 4 * W_C + H_C:].astype(f32)
    qkv_c, new_dn_buf = causal_conv(qkv, dn_buf, p['dn_conv_w'][l])
    qkv_c = jax.nn.silu(qkv_c).astype(f32)
    qd, kd, vd = jnp.split(qkv_c, 3, axis=-1)
    qd = l2norm(qd.reshape(b, t, H_C, DK_C)) * (DK_C ** -0.5)
    kd = l2norm(kd.reshape(b, t, H_C, DK_C))
    vd = vd.reshape(b, t, H_C, DV_C)
    beta = jax.nn.sigmoid(b_raw)
    g = -jnp.exp(p['dn_a_log'][l].astype(f32)) * jax.nn.softplus(a_raw + p['dn_dt_bias'][l].astype(f32))
    oc, new_s = gated_delta_rule(qd, kd, vd, g, beta, dn_s.astype(f32))
    oc = rmsnorm(oc, p['dn_norm_w'][l]) * jax.nn.silu(z.reshape(b, t, H_C, DV_C).astype(f32))
    yc = oc.reshape(b, t, W_C)

    xd, gd = jnp.split(pd, 2, axis=-1)
    xcv, new_lru_buf = causal_conv(xd, lru_buf, p['lru_conv_w'][l])
    xf = (xcv + p['lru_conv_b'][l]).astype(f32)
    xh = xf.reshape(b, t, H_D, BW_D)
    r = jax.nn.sigmoid(jnp.einsum('bthi,hij->bthj', xh, p['lru_wa'][l].astype(f32)).reshape(b, t, W_D)
                       + p['lru_ba'][l].astype(f32))
    ig = jax.nn.sigmoid(jnp.einsum('bthi,hij->bthj', xh, p['lru_wx'][l].astype(f32)).reshape(b, t, W_D)
                        + p['lru_bx'][l].astype(f32))
    log_a = -RG_C * r * jax.nn.softplus(-p['lru_lambda'][l].astype(f32))
    a = jnp.exp(log_a)
    hseq = linear_scan(a, jnp.sqrt(-jnp.expm1(2.0 * log_a)) * (ig * xf), lru_h.astype(f32))
    yd = hseq * jax.nn.gelu(gd.astype(f32))

    dt = h.dtype
    y = jnp.concatenate([ya.astype(dt), yb.astype(dt), yc.astype(dt), yd.astype(dt)], axis=-1) @ p['w_out'][l]
    return (y, ka, va, new_conv_b, new_dn_buf, new_s.astype(dt), new_lru_buf, hseq[:, -1].astype(dt))


def swiglu(h, wg, wu, wd):
    return (jax.nn.silu(h @ wg) * (h @ wu)) @ wd


def moe_swiglu(h, router, wg, wu, wd):
    b, t, d = h.shape
    hf = h.reshape(b * t, d)
    logits = (hf @ router).astype(jnp.float32)
    top_v, top_i = lax.top_k(logits, TOP_K)
    gates = jax.nn.softmax(top_v, axis=-1)
    combine = jnp.sum(jax.nn.one_hot(top_i, N_EXPERTS, dtype=jnp.float32) * gates[..., None], axis=1)
    out = jnp.zeros_like(hf)
    for e in range(N_EXPERTS):
        out = out + combine[:, e:e + 1].astype(hf.dtype) * swiglu(hf, wg[e], wu[e], wd[e])
    return out.reshape(b, t, d)


def forward_group(x, c, cache_k, cache_v, page_table, conv_b_st, dn_conv_st, dn_s_st, lru_conv_st, lru_h_st, p):
    b = x.shape[0]
    ks, vs, cbs, dcs, dss, lcs, lhs = [], [], [], [], [], [], []
    for l in range(DEPTH):
        mod = jax.nn.silu(c) @ p['w_mod'][l] + p['b_mod'][l]
        sh1, sc1, g1, sh2, sc2, g2 = jnp.split(mod[:, None, :], N_MOD, axis=-1)
        if page_table is None:
            past_k = jnp.zeros((b, 0, H_A, 2, DH_A), x.dtype)
            past_v = jnp.zeros((b, 0, H_A, DV_A), x.dtype)
        else:
            past_k = cache_k[l][page_table].reshape(b, -1, H_A, 2, DH_A)
            past_v = cache_v[l][page_table].reshape(b, -1, H_A, DV_A)
        hn = rmsnorm(x, p['norm_mix'][l]) * (1.0 + sc1) + sh1
        y, k_new, v_new, cb, dc, ds, lc, lh = mixer_block(
            hn, l, p, past_k, past_v, conv_b_st[l], dn_conv_st[l], dn_s_st[l], lru_conv_st[l], lru_h_st[l])
        x = x + g1 * y
        hn = rmsnorm(x, p['norm_ffn'][l]) * (1.0 + sc2) + sh2
        if l % 2 == 0:
            j = l // 2
            f = swiglu(hn, p['ffn_w_gate'][j], p['ffn_w_up'][j], p['ffn_w_down'][j])
        else:
            j = l // 2
            f = moe_swiglu(hn, p['moe_router'][j], p['moe_w_gate'][j], p['moe_w_up'][j], p['moe_w_down'][j])
        x = x + g2 * f
        ks.append(k_new); vs.append(v_new); cbs.append(cb); dcs.append(dc)
        dss.append(ds); lcs.append(lc); lhs.append(lh)
    y = rmsnorm(x, p['final_norm'])
    return (y, jnp.stack(ks), jnp.stack(vs), jnp.stack(cbs), jnp.stack(dcs),
            jnp.stack(dss), jnp.stack(lcs), jnp.stack(lhs))


def setup_inputs(seed: int = 0) -> dict:
    key = jax.random.key(seed)
    ks = iter(jax.random.split(key, 64))
    f32 = jnp.float32

    def nrm(shape, scale):
        return jax.random.normal(next(ks), shape, f32) * scale

    def unif(shape, lo, hi):
        return jax.random.uniform(next(ks), shape, f32, lo, hi)

    n_pages = PAST_LEN // PAGE_SIZE
    n_used = DEC_BATCH * n_pages
    n_pool = n_used + (n_used + 3) // 4
    d = D_MODEL
    inp = {}
    inp['x_prompt'] = nrm((BATCH, SEQ, d), 1.0)
    inp['x_sample'] = nrm((DEC_BATCH, DEC_SEQ, d), 1.0)
    inp['cache_k'] = nrm((DEPTH, n_pool, PAGE_SIZE, H_A, 2, DH_A), 1.0)
    inp['cache_v'] = nrm((DEPTH, n_pool, PAGE_SIZE, H_A, DV_A), 1.0)
    inp['state_conv_b'] = nrm((DEPTH, DEC_BATCH, CONV_B - 1, W_B), 1.0)
    inp['state_dn_conv'] = nrm((DEPTH, DEC_BATCH, CONV_C - 1, 3 * W_C), 1.0)
    inp['state_dn'] = nrm((DEPTH, DEC_BATCH, H_C, DK_C, DV_C), 0.1)
    inp['state_lru_conv'] = nrm((DEPTH, DEC_BATCH, CONV_D - 1, W_D), 1.0)
    inp['state_lru_h'] = nrm((DEPTH, DEC_BATCH, W_D), 0.5)
    perm = jax.random.permutation(next(ks), n_pool)
    inp['page_table'] = perm[:n_used].reshape(DEC_BATCH, n_pages).astype(jnp.int32)
    inp['c_prompt'] = nrm((BATCH, d), 1.0)
    inp['c_sample'] = nrm((DEC_BATCH, d), 1.0)
    inp['w_mod'] = nrm((DEPTH, d, N_MOD * d), 0.5 * d ** -0.5)
    inp['b_mod'] = nrm((DEPTH, N_MOD * d), 0.02)
    inp['norm_mix'] = 1.0 + nrm((DEPTH, d), 0.02)
    inp['norm_ffn'] = 1.0 + nrm((DEPTH, d), 0.02)
    inp['w_in'] = nrm((DEPTH, d, N_IN), d ** -0.5)
    inp['w_out'] = nrm((DEPTH, D_MIX, d), D_MIX ** -0.5)
    inp['lam_q1'] = nrm((DEPTH, DH_A), 0.1)
    inp['lam_k1'] = nrm((DEPTH, DH_A), 0.1)
    inp['lam_q2'] = nrm((DEPTH, DH_A), 0.1)
    inp['lam_k2'] = nrm((DEPTH, DH_A), 0.1)
    inp['subln_w'] = 1.0 + nrm((DEPTH, DV_A), 0.02)
    inp['conv_b_w'] = nrm((DEPTH, CONV_B, W_B), CONV_B ** -0.5)
    inp['dn_conv_w'] = nrm((DEPTH, CONV_C, 3 * W_C), CONV_C ** -0.5)
    inp['dn_a_log'] = jnp.log(unif((DEPTH, H_C), 1.0, 16.0))
    dt0 = jnp.exp(unif((DEPTH, H_C), math.log(1e-3), math.log(1e-1)))
    inp['dn_dt_bias'] = dt0 + jnp.log(-jnp.expm1(-dt0))
    inp['dn_norm_w'] = 1.0 + nrm((DEPTH, DV_C), 0.02)
    inp['lru_conv_w'] = nrm((DEPTH, CONV_D, W_D), CONV_D ** -0.5)
    inp['lru_conv_b'] = nrm((DEPTH, W_D), 0.02)
    inp['lru_wa'] = nrm((DEPTH, H_D, BW_D, BW_D), BW_D ** -0.5)
    inp['lru_ba'] = nrm((DEPTH, W_D), 0.02)
    inp['lru_wx'] = nrm((DEPTH, H_D, BW_D, BW_D), BW_D ** -0.5)
    inp['lru_bx'] = nrm((DEPTH, W_D), 0.02)
    a_base = unif((DEPTH, W_D), 0.9, 0.999) ** (1.0 / RG_C)
    inp['lru_lambda'] = jnp.log(a_base) - jnp.log1p(-a_base)
    inp['ffn_w_gate'] = nrm((N_DENSE_LAYERS, d, D_FF), d ** -0.5)
    inp['ffn_w_up'] = nrm((N_DENSE_LAYERS, d, D_FF), d ** -0.5)
    inp['ffn_w_down'] = nrm((N_DENSE_LAYERS, D_FF, d), D_FF ** -0.5)
    inp['moe_router'] = nrm((N_MOE_LAYERS, d, N_EXPERTS), d ** -0.5)
    inp['moe_w_gate'] = nrm((N_MOE_LAYERS, N_EXPERTS, d, D_FF_EXPERT), d ** -0.5)
    inp['moe_w_up'] = nrm((N_MOE_LAYERS, N_EXPERTS, d, D_FF_EXPERT), d ** -0.5)
    inp['moe_w_down'] = nrm((N_MOE_LAYERS, N_EXPERTS, D_FF_EXPERT, d), D_FF_EXPERT ** -0.5)
    inp['final_norm'] = 1.0 + nrm((d,), 0.02)
    return inp


def reference(x_prompt, x_sample, cache_k, cache_v, state_conv_b, state_dn_conv, state_dn,
              state_lru_conv, state_lru_h, page_table, c_prompt, c_sample,
              w_mod, b_mod, norm_mix, norm_ffn, w_in, w_out,
              lam_q1, lam_k1, lam_q2, lam_k2, subln_w, conv_b_w,
              dn_conv_w, dn_a_log, dn_dt_bias, dn_norm_w,
              lru_conv_w, lru_conv_b, lru_wa, lru_ba, lru_wx, lru_bx, lru_lambda,
              ffn_w_gate, ffn_w_up, ffn_w_down,
              moe_router, moe_w_gate, moe_w_up, moe_w_down, final_norm):
    p = dict(w_mod=w_mod, b_mod=b_mod, norm_mix=norm_mix, norm_ffn=norm_ffn, w_in=w_in, w_out=w_out,
             lam_q1=lam_q1, lam_k1=lam_k1, lam_q2=lam_q2, lam_k2=lam_k2, subln_w=subln_w,
             conv_b_w=conv_b_w, dn_conv_w=dn_conv_w, dn_a_log=dn_a_log, dn_dt_bias=dn_dt_bias,
             dn_norm_w=dn_norm_w, lru_conv_w=lru_conv_w, lru_conv_b=lru_conv_b, lru_wa=lru_wa,
             lru_ba=lru_ba, lru_wx=lru_wx, lru_bx=lru_bx, lru_lambda=lru_lambda,
             ffn_w_gate=ffn_w_gate, ffn_w_up=ffn_w_up, ffn_w_down=ffn_w_down,
             moe_router=moe_router, moe_w_gate=moe_w_gate, moe_w_up=moe_w_up, moe_w_down=moe_w_down,
             final_norm=final_norm)
    bp = x_prompt.shape[0]
    dt = x_prompt.dtype
    zero_cb = jnp.zeros((DEPTH, bp, CONV_B - 1, W_B), dt)
    zero_dc = jnp.zeros((DEPTH, bp, CONV_C - 1, 3 * W_C), dt)
    zero_ds = jnp.zeros((DEPTH, bp, H_C, DK_C, DV_C), jnp.float32)
    zero_lc = jnp.zeros((DEPTH, bp, CONV_D - 1, W_D), dt)
    zero_lh = jnp.zeros((DEPTH, bp, W_D), jnp.float32)
    (y_prompt, k_prompt, v_prompt, conv_b_prompt, dn_conv_prompt, dn_prompt,
     lru_conv_prompt, lru_h_prompt) = forward_group(
        x_prompt, c_prompt, None, None, None, zero_cb, zero_dc, zero_ds, zero_lc, zero_lh, p)
    (y_sample, k_sample, v_sample, conv_b_sample, dn_conv_sample, dn_sample,
     lru_conv_sample, lru_h_sample) = forward_group(
        x_sample, c_sample, cache_k, cache_v, page_table, state_conv_b, state_dn_conv, state_dn,
        state_lru_conv, state_lru_h, p)
    return (y_prompt, y_sample, k_prompt, v_prompt, k_sample, v_sample,
            conv_b_prompt, conv_b_sample, dn_conv_prompt, dn_conv_sample,
            dn_prompt, dn_sample, lru_conv_prompt, lru_conv_sample,
            lru_h_prompt, lru_h_sample)
```

```python
import functools
import math

import jax
import jax.numpy as jnp
from jax import lax
from jax.experimental import pallas as pl
from jax.experimental.pallas import tpu as pltpu

D_MODEL = 2048
BATCH = 8
SEQ = 2048
DEPTH = 2
DEC_BATCH = 128
DEC_SEQ = 8
W_GROUP = D_MODEL // 4
W_A = W_B = W_C = W_D = W_GROUP
H_A = 4
DH_A = W_A // (2 * H_A)
DV_A = 2 * DH_A
Q_BLOCK = 128
CONV_B = 3
H_C = 4
DK_C = W_C // H_C
DV_C = W_C // H_C
CONV_C = 4
DN_CHUNK = 64
H_D = 4
BW_D = W_D // H_D
CONV_D = 4
RG_C = 8.0
D_FF = 5632
N_EXPERTS = 8
TOP_K = 2
D_FF_EXPERT = 7168
N_MOD = 6
EPS = 1e-6
N_A = 3 * W_A
N_B = 3 * W_B
N_C = 4 * W_C + 2 * H_C
N_D = 2 * W_D

N_PROMPT = BATCH * SEQ
N_SAMPLE = DEC_BATCH * DEC_SEQ
N_TOK = N_PROMPT + N_SAMPLE
SUBLANES = 8
N_GROUPS = N_TOK // SUBLANES
LANES = 128

OFF_A = 0
OFF_B = N_A
OFF_C = N_A + N_B
OFF_D = OFF_C + 4 * W_C
OFF_AB = OFF_D + N_D
N_IN_PAD = OFF_AB + LANES

VMEM_LIMIT = 48 * 1024 * 1024
MOE_TM = 512
MOE_TN = 512
MOE_TK = 1024
MOE_TILES = (TOP_K * N_TOK) // MOE_TM + N_EXPERTS
MOE_ROWS = MOE_TILES * MOE_TM

f32 = jnp.float32
bf16 = jnp.bfloat16


def _params(sem):
    return pltpu.CompilerParams(dimension_semantics=sem, vmem_limit_bytes=VMEM_LIMIT)


def _silu(x):
    return x * (1.0 / (1.0 + jnp.exp(-x)))


def _mod_kernel(c_ref, w_ref, b_ref, o_ref):
    c = _silu(c_ref[...]).astype(bf16)
    o_ref[...] = jnp.dot(c, w_ref[...].astype(bf16), preferred_element_type=f32) + b_ref[...]


def modulation(c_all, w, b):
    r = c_all.shape[0]
    n = w.shape[1]
    tn = 1024
    return pl.pallas_call(
        _mod_kernel,
        grid=(n // tn,),
        in_specs=[pl.BlockSpec((r, D_MODEL), lambda j: (0, 0)),
                  pl.BlockSpec((D_MODEL, tn), lambda j: (0, j)),
                  pl.BlockSpec((1, tn), lambda j: (0, j))],
        out_specs=pl.BlockSpec((r, tn), lambda j: (0, j)),
        out_shape=jax.ShapeDtypeStruct((r, n), f32),
        compiler_params=_params(("arbitrary",)),
        name="modulation",
    )(c_all, w, b)


def _norm_mod_body(x_ref, nw_ref, sc_ref, sh_ref):
    x = x_ref[...]
    tr = x.shape[0]
    y = x * lax.rsqrt(jnp.mean(x * x, axis=-1, keepdims=True) + EPS) * nw_ref[...]
    y = y.reshape(tr // SUBLANES, SUBLANES, D_MODEL)
    y = y * (1.0 + sc_ref[...][:, None, :]) + sh_ref[...][:, None, :]
    return y.reshape(tr, D_MODEL)


def _norm_mod_kernel(x_ref, nw_ref, sc_ref, sh_ref, o_ref):
    o_ref[...] = _norm_mod_body(x_ref, nw_ref, sc_ref, sh_ref).astype(o_ref.dtype)


def _norm_mod_router_kernel(x_ref, nw_ref, sc_ref, sh_ref, r_ref, o_ref, lg_ref):
    hn = _norm_mod_body(x_ref, nw_ref, sc_ref, sh_ref)
    o_ref[...] = hn.astype(o_ref.dtype)
    lg_ref[...] = jnp.dot(hn, r_ref[...], preferred_element_type=f32, precision=lax.Precision.HIGHEST)


def norm_mod(x, nw, sc_g, sh_g, router=None):
    tr = 512
    tg = tr // SUBLANES
    in_specs = [pl.BlockSpec((tr, D_MODEL), lambda i: (i, 0)),
                pl.BlockSpec((1, D_MODEL), lambda i: (0, 0)),
                pl.BlockSpec((tg, D_MODEL), lambda i: (i, 0)),
                pl.BlockSpec((tg, D_MODEL), lambda i: (i, 0))]
    out_spec = pl.BlockSpec((tr, D_MODEL), lambda i: (i, 0))
    out_shape = jax.ShapeDtypeStruct((N_TOK, D_MODEL), bf16)
    if router is None:
        return pl.pallas_call(
            _norm_mod_kernel, grid=(N_TOK // tr,), in_specs=in_specs, out_specs=out_spec,
            out_shape=out_shape, compiler_params=_params(("arbitrary",)), name="norm_mod",
        )(x, nw, sc_g, sh_g)
    return pl.pallas_call(
        _norm_mod_router_kernel, grid=(N_TOK // tr,),
        in_specs=in_specs + [pl.BlockSpec((D_MODEL, LANES), lambda i: (0, 0))],
        out_specs=[out_spec, pl.BlockSpec((tr, LANES), lambda i: (i, 0))],
        out_shape=[out_shape, jax.ShapeDtypeStruct((N_TOK, LANES), f32)],
        compiler_params=_params(("arbitrary",)), name="norm_mod_router",
    )(x, nw, sc_g, sh_g, router)


def _final_norm_kernel(x_ref, nw_ref, o_ref):
    x = x_ref[...]
    o_ref[...] = x * lax.rsqrt(jnp.mean(x * x, axis=-1, keepdims=True) + EPS) * nw_ref[...]


def final_norm(x, nw):
    tr = 512
    return pl.pallas_call(
        _final_norm_kernel, grid=(N_TOK // tr,),
        in_specs=[pl.BlockSpec((tr, D_MODEL), lambda i: (i, 0)),
                  pl.BlockSpec((1, D_MODEL), lambda i: (0, 0))],
        out_specs=pl.BlockSpec((tr, D_MODEL), lambda i: (i, 0)),
        out_shape=jax.ShapeDtypeStruct((N_TOK, D_MODEL), f32),
        compiler_params=_params(("arbitrary",)), name="final_norm",
    )(x, nw)


def _mm_kernel(x_ref, w_ref, o_ref):
    o_ref[...] = jnp.dot(x_ref[...], w_ref[...], preferred_element_type=f32).astype(o_ref.dtype)


def matmul(x, w, tm, tn, out_dtype):
    m, k = x.shape
    n = w.shape[1]
    return pl.pallas_call(
        _mm_kernel, grid=(m // tm, n // tn),
        in_specs=[pl.BlockSpec((tm, k), lambda i, j: (i, 0)),
                  pl.BlockSpec((k, tn), lambda i, j: (0, j))],
        out_specs=pl.BlockSpec((tm, tn), lambda i, j: (i, j)),
        out_shape=jax.ShapeDtypeStruct((m, n), out_dtype),
        compiler_params=_params(("arbitrary", "arbitrary")), name="matmul",
    )(x, w)


def _mm_resid_kernel(y_ref, w_ref, x_ref, g_ref, o_ref, acc_ref, *, nk):
    k = pl.program_id(2)
    part = jnp.dot(y_ref[...], w_ref[...], preferred_element_type=f32)

    @pl.when(k == 0)
    def _():
        acc_ref[...] = part

    @pl.when(k > 0)
    def _():
        acc_ref[...] += part

    @pl.when(k == nk - 1)
    def _():
        tm, tn = acc_ref.shape
        a = acc_ref[...].reshape(tm // SUBLANES, SUBLANES, tn) * g_ref[...][:, None, :]
        o_ref[...] = x_ref[...] + a.reshape(tm, tn)


def matmul_gated_residual(y, w, x, g_g, tm, tn, tk):
    m, kdim = y.shape
    n = w.shape[1]
    nk = kdim // tk
    return pl.pallas_call(
        functools.partial(_mm_resid_kernel, nk=nk), grid=(m // tm, n // tn, nk),
        in_specs=[pl.BlockSpec((tm, tk), lambda i, j, k: (i, k)),
                  pl.BlockSpec((tk, tn), lambda i, j, k: (k, j)),
                  pl.BlockSpec((tm, tn), lambda i, j, k: (i, j)),
                  pl.BlockSpec((tm // SUBLANES, tn), lambda i, j, k: (i, j))],
        out_specs=pl.BlockSpec((tm, tn), lambda i, j, k: (i, j)),
        out_shape=jax.ShapeDtypeStruct((m, n), f32),
        scratch_shapes=[pltpu.VMEM((tm, tn), f32)],
        compiler_params=_params(("arbitrary", "arbitrary", "arbitrary")), name="matmul_gated_residual",
    )(y, w, x, g_g)


def _swiglu_body(x, wg, wu):
    g = jnp.dot(x, wg, preferred_element_type=f32)
    u = jnp.dot(x, wu, preferred_element_type=f32)
    return _silu(g) * u


def _swiglu_kernel(x_ref, wg_ref, wu_ref, o_ref):
    o_ref[...] = _swiglu_body(x_ref[...], wg_ref[...], wu_ref[...]).astype(o_ref.dtype)


def swiglu_up(x, wg, wu, tm, tn):
    m, k = x.shape
    n = wg.shape[1]
    return pl.pallas_call(
        _swiglu_kernel, grid=(m // tm, n // tn),
        in_specs=[pl.BlockSpec((tm, k), lambda i, j: (i, 0)),
                  pl.BlockSpec((k, tn), lambda i, j: (0, j)),
                  pl.BlockSpec((k, tn), lambda i, j: (0, j))],
        out_specs=pl.BlockSpec((tm, tn), lambda i, j: (i, j)),
        out_shape=jax.ShapeDtypeStruct((m, n), bf16),
        compiler_params=_params(("arbitrary", "arbitrary")), name="swiglu_up",
    )(x, wg, wu)


def _moe_up_kernel(m_ref, n_ref, e_ref, v_ref, x_ref, wg_ref, wu_ref, o_ref):
    i = pl.program_id(0)

    @pl.when(v_ref[i] == 1)
    def _():
        o_ref[...] = _swiglu_body(x_ref[...], wg_ref[0], wu_ref[0]).astype(o_ref.dtype)


def moe_up(item_m, item_n, item_e, item_v, xs, wg, wu):
    nt = D_FF_EXPERT // MOE_TN
    n_items = MOE_TILES * nt
    grid_spec = pltpu.PrefetchScalarGridSpec(
        num_scalar_prefetch=4, grid=(n_items,),
        in_specs=[pl.BlockSpec((MOE_TM, D_MODEL), lambda i, m, n, e, v: (m[i], 0)),
                  pl.BlockSpec((1, D_MODEL, MOE_TN), lambda i, m, n, e, v: (e[i], 0, n[i])),
                  pl.BlockSpec((1, D_MODEL, MOE_TN), lambda i, m, n, e, v: (e[i], 0, n[i]))],
        out_specs=pl.BlockSpec((MOE_TM, MOE_TN), lambda i, m, n, e, v: (m[i], n[i])))
    return pl.pallas_call(
        _moe_up_kernel, grid_spec=grid_spec,
        out_shape=jax.ShapeDtypeStruct((MOE_ROWS, D_FF_EXPERT), bf16),
        compiler_params=_params(("arbitrary",)), name="moe_up",
    )(item_m, item_n, item_e, item_v, xs, wg, wu)


def _moe_down_kernel(te_ref, nu_ref, h_ref, wd_ref, o_ref):
    m = pl.program_id(0)
    k = pl.program_id(1)

    @pl.when(m < nu_ref[0])
    def _():
        part = jnp.dot(h_ref[...], wd_ref[0], preferred_element_type=f32)

        @pl.when(k == 0)
        def _():
            o_ref[...] = part

        @pl.when(k > 0)
        def _():
            o_ref[...] += part


def moe_down(tile_e, n_used, h, wd):
    nk = D_FF_EXPERT // MOE_TK

    def mclamp(m, nu):
        return jnp.minimum(m, nu[0] - 1)

    def kclamp(m, k, nu):
        return jnp.where(m < nu[0], k, nk - 1)

    grid_spec = pltpu.PrefetchScalarGridSpec(
        num_scalar_prefetch=2, grid=(MOE_TILES, nk),
        in_specs=[pl.BlockSpec((MOE_TM, MOE_TK), lambda m, k, te, nu: (mclamp(m, nu), kclamp(m, k, nu))),
                  pl.BlockSpec((1, MOE_TK, D_MODEL),
                               lambda m, k, te, nu: (te[mclamp(m, nu)], kclamp(m, k, nu), 0))],
        out_specs=pl.BlockSpec((MOE_TM, D_MODEL), lambda m, k, te, nu: (mclamp(m, nu), 0)))
    return pl.pallas_call(
        _moe_down_kernel, grid_spec=grid_spec,
        out_shape=jax.ShapeDtypeStruct((MOE_ROWS, D_MODEL), f32),
        compiler_params=_params(("arbitrary", "arbitrary")), name="moe_down",
    )(tile_e, n_used, h, wd)


def moe_route(logits):
    top_v, top_i = lax.top_k(logits, TOP_K)
    gates = jax.nn.softmax(top_v, axis=-1)
    e_flat = top_i.reshape(-1).astype(jnp.int32)
    onehot = (e_flat[:, None] == jnp.arange(N_EXPERTS, dtype=jnp.int32)[None, :]).astype(jnp.int32)
    csum = jnp.cumsum(onehot, axis=0)
    counts = csum[-1]
    rank = jnp.sum((csum - onehot) * onehot, axis=-1)
    tiles_e = (counts + MOE_TM - 1) // MOE_TM
    tile_end = jnp.cumsum(tiles_e)
    tile_start = tile_end - tiles_e
    dest = tile_start[e_flat] * MOE_TM + rank
    tok = jnp.arange(TOP_K * N_TOK, dtype=jnp.int32) // TOP_K
    row_src = jnp.zeros((MOE_ROWS,), jnp.int32).at[dest].set(tok)
    n_used = tile_end[-1]
    t_ids = jnp.arange(MOE_TILES, dtype=jnp.int32)
    tile_e = jnp.minimum(jnp.searchsorted(tile_end, t_ids, side="right"), N_EXPERTS - 1).astype(jnp.int32)
    nt = D_FF_EXPERT // MOE_TN
    items_e = tiles_e * nt
    item_end = jnp.cumsum(items_e)
    item_start = item_end - items_e
    total = item_end[-1]
    idx = jnp.arange(MOE_TILES * nt, dtype=jnp.int32)
    valid = idx < total
    idc = jnp.minimum(idx, total - 1)
    ie = jnp.minimum(jnp.searchsorted(item_end, idc, side="right"), N_EXPERTS - 1).astype(jnp.int32)
    local = idc - item_start[ie]
    te = jnp.maximum(tiles_e[ie], 1)
    item_n = (local // te).astype(jnp.int32)
    item_m = (tile_start[ie] + local % te).astype(jnp.int32)
    return dict(gates=gates, dest=dest.reshape(N_TOK, TOP_K), row_src=row_src, tile_e=tile_e,
                n_used=n_used.reshape(1).astype(jnp.int32), item_m=item_m, item_n=item_n, item_e=ie,
                item_v=valid.astype(jnp.int32))


def rmsnorm(x, w):
    xf = x.astype(f32)
    y = xf * lax.rsqrt(jnp.mean(xf * xf, axis=-1, keepdims=True) + EPS)
    return (y * w.astype(f32)).astype(x.dtype)


def l2norm(x):
    return x * lax.rsqrt(jnp.sum(x * x, axis=-1, keepdims=True) + EPS)


def causal_conv(u, buf, w):
    width = w.shape[0]
    t = u.shape[1]
    full = jnp.concatenate([buf.astype(u.dtype), u], axis=1)
    out = sum(full[:, j:j + t] * w[j] for j in range(width))
    return out, full[:, t:]


def diff_attention(q, k, v, qpos, kpos, lam, slopes):
    b, t = q.shape[:2]
    qb = math.gcd(t, Q_BLOCK)
    nb = t // qb
    q_blocks = jnp.moveaxis(q.reshape(b, nb, qb, H_A, 2, DH_A), 1, 0)
    p_blocks = qpos.reshape(nb, qb)
    scale = DH_A ** -0.5

    def one_block(args):
        qblk, pblk = args
        s = jnp.einsum('bqhmd,bshmd->bhmqs', qblk, k).astype(f32) * scale
        dist = (pblk[:, None] - kpos[None, :]).astype(f32)
        s = s - slopes[None, :, None, None, None] * dist[None, None, None]
        s = jnp.where((dist >= 0)[None, None, None], s, -1e30)
        probs = jax.nn.softmax(s, axis=-1)
        amap = probs[:, :, 0] - lam * probs[:, :, 1]
        return jnp.einsum('bhqs,bshe->bqhe', amap.astype(v.dtype), v)

    o = lax.map(one_block, (q_blocks, p_blocks))
    return jnp.moveaxis(o, 0, 1).reshape(b, t, H_A, DV_A)


def gated_delta_rule(q, k, v, g, beta, s0):
    b, t, h, dk = q.shape
    dv = v.shape[-1]
    c = math.gcd(t, DN_CHUNK)
    n = t // c

    def to_chunks(a):
        a = jnp.moveaxis(a, 2, 1)
        a = a.reshape(b, h, n, c, *a.shape[3:])
        return jnp.moveaxis(a, 2, 0)

    qc, kc, vc, gc, bc = (to_chunks(a) for a in (q, k, v, g, beta))
    gcum = jnp.cumsum(gc, axis=-1)
    causal = jnp.tril(jnp.ones((c, c), bool))
    strict = jnp.tril(jnp.ones((c, c), bool), -1)
    diff = gcum[..., :, None] - gcum[..., None, :]
    decay = jnp.where(causal, jnp.exp(jnp.where(causal, diff, 0.0)), 0.0)
    kb = kc * bc[..., None]
    lmat = jnp.where(strict, jnp.einsum('...id,...jd->...ij', kb, kc) * decay, 0.0)
    amat = lmat + jnp.eye(c, dtype=lmat.dtype)
    rhs = jnp.concatenate([vc * bc[..., None], kb * jnp.exp(gcum)[..., None]], axis=-1)
    sol = lax.linalg.triangular_solve(amat, rhs, left_side=True, lower=True, unit_diagonal=True)
    u, w = sol[..., :dv], sol[..., dv:]
    qk = jnp.einsum('...id,...jd->...ij', qc, kc) * decay

    def step(s, xs):
        q_i, k_i, u_i, w_i, qk_i, g_i = xs
        v_new = u_i - jnp.einsum('bhcd,bhde->bhce', w_i, s)
        o_i = (jnp.einsum('bhcd,bhde->bhce', q_i * jnp.exp(g_i)[..., None], s)
               + jnp.einsum('bhij,bhje->bhie', qk_i, v_new))
        g_last = g_i[..., -1:]
        s = (s * jnp.exp(g_last)[..., None]
             + jnp.einsum('bhcd,bhce->bhde', k_i * jnp.exp(g_last - g_i)[..., None], v_new))
        return s, o_i

    s_final, o = lax.scan(step, s0, (qc, kc, u, w, qk, gcum))
    o = jnp.moveaxis(o, 0, 2).reshape(b, h, t, dv)
    return jnp.moveaxis(o, 1, 2), s_final


def linear_scan(a, bx, h0):
    bx = bx.at[:, 0].add(a[:, 0] * h0)

    def combine(lhs, rhs):
        return lhs[0] * rhs[0], rhs[0] * lhs[1] + rhs[1]

    _, hs = lax.associative_scan(combine, (a, bx), axis=1)
    return hs


def mixer_block(proj, l, p, past_k, past_v, conv_b_buf, dn_buf, dn_s, lru_buf, lru_h):
    b, t, _ = proj.shape
    n_past = past_k.shape[1]
    pa = proj[..., OFF_A:OFF_B]
    pb = proj[..., OFF_B:OFF_C]
    pc = proj[..., OFF_C:OFF_D]
    pd = proj[..., OFF_D:OFF_AB]
    a_raw = proj[..., OFF_AB:OFF_AB + H_C]
    b_raw = proj[..., OFF_AB + H_C:OFF_AB + 2 * H_C]

    qa, ka, va = jnp.split(pa, 3, axis=-1)
    qa = qa.reshape(b, t, H_A, 2, DH_A)
    ka = ka.reshape(b, t, H_A, 2, DH_A)
    va = va.reshape(b, t, H_A, DV_A)
    k_all = jnp.concatenate([past_k, ka], axis=1)
    v_all = jnp.concatenate([past_v, va], axis=1)
    kpos = jnp.arange(n_past + t)
    qpos = n_past + jnp.arange(t)
    slopes = jnp.exp2(-8.0 * jnp.arange(1, H_A + 1, dtype=f32) / H_A)
    lam_init = 0.8 - 0.6 * math.exp(-0.3 * l)
    lam = (jnp.exp(jnp.sum(p['lam_q1'][l] * p['lam_k1'][l]).astype(f32))
           - jnp.exp(jnp.sum(p['lam_q2'][l] * p['lam_k2'][l]).astype(f32)) + lam_init)
    oa = diff_attention(qa, k_all, v_all, qpos, kpos, lam, slopes)
    ya = (rmsnorm(oa, p['subln_w'][l]) * (1.0 - lam_init)).reshape(b, t, W_A)

    bg, cg, xt = jnp.split(pb, 3, axis=-1)
    conv_out, new_conv_b = causal_conv(cg * xt, conv_b_buf, p['conv_b_w'][l])
    yb = bg * conv_out

    qkv = pc[..., :3 * W_C]
    z = pc[..., 3 * W_C:4 * W_C]
    qkv_c, new_dn_buf = causal_conv(qkv, dn_buf, p['dn_conv_w'][l])
    qkv_c = jax.nn.silu(qkv_c)
    qd, kd, vd = jnp.split(qkv_c, 3, axis=-1)
    qd = l2norm(qd.reshape(b, t, H_C, DK_C)) * (DK_C ** -0.5)
    kd = l2norm(kd.reshape(b, t, H_C, DK_C))
    vd = vd.reshape(b, t, H_C, DV_C)
    beta = jax.nn.sigmoid(b_raw)
    g = -jnp.exp(p['dn_a_log'][l]) * jax.nn.softplus(a_raw + p['dn_dt_bias'][l])
    oc, new_s = gated_delta_rule(qd, kd, vd, g, beta, dn_s)
    oc = rmsnorm(oc, p['dn_norm_w'][l]) * jax.nn.silu(z.reshape(b, t, H_C, DV_C))
    yc = oc.reshape(b, t, W_C)

    xd, gd = jnp.split(pd, 2, axis=-1)
    xcv, new_lru_buf = causal_conv(xd, lru_buf, p['lru_conv_w'][l])
    xf = xcv + p['lru_conv_b'][l]
    xh = xf.reshape(b, t, H_D, BW_D)
    r = jax.nn.sigmoid(jnp.einsum('bthi,hij->bthj', xh, p['lru_wa'][l]).reshape(b, t, W_D) + p['lru_ba'][l])
    ig = jax.nn.sigmoid(jnp.einsum('bthi,hij->bthj', xh, p['lru_wx'][l]).reshape(b, t, W_D) + p['lru_bx'][l])
    log_a = -RG_C * r * jax.nn.softplus(-p['lru_lambda'][l])
    a = jnp.exp(log_a)
    hseq = linear_scan(a, jnp.sqrt(-jnp.expm1(2.0 * log_a)) * (ig * xf), lru_h)
    yd = hseq * jax.nn.gelu(gd)

    y = jnp.concatenate([ya, yb, yc, yd], axis=-1).astype(bf16)
    return (y, ka, va, new_conv_b, new_dn_buf, new_s, new_lru_buf, hseq[:, -1])


def _group_rows(m):
    return jnp.concatenate([jnp.repeat(m[:BATCH], SEQ // SUBLANES, axis=0), m[BATCH:]], axis=0)


def kernel(x_prompt, x_sample, cache_k, cache_v, state_conv_b, state_dn_conv, state_dn, state_lru_conv, state_lru_h, page_table, c_prompt, c_sample, w_mod, b_mod, norm_mix, norm_ffn, w_in, w_out, lam_q1, lam_k1, lam_q2, lam_k2, subln_w, conv_b_w, dn_conv_w, dn_a_log, dn_dt_bias, dn_norm_w, lru_conv_w, lru_conv_b, lru_wa, lru_ba, lru_wx, lru_bx, lru_lambda, ffn_w_gate, ffn_w_up, ffn_w_down, moe_router, moe_w_gate, moe_w_up, moe_w_down, final_norm_w):
    p = dict(lam_q1=lam_q1, lam_k1=lam_k1, lam_q2=lam_q2, lam_k2=lam_k2, subln_w=subln_w,
             conv_b_w=conv_b_w, dn_conv_w=dn_conv_w, dn_a_log=dn_a_log, dn_dt_bias=dn_dt_bias,
             dn_norm_w=dn_norm_w, lru_conv_w=lru_conv_w, lru_conv_b=lru_conv_b, lru_wa=lru_wa,
             lru_ba=lru_ba, lru_wx=lru_wx, lru_bx=lru_bx, lru_lambda=lru_lambda)
    x = jnp.concatenate([x_prompt.reshape(N_PROMPT, D_MODEL), x_sample.reshape(N_SAMPLE, D_MODEL)], axis=0)
    c_all = jnp.concatenate([c_prompt, c_sample], axis=0)

    zero_cb = jnp.zeros((BATCH, CONV_B - 1, W_B), f32)
    zero_dc = jnp.zeros((BATCH, CONV_C - 1, 3 * W_C), f32)
    zero_ds = jnp.zeros((BATCH, H_C, DK_C, DV_C), f32)
    zero_lc = jnp.zeros((BATCH, CONV_D - 1, W_D), f32)
    zero_lh = jnp.zeros((BATCH, W_D), f32)
    no_k = jnp.zeros((BATCH, 0, H_A, 2, DH_A), f32)
    no_v = jnp.zeros((BATCH, 0, H_A, DV_A), f32)

    outs_p = [[] for _ in range(7)]
    outs_s = [[] for _ in range(7)]
    for l in range(DEPTH):
        mod = modulation(c_all, w_mod[l], b_mod[l][None, :])
        sh1, sc1, g1, sh2, sc2, g2 = (_group_rows(m) for m in jnp.split(mod, N_MOD, axis=-1))

        hn = norm_mod(x, norm_mix[l][None, :], sc1, sh1)
        wl = w_in[l]
        n_cqkvz = 4 * W_C
        w_pack = jnp.concatenate(
            [wl[:, :N_A + N_B + n_cqkvz], wl[:, N_A + N_B + N_C:], wl[:, N_A + N_B + n_cqkvz:N_A + N_B + N_C],
             jnp.zeros((D_MODEL, LANES - 2 * H_C), f32)], axis=1).astype(bf16)
        proj = matmul(hn, w_pack, tm=1088, tn=896, out_dtype=f32)

        past_k = cache_k[l][page_table].reshape(DEC_BATCH, -1, H_A, 2, DH_A)
        past_v = cache_v[l][page_table].reshape(DEC_BATCH, -1, H_A, DV_A)
        res_p = mixer_block(proj[:N_PROMPT].reshape(BATCH, SEQ, N_IN_PAD), l, p, no_k, no_v,
                            zero_cb, zero_dc, zero_ds, zero_lc, zero_lh)
        res_s = mixer_block(proj[N_PROMPT:].reshape(DEC_BATCH, DEC_SEQ, N_IN_PAD), l, p, past_k, past_v,
                            state_conv_b[l], state_dn_conv[l], state_dn[l], state_lru_conv[l], state_lru_h[l])
        y = jnp.concatenate([res_p[0].reshape(N_PROMPT, D_MODEL), res_s[0].reshape(N_SAMPLE, D_MODEL)], axis=0)
        for i in range(7):
            outs_p[i].append(res_p[i + 1])
            outs_s[i].append(res_s[i + 1])

        x = matmul_gated_residual(y, w_out[l].astype(bf16), x, g1, tm=1088, tn=1024, tk=D_MODEL)

        j = l // 2
        if l % 2 == 0:
            hn = norm_mod(x, norm_ffn[l][None, :], sc2, sh2)
            h = swiglu_up(hn, ffn_w_gate[j].astype(bf16), ffn_w_up[j].astype(bf16), tm=1088, tn=512)
            x = matmul_gated_residual(h, ffn_w_down[j].astype(bf16), x, g2, tm=1088, tn=512, tk=D_FF // 2)
        else:
            router = jnp.concatenate([moe_router[j], jnp.zeros((D_MODEL, LANES - N_EXPERTS), f32)], axis=1)
            hn, logits = norm_mod(x, norm_ffn[l][None, :], sc2, sh2, router=router)
            plan = moe_route(logits[:, :N_EXPERTS])
            hn_words = lax.bitcast_convert_type(hn.reshape(N_TOK, D_MODEL // 2, 2), jnp.uint32)
            xs_words = jnp.take(hn_words, plan['row_src'], axis=0)
            xs = lax.bitcast_convert_type(xs_words, bf16).reshape(MOE_ROWS, D_MODEL)
            h = moe_up(plan['item_m'], plan['item_n'], plan['item_e'], plan['item_v'], xs,
                       moe_w_gate[j].astype(bf16), moe_w_up[j].astype(bf16))
            ys = moe_down(plan['tile_e'], plan['n_used'], h, moe_w_down[j].astype(bf16))
            gates = plan['gates']
            dest = plan['dest']
            f = (gates[:, 0:1] * jnp.take(ys, dest[:, 0], axis=0)
                 + gates[:, 1:2] * jnp.take(ys, dest[:, 1], axis=0))
            g2_tok = jnp.repeat(g2, SUBLANES, axis=0)
            x = x + g2_tok * f

    y = final_norm(x, final_norm_w[None, :])
    y_prompt = y[:N_PROMPT].reshape(BATCH, SEQ, D_MODEL)
    y_sample = y[N_PROMPT:].reshape(DEC_BATCH, DEC_SEQ, D_MODEL)
    sp = [jnp.stack(o) for o in outs_p]
    ss = [jnp.stack(o) for o in outs_s]
    return (y_prompt, y_sample, sp[0], sp[1], ss[0], ss[1], sp[2], ss[2], sp[3], ss[3],
            sp[4], ss[4], sp[5], ss[5], sp[6], ss[6])
```

```python
import functools
import math

import jax
import jax.numpy as jnp
from jax import lax
from jax.experimental import pallas as pl
from jax.experimental.pallas import tpu as pltpu

D_MODEL = 2048
BATCH = 8
SEQ = 2048
DEPTH = 2
DEC_BATCH = 128
DEC_SEQ = 8
W_GROUP = D_MODEL // 4
W_A = W_B = W_C = W_D = W_GROUP
H_A = 4
DH_A = W_A // (2 * H_A)
DV_A = 2 * DH_A
Q_BLOCK = 128
CONV_B = 3
H_C = 4
DK_C = W_C // H_C
DV_C = W_C // H_C
CONV_C = 4
DN_CHUNK = 64
H_D = 4
BW_D = W_D // H_D
CONV_D = 4
RG_C = 8.0
D_FF = 5632
N_EXPERTS = 8
TOP_K = 2
D_FF_EXPERT = 7168
N_MOD = 6
EPS = 1e-6
N_A = 3 * W_A
N_B = 3 * W_B
N_C = 4 * W_C + 2 * H_C
N_D = 2 * W_D

N_PROMPT = BATCH * SEQ
N_SAMPLE = DEC_BATCH * DEC_SEQ
N_TOK = N_PROMPT + N_SAMPLE
SUBLANES = 8
N_GROUPS = N_TOK // SUBLANES
LANES = 128

OFF_A = 0
OFF_B = N_A
OFF_C = N_A + N_B
OFF_D = OFF_C + 4 * W_C
OFF_AB = OFF_D + N_D
N_IN_PAD = OFF_AB + LANES

VMEM_LIMIT = 48 * 1024 * 1024
MOE_TM = 512
MOE_TN = 512
MOE_TK = 1024
MOE_TILES = (TOP_K * N_TOK) // MOE_TM + N_EXPERTS
MOE_ROWS = MOE_TILES * MOE_TM
ATT_TQ = 256
LRU_CHUNK = 256
LRU_SPS = 16
DN_SPS = 4

f32 = jnp.float32
bf16 = jnp.bfloat16


def _params(sem):
    return pltpu.CompilerParams(dimension_semantics=sem, vmem_limit_bytes=VMEM_LIMIT)


def _silu(x):
    return x * (1.0 / (1.0 + jnp.exp(-x)))


def _mod_kernel(c_ref, w_ref, b_ref, o_ref):
    c = _silu(c_ref[...]).astype(bf16)
    o_ref[...] = jnp.dot(c, w_ref[...].astype(bf16), preferred_element_type=f32) + b_ref[...]


def modulation(c_all, w, b):
    r = c_all.shape[0]
    n = w.shape[1]
    tn = 1024
    return pl.pallas_call(
        _mod_kernel,
        grid=(n // tn,),
        in_specs=[pl.BlockSpec((r, D_MODEL), lambda j: (0, 0)),
                  pl.BlockSpec((D_MODEL, tn), lambda j: (0, j)),
                  pl.BlockSpec((1, tn), lambda j: (0, j))],
        out_specs=pl.BlockSpec((r, tn), lambda j: (0, j)),
        out_shape=jax.ShapeDtypeStruct((r, n), f32),
        compiler_params=_params(("arbitrary",)),
        name="modulation",
    )(c_all, w, b)


def _norm_mod_body(x_ref, nw_ref, sc_ref, sh_ref):
    x = x_ref[...]
    tr = x.shape[0]
    y = x * lax.rsqrt(jnp.mean(x * x, axis=-1, keepdims=True) + EPS) * nw_ref[...]
    y = y.reshape(tr // SUBLANES, SUBLANES, D_MODEL)
    y = y * (1.0 + sc_ref[...][:, None, :]) + sh_ref[...][:, None, :]
    return y.reshape(tr, D_MODEL)


def _norm_mod_kernel(x_ref, nw_ref, sc_ref, sh_ref, o_ref):
    o_ref[...] = _norm_mod_body(x_ref, nw_ref, sc_ref, sh_ref).astype(o_ref.dtype)


def _norm_mod_router_kernel(x_ref, nw_ref, sc_ref, sh_ref, r_ref, o_ref, lg_ref):
    hn = _norm_mod_body(x_ref, nw_ref, sc_ref, sh_ref)
    o_ref[...] = hn.astype(o_ref.dtype)
    lg_ref[...] = jnp.dot(hn, r_ref[...], preferred_element_type=f32, precision=lax.Precision.HIGHEST)


def norm_mod(x, nw, sc_g, sh_g, router=None):
    tr = 512
    tg = tr // SUBLANES
    in_specs = [pl.BlockSpec((tr, D_MODEL), lambda i: (i, 0)),
                pl.BlockSpec((1, D_MODEL), lambda i: (0, 0)),
                pl.BlockSpec((tg, D_MODEL), lambda i: (i, 0)),
                pl.BlockSpec((tg, D_MODEL), lambda i: (i, 0))]
    out_spec = pl.BlockSpec((tr, D_MODEL), lambda i: (i, 0))
    out_shape = jax.ShapeDtypeStruct((N_TOK, D_MODEL), bf16)
    if router is None:
        return pl.pallas_call(
            _norm_mod_kernel, grid=(N_TOK // tr,), in_specs=in_specs, out_specs=out_spec,
            out_shape=out_shape, compiler_params=_params(("arbitrary",)), name="norm_mod",
        )(x, nw, sc_g, sh_g)
    return pl.pallas_call(
        _norm_mod_router_kernel, grid=(N_TOK // tr,),
        in_specs=in_specs + [pl.BlockSpec((D_MODEL, LANES), lambda i: (0, 0))],
        out_specs=[out_spec, pl.BlockSpec((tr, LANES), lambda i: (i, 0))],
        out_shape=[out_shape, jax.ShapeDtypeStruct((N_TOK, LANES), f32)],
        compiler_params=_params(("arbitrary",)), name="norm_mod_router",
    )(x, nw, sc_g, sh_g, router)


def _final_norm_kernel(x_ref, nw_ref, o_ref):
    x = x_ref[...]
    o_ref[...] = x * lax.rsqrt(jnp.mean(x * x, axis=-1, keepdims=True) + EPS) * nw_ref[...]


def final_norm(x, nw):
    tr = 512
    return pl.pallas_call(
        _final_norm_kernel, grid=(N_TOK // tr,),
        in_specs=[pl.BlockSpec((tr, D_MODEL), lambda i: (i, 0)),
                  pl.BlockSpec((1, D_MODEL), lambda i: (0, 0))],
        out_specs=pl.BlockSpec((tr, D_MODEL), lambda i: (i, 0)),
        out_shape=jax.ShapeDtypeStruct((N_TOK, D_MODEL), f32),
        compiler_params=_params(("arbitrary",)), name="final_norm",
    )(x, nw)


def _mm_kernel(x_ref, w_ref, o_ref):
    o_ref[...] = jnp.dot(x_ref[...], w_ref[...], preferred_element_type=f32).astype(o_ref.dtype)


def matmul(x, w, tm, tn, out_dtype):
    m, k = x.shape
    n = w.shape[1]
    return pl.pallas_call(
        _mm_kernel, grid=(m // tm, n // tn),
        in_specs=[pl.BlockSpec((tm, k), lambda i, j: (i, 0)),
                  pl.BlockSpec((k, tn), lambda i, j: (0, j))],
        out_specs=pl.BlockSpec((tm, tn), lambda i, j: (i, j)),
        out_shape=jax.ShapeDtypeStruct((m, n), out_dtype),
        compiler_params=_params(("arbitrary", "arbitrary")), name="matmul",
    )(x, w)


def _mm_resid_kernel(y_ref, w_ref, x_ref, g_ref, o_ref, acc_ref, *, nk):
    k = pl.program_id(2)
    part = jnp.dot(y_ref[...], w_ref[...], preferred_element_type=f32)

    @pl.when(k == 0)
    def _():
        acc_ref[...] = part

    @pl.when(k > 0)
    def _():
        acc_ref[...] += part

    @pl.when(k == nk - 1)
    def _():
        tm, tn = acc_ref.shape
        a = acc_ref[...].reshape(tm // SUBLANES, SUBLANES, tn) * g_ref[...][:, None, :]
        o_ref[...] = x_ref[...] + a.reshape(tm, tn)


def matmul_gated_residual(y, w, x, g_g, tm, tn, tk):
    m, kdim = y.shape
    n = w.shape[1]
    nk = kdim // tk
    return pl.pallas_call(
        functools.partial(_mm_resid_kernel, nk=nk), grid=(m // tm, n // tn, nk),
        in_specs=[pl.BlockSpec((tm, tk), lambda i, j, k: (i, k)),
                  pl.BlockSpec((tk, tn), lambda i, j, k: (k, j)),
                  pl.BlockSpec((tm, tn), lambda i, j, k: (i, j)),
                  pl.BlockSpec((tm // SUBLANES, tn), lambda i, j, k: (i, j))],
        out_specs=pl.BlockSpec((tm, tn), lambda i, j, k: (i, j)),
        out_shape=jax.ShapeDtypeStruct((m, n), f32),
        scratch_shapes=[pltpu.VMEM((tm, tn), f32)],
        compiler_params=_params(("arbitrary", "arbitrary", "arbitrary")), name="matmul_gated_residual",
    )(y, w, x, g_g)


def _swiglu_body(x, wg, wu):
    g = jnp.dot(x, wg, preferred_element_type=f32)
    u = jnp.dot(x, wu, preferred_element_type=f32)
    return _silu(g) * u


def _swiglu_kernel(x_ref, wg_ref, wu_ref, o_ref):
    o_ref[...] = _swiglu_body(x_ref[...], wg_ref[...], wu_ref[...]).astype(o_ref.dtype)


def swiglu_up(x, wg, wu, tm, tn):
    m, k = x.shape
    n = wg.shape[1]
    return pl.pallas_call(
        _swiglu_kernel, grid=(m // tm, n // tn),
        in_specs=[pl.BlockSpec((tm, k), lambda i, j: (i, 0)),
                  pl.BlockSpec((k, tn), lambda i, j: (0, j)),
                  pl.BlockSpec((k, tn), lambda i, j: (0, j))],
        out_specs=pl.BlockSpec((tm, tn), lambda i, j: (i, j)),
        out_shape=jax.ShapeDtypeStruct((m, n), bf16),
        compiler_params=_params(("arbitrary", "arbitrary")), name="swiglu_up",
    )(x, wg, wu)


def _moe_up_kernel(m_ref, n_ref, e_ref, v_ref, x_ref, wg_ref, wu_ref, o_ref):
    i = pl.program_id(0)

    @pl.when(v_ref[i] == 1)
    def _():
        o_ref[...] = _swiglu_body(x_ref[...], wg_ref[0], wu_ref[0]).astype(o_ref.dtype)


def moe_up(item_m, item_n, item_e, item_v, xs, wg, wu):
    nt = D_FF_EXPERT // MOE_TN
    n_items = MOE_TILES * nt
    grid_spec = pltpu.PrefetchScalarGridSpec(
        num_scalar_prefetch=4, grid=(n_items,),
        in_specs=[pl.BlockSpec((MOE_TM, D_MODEL), lambda i, m, n, e, v: (m[i], 0)),
                  pl.BlockSpec((1, D_MODEL, MOE_TN), lambda i, m, n, e, v: (e[i], 0, n[i])),
                  pl.BlockSpec((1, D_MODEL, MOE_TN), lambda i, m, n, e, v: (e[i], 0, n[i]))],
        out_specs=pl.BlockSpec((MOE_TM, MOE_TN), lambda i, m, n, e, v: (m[i], n[i])))
    return pl.pallas_call(
        _moe_up_kernel, grid_spec=grid_spec,
        out_shape=jax.ShapeDtypeStruct((MOE_ROWS, D_FF_EXPERT), bf16),
        compiler_params=_params(("arbitrary",)), name="moe_up",
    )(item_m, item_n, item_e, item_v, xs, wg, wu)


def _moe_down_kernel(te_ref, nu_ref, h_ref, wd_ref, o_ref):
    m = pl.program_id(0)
    k = pl.program_id(1)

    @pl.when(m < nu_ref[0])
    def _():
        part = jnp.dot(h_ref[...], wd_ref[0], preferred_element_type=f32)

        @pl.when(k == 0)
        def _():
            o_ref[...] = part

        @pl.when(k > 0)
        def _():
            o_ref[...] += part


def moe_down(tile_e, n_used, h, wd):
    nk = D_FF_EXPERT // MOE_TK

    def mclamp(m, nu):
        return jnp.minimum(m, nu[0] - 1)

    def kclamp(m, k, nu):
        return jnp.where(m < nu[0], k, nk - 1)

    grid_spec = pltpu.PrefetchScalarGridSpec(
        num_scalar_prefetch=2, grid=(MOE_TILES, nk),
        in_specs=[pl.BlockSpec((MOE_TM, MOE_TK), lambda m, k, te, nu: (mclamp(m, nu), kclamp(m, k, nu))),
                  pl.BlockSpec((1, MOE_TK, D_MODEL),
                               lambda m, k, te, nu: (te[mclamp(m, nu)], kclamp(m, k, nu), 0))],
        out_specs=pl.BlockSpec((MOE_TM, D_MODEL), lambda m, k, te, nu: (mclamp(m, nu), 0)))
    return pl.pallas_call(
        _moe_down_kernel, grid_spec=grid_spec,
        out_shape=jax.ShapeDtypeStruct((MOE_ROWS, D_MODEL), f32),
        compiler_params=_params(("arbitrary", "arbitrary")), name="moe_down",
    )(tile_e, n_used, h, wd)


def moe_route(logits):
    top_v, top_i = lax.top_k(logits, TOP_K)
    gates = jax.nn.softmax(top_v, axis=-1)
    e_flat = top_i.reshape(-1).astype(jnp.int32)
    onehot = (e_flat[:, None] == jnp.arange(N_EXPERTS, dtype=jnp.int32)[None, :]).astype(jnp.int32)
    csum = jnp.cumsum(onehot, axis=0)
    counts = csum[-1]
    rank = jnp.sum((csum - onehot) * onehot, axis=-1)
    tiles_e = (counts + MOE_TM - 1) // MOE_TM
    tile_end = jnp.cumsum(tiles_e)
    tile_start = tile_end - tiles_e
    dest = tile_start[e_flat] * MOE_TM + rank
    tok = jnp.arange(TOP_K * N_TOK, dtype=jnp.int32) // TOP_K
    row_src = jnp.zeros((MOE_ROWS,), jnp.int32).at[dest].set(tok)
    n_used = tile_end[-1]
    t_ids = jnp.arange(MOE_TILES, dtype=jnp.int32)
    tile_e = jnp.minimum(jnp.searchsorted(tile_end, t_ids, side="right"), N_EXPERTS - 1).astype(jnp.int32)
    nt = D_FF_EXPERT // MOE_TN
    items_e = tiles_e * nt
    item_end = jnp.cumsum(items_e)
    item_start = item_end - items_e
    total = item_end[-1]
    idx = jnp.arange(MOE_TILES * nt, dtype=jnp.int32)
    valid = idx < total
    idc = jnp.minimum(idx, total - 1)
    ie = jnp.minimum(jnp.searchsorted(item_end, idc, side="right"), N_EXPERTS - 1).astype(jnp.int32)
    local = idc - item_start[ie]
    te = jnp.maximum(tiles_e[ie], 1)
    item_n = (local // te).astype(jnp.int32)
    item_m = (tile_start[ie] + local % te).astype(jnp.int32)
    return dict(gates=gates, dest=dest.reshape(N_TOK, TOP_K), row_src=row_src, tile_e=tile_e,
                n_used=n_used.reshape(1).astype(jnp.int32), item_m=item_m, item_n=item_n, item_e=ie,
                item_v=valid.astype(jnp.int32))


def _smem_spec():
    return pl.BlockSpec(memory_space=pltpu.SMEM)


def _subln(o, w, gain):
    return o * lax.rsqrt(jnp.mean(o * o, axis=-1, keepdims=True) + EPS) * w * gain


def _attn_prompt_kernel(lam_ref, slope_ref, q_ref, k_ref, v_ref, w_ref, o_ref, m_sc, l_sc, acc_sc, *, tq, gain):
    h = pl.program_id(1)
    qi = pl.program_id(2)
    lane = lax.broadcasted_iota(jnp.int32, (tq, DV_A), 1)
    q = q_ref[...] * (DH_A ** -0.5)
    qs = jnp.concatenate([jnp.where(lane < DH_A, q, 0.0), jnp.where(lane >= DH_A, q, 0.0)], axis=0).astype(bf16)
    slope = slope_ref[h]
    m_sc[...] = jnp.full(m_sc.shape, -jnp.inf, f32)
    l_sc[...] = jnp.zeros(l_sc.shape, f32)
    acc_sc[...] = jnp.zeros(acc_sc.shape, f32)
    row = lax.broadcasted_iota(jnp.int32, (2 * tq, tq), 0)
    col = lax.broadcasted_iota(jnp.int32, (2 * tq, tq), 1)
    qpos = qi * tq + jnp.where(row >= tq, row - tq, row)

    def body(j, carry):
        off = pl.multiple_of(j * tq, tq)
        kc = k_ref[pl.ds(off, tq), :].astype(bf16)
        vc = v_ref[pl.ds(off, tq), :].astype(bf16)
        s = lax.dot_general(qs, kc, (((1,), (1,)), ((), ())), preferred_element_type=f32)
        dist = (qpos - (col + off)).astype(f32)
        s = jnp.where(dist >= 0, s - slope * dist, -1e30)
        m_prev = m_sc[...]
        m_new = jnp.maximum(m_prev, jnp.max(s, axis=-1, keepdims=True))
        alpha = jnp.exp(m_prev - m_new)
        pr = jnp.exp(s - m_new)
        l_sc[...] = alpha * l_sc[...] + jnp.sum(pr, axis=-1, keepdims=True)
        acc_sc[...] = alpha * acc_sc[...] + jnp.dot(pr.astype(bf16), vc, preferred_element_type=f32)
        m_sc[...] = m_new
        return carry

    lax.fori_loop(0, qi + 1, body, 0)
    o = acc_sc[...] / l_sc[...]
    o = o[:tq] - lam_ref[0] * o[tq:]
    o_ref[...] = _subln(o, w_ref[...], gain).astype(o_ref.dtype)


def attn_prompt(proj, lam, slopes, subln_w, gain, batch, seq, tq):
    nq = seq // tq
    return pl.pallas_call(
        functools.partial(_attn_prompt_kernel, tq=tq, gain=gain),
        grid=(batch, H_A, nq),
        in_specs=[_smem_spec(), _smem_spec(),
                  pl.BlockSpec((tq, DV_A), lambda b, h, qi: (b * nq + qi, h)),
                  pl.BlockSpec((seq, DV_A), lambda b, h, qi: (b, H_A + h)),
                  pl.BlockSpec((seq, DV_A), lambda b, h, qi: (b, 2 * H_A + h)),
                  pl.BlockSpec((1, DV_A), lambda b, h, qi: (0, 0))],
        out_specs=pl.BlockSpec((tq, DV_A), lambda b, h, qi: (b * nq + qi, h)),
        out_shape=jax.ShapeDtypeStruct((batch * seq, W_A), bf16),
        scratch_shapes=[pltpu.VMEM((2 * tq, 1), f32), pltpu.VMEM((2 * tq, 1), f32),
                        pltpu.VMEM((2 * tq, DV_A), f32)],
        compiler_params=_params(("arbitrary", "arbitrary", "arbitrary")), name="attn_prompt",
    )(lam, slopes, proj, proj, proj, subln_w)


def _attn_sample_kernel(pt_ref, lam_ref, slope_ref, q_ref, kn_ref, vn_ref, w_ref, *rest, t, n_pages, page, gain):
    k_refs = rest[:n_pages]
    v_refs = rest[n_pages:2 * n_pages]
    o_ref = rest[2 * n_pages]
    s_sc = rest[2 * n_pages + 1]
    n_rep = 2 * H_A
    rows = n_rep * t
    n_past = n_pages * page
    q = q_ref[...] * (DH_A ** -0.5)
    q_rep = jnp.concatenate([q] * n_rep, axis=0)
    r_grp = lax.broadcasted_iota(jnp.int32, (rows, W_A), 0) // t
    c_grp = lax.broadcasted_iota(jnp.int32, (rows, W_A), 1) // DH_A
    qbd = jnp.where(r_grp == c_grp, q_rep, 0.0).astype(bf16)
    for j in range(n_pages):
        s_sc[:, j * page:(j + 1) * page] = jnp.dot(qbd, k_refs[j][0, 0].astype(bf16), preferred_element_type=f32)
    s_new = lax.dot_general(qbd, kn_ref[...].astype(bf16), (((1,), (1,)), ((), ())), preferred_element_type=f32)

    row1 = lax.broadcasted_iota(jnp.int32, (rows, 1), 0)
    head1 = row1 // (2 * t)
    slope = jnp.where(head1 == 0, slope_ref[0],
                      jnp.where(head1 == 1, slope_ref[1], jnp.where(head1 == 2, slope_ref[2], slope_ref[3])))
    t_row = (row1 % t)
    kpos = lax.broadcasted_iota(jnp.int32, (rows, n_past), 1)
    s_past = s_sc[...] - slope * (n_past + t_row - kpos).astype(f32)
    dist_new = (t_row - lax.broadcasted_iota(jnp.int32, (rows, t), 1)).astype(f32)
    s_new = jnp.where(dist_new >= 0, s_new - slope * dist_new, -1e30)
    m = jnp.maximum(jnp.max(s_past, axis=-1, keepdims=True), jnp.max(s_new, axis=-1, keepdims=True))
    p_past = jnp.exp(s_past - m)
    p_new = jnp.exp(s_new - m)
    denom = jnp.sum(p_past, axis=-1, keepdims=True) + jnp.sum(p_new, axis=-1, keepdims=True)
    p_past = p_past.astype(bf16)
    p_new = p_new.astype(bf16)
    lam = lam_ref[0]
    w = w_ref[...]
    for h in range(H_A):
        vh = jnp.concatenate([v_refs[j][0, 0, pl.ds(h, page, stride=H_A), :] for j in range(n_pages)], axis=0)
        rs = slice(h * 2 * t, (h + 1) * 2 * t)
        oh = jnp.dot(p_past[rs], vh.astype(bf16), preferred_element_type=f32)
        oh = oh + jnp.dot(p_new[rs], vn_ref[:, h * DV_A:(h + 1) * DV_A].astype(bf16), preferred_element_type=f32)
        oh = oh / denom[rs]
        o = oh[:t] - lam * oh[t:]
        o_ref[:, h * DV_A:(h + 1) * DV_A] = _subln(o, w, gain).astype(o_ref.dtype)


def attn_sample(proj, ckt, cvr, layer, page_table, lam, slopes, subln_w, gain, row0, nseq, t):
    n_pages = page_table.shape[1]
    page = ckt.shape[-1]
    rb0 = row0 // t

    def kv_spec(j):
        return pl.BlockSpec((1, 1, W_A, page), lambda b, pt, j=j: (layer, pt[b * n_pages + j], 0, 0))

    in_specs = ([_smem_spec(), _smem_spec(),
                 pl.BlockSpec((t, W_A), lambda b, pt: (rb0 + b, 0)),
                 pl.BlockSpec((t, W_A), lambda b, pt: (rb0 + b, 1)),
                 pl.BlockSpec((t, W_A), lambda b, pt: (rb0 + b, 2)),
                 pl.BlockSpec((1, DV_A), lambda b, pt: (0, 0))]
                + [kv_spec(j) for j in range(n_pages)] + [kv_spec(j) for j in range(n_pages)])
    grid_spec = pltpu.PrefetchScalarGridSpec(
        num_scalar_prefetch=1, grid=(nseq,), in_specs=in_specs,
        out_specs=pl.BlockSpec((t, W_A), lambda b, pt: (b, 0)),
        scratch_shapes=[pltpu.VMEM((2 * H_A * t, n_pages * page), f32)])
    return pl.pallas_call(
        functools.partial(_attn_sample_kernel, t=t, n_pages=n_pages, page=page, gain=gain),
        grid_spec=grid_spec, out_shape=jax.ShapeDtypeStruct((nseq * t, W_A), bf16),
        compiler_params=_params(("arbitrary",)), name="attn_sample",
    )(page_table.reshape(-1), lam, slopes, proj, proj, proj, subln_w, *([ckt] * n_pages), *([cvr] * n_pages))


def _conv_step(f_sc, x3, w_ref, buf0_ref, buf_out, width, chunk, first):
    lo = SUBLANES - (width - 1)

    @pl.when(first)
    def _():
        f_sc[:, 0:SUBLANES, :] = jnp.zeros((f_sc.shape[0], SUBLANES, f_sc.shape[2]), f32)
        f_sc[:, lo:SUBLANES, :] = buf0_ref[...]

    f_sc[:, SUBLANES:SUBLANES + chunk, :] = x3
    out = w_ref[0:1, :][None] * f_sc[:, lo:lo + chunk, :]
    for j in range(1, width):
        out = out + w_ref[j:j + 1, :][None] * f_sc[:, lo + j:lo + j + chunk, :]
    buf_out[...] = f_sc[:, chunk + lo:chunk + SUBLANES, :]
    f_sc[:, 0:SUBLANES, :] = f_sc[:, chunk:chunk + SUBLANES, :]
    return out


def _sigmoid(x):
    return 1.0 / (1.0 + jnp.exp(-x))


def _gelu_tanh(x):
    return 0.5 * x * (1.0 + jnp.tanh(0.7978845608028654 * (x + 0.044715 * (x * x * x))))


def _conv_lru_kernel(pb_ref, pd_ref, cb0_ref, lb0_ref, h0_ref, wb_ref, wd_ref, bd_ref, wa_ref, ba_ref, wx_ref,
                     bx_ref, sp_ref, yb_ref, yd_ref, cb_out, lb_out, h_out, fb_sc, fd_sc, h_sc, *, sps, chunk):
    first = pl.program_id(1) == 0
    rows = sps * chunk
    pb = pb_ref[...]
    bg, cg, xt = pb[:, :W_B], pb[:, W_B:2 * W_B], pb[:, 2 * W_B:]
    conv_b = _conv_step(fb_sc, (cg * xt).reshape(sps, chunk, W_B), wb_ref, cb0_ref, cb_out, CONV_B, chunk, first)
    yb_ref[...] = (bg * conv_b.reshape(rows, W_B)).astype(yb_ref.dtype)

    pd = pd_ref[...]
    xd, gd = pd[:, :W_D], pd[:, W_D:]
    conv_d = _conv_step(fd_sc, xd.reshape(sps, chunk, W_D), wd_ref, lb0_ref, lb_out, CONV_D, chunk, first)
    xf = conv_d.reshape(rows, W_D) + bd_ref[...]
    xfb = xf.astype(bf16)
    ra = jnp.concatenate([jnp.dot(xfb[:, h * BW_D:(h + 1) * BW_D], wa_ref[h].astype(bf16),
                                  preferred_element_type=f32) for h in range(H_D)], axis=-1)
    rx = jnp.concatenate([jnp.dot(xfb[:, h * BW_D:(h + 1) * BW_D], wx_ref[h].astype(bf16),
                                  preferred_element_type=f32) for h in range(H_D)], axis=-1)
    r = _sigmoid(ra + ba_ref[...])
    ig = _sigmoid(rx + bx_ref[...])
    log_a = -RG_C * r * sp_ref[...]
    a = jnp.exp(log_a)
    th = jnp.tanh(log_a)
    bx = jnp.sqrt(-2.0 * th / (1.0 - th)) * (ig * xf)

    pos = lax.broadcasted_iota(jnp.int32, (rows, W_D), 0) % chunk
    step = 1
    while step < chunk:
        a_s = pltpu.roll(a, step, 0)
        b_s = pltpu.roll(bx, step, 0)
        live = pos >= step
        bx = jnp.where(live, a * b_s + bx, bx)
        a = jnp.where(live, a * a_s, a)
        step *= 2

    @pl.when(first)
    def _():
        h_sc[...] = h0_ref[...]

    h_in = jnp.broadcast_to(h_sc[...], (sps, chunk, W_D)).reshape(rows, W_D)
    hs = bx + a * h_in
    h_last = hs.reshape(sps, chunk, W_D)[:, chunk - 1:chunk, :]
    h_sc[...] = h_last
    h_out[...] = h_last
    yd_ref[...] = (hs * _gelu_tanh(gd)).astype(yd_ref.dtype)


def conv_lru(proj, row0, nseq, t, sps, chunk, conv_b_buf, lru_buf, lru_h, wb, wd, bd, wa, ba, wx, bx, sp):
    rows = sps * chunk
    nc = t // chunk
    rb0 = row0 // rows

    def rmap(s, c):
        return rb0 + s * nc + c

    def full(shape):
        return pl.BlockSpec(shape, lambda s, c: (0,) * len(shape))

    def per_seq(shape):
        return pl.BlockSpec((sps,) + shape, lambda s, c: (s,) + (0,) * len(shape))

    in_specs = [pl.BlockSpec((rows, N_B), lambda s, c: (rmap(s, c), OFF_B // N_B)),
                pl.BlockSpec((rows, N_D), lambda s, c: (rmap(s, c), OFF_D // N_D)),
                per_seq((CONV_B - 1, W_B)), per_seq((CONV_D - 1, W_D)), per_seq((1, W_D)),
                full((CONV_B, W_B)), full((CONV_D, W_D)), full((1, W_D)),
                full((H_D, BW_D, BW_D)), full((1, W_D)), full((H_D, BW_D, BW_D)), full((1, W_D)), full((1, W_D))]
    out_specs = [pl.BlockSpec((rows, W_B), lambda s, c: (s * nc + c, 0)),
                 pl.BlockSpec((rows, W_D), lambda s, c: (s * nc + c, 0)),
                 per_seq((CONV_B - 1, W_B)), per_seq((CONV_D - 1, W_D)), per_seq((1, W_D))]
    out_shape = [jax.ShapeDtypeStruct((nseq * t, W_B), bf16), jax.ShapeDtypeStruct((nseq * t, W_D), bf16),
                 jax.ShapeDtypeStruct((nseq, CONV_B - 1, W_B), f32),
                 jax.ShapeDtypeStruct((nseq, CONV_D - 1, W_D), f32),
                 jax.ShapeDtypeStruct((nseq, 1, W_D), f32)]
    return pl.pallas_call(
        functools.partial(_conv_lru_kernel, sps=sps, chunk=chunk),
        grid=(nseq // sps, nc), in_specs=in_specs, out_specs=out_specs, out_shape=out_shape,
        scratch_shapes=[pltpu.VMEM((sps, chunk + SUBLANES, W_B), f32), pltpu.VMEM((sps, chunk + SUBLANES, W_D), f32),
                        pltpu.VMEM((sps, 1, W_D), f32)],
        compiler_params=_params(("arbitrary", "arbitrary")), name="conv_lru",
    )(proj, proj, conv_b_buf, lru_buf, lru_h, wb, wd, bd, wa, ba, wx, bx, sp)


_HI = lax.Precision.HIGHEST


def _dot_nt(a, b, precision=None):
    return lax.dot_general(a, b, (((1,), (1,)), ((), ())), preferred_element_type=f32, precision=precision)


def _dot_tn(a, b):
    return lax.dot_general(a, b, (((0,), (0,)), ((), ())), preferred_element_type=f32)


def _unit_lower_inverse(a_strict, chunk):
    n = -a_strict
    eye = (lax.broadcasted_iota(jnp.int32, (chunk, chunk), 0)
           == lax.broadcasted_iota(jnp.int32, (chunk, chunk), 1)).astype(f32)
    inv = eye + n
    span = 2
    while span < chunk:
        n = jnp.dot(n, n, preferred_element_type=f32, precision=_HI)
        inv = inv + jnp.dot(inv, n, preferred_element_type=f32, precision=_HI)
        span *= 2
    return inv


def _deltanet_kernel(qkv_ref, z_ref, ab_ref, buf0_ref, s0_ref, wc_ref, nega_ref, dtb_ref, nw_ref,
                     y_ref, buf_out, s_out, f_sc, s_sc, *, sps, chunk):
    first = pl.program_id(1) == 0
    rows = sps * chunk
    wq = 3 * W_C
    conv = _conv_step(f_sc, qkv_ref[...].reshape(sps, chunk, wq), wc_ref, buf0_ref, buf_out, CONV_C, chunk, first)
    conv = conv.reshape(rows, wq)
    qkv = conv * _sigmoid(conv)

    @pl.when(first)
    def _():
        s_sc[...] = s0_ref[...]

    ab = ab_ref[...]
    xg = ab + dtb_ref[...]
    g_all = nega_ref[...] * (jnp.maximum(xg, 0.0) + jnp.log1p(jnp.exp(-jnp.abs(xg))))
    beta_all = _sigmoid(ab)
    z = z_ref[...]
    ri = lax.broadcasted_iota(jnp.int32, (chunk, chunk), 0)
    ci = lax.broadcasted_iota(jnp.int32, (chunk, chunk), 1)
    causal = ri >= ci
    strict = ri > ci
    tri = causal.astype(f32)
    lane = lax.broadcasted_iota(jnp.int32, (chunk, LANES), 1)
    nw = nw_ref[...]
    for i in range(sps):
        rs = slice(i * chunk, (i + 1) * chunk)
        gc = jnp.dot(tri, g_all[rs], preferred_element_type=f32, precision=_HI)
        for h in range(H_C):
            q = qkv[rs, h * DK_C:(h + 1) * DK_C]
            k = qkv[rs, W_C + h * DK_C:W_C + (h + 1) * DK_C]
            v = qkv[rs, 2 * W_C + h * DV_C:2 * W_C + (h + 1) * DV_C]
            q = q * lax.rsqrt(jnp.sum(q * q, axis=-1, keepdims=True) + EPS) * (DK_C ** -0.5)
            k = k * lax.rsqrt(jnp.sum(k * k, axis=-1, keepdims=True) + EPS)
            beta = beta_all[rs, H_C + h:H_C + h + 1]
            gcol = gc[:, h:h + 1]
            grow = _dot_nt((lane == h).astype(f32), gc, precision=_HI)
            decay = jnp.where(causal, jnp.exp(jnp.where(causal, gcol - grow, 0.0)), 0.0)
            kb = k * beta
            a_strict = jnp.where(strict, _dot_nt(kb, k) * decay, 0.0)
            inv = _unit_lower_inverse(a_strict, chunk)
            eg = jnp.exp(gcol)
            u = jnp.dot(inv, v * beta, preferred_element_type=f32, precision=_HI)
            w = jnp.dot(inv, kb * eg, preferred_element_type=f32, precision=_HI)
            s = s_sc[i, h]
            v_new = u - jnp.dot(w, s, preferred_element_type=f32)
            qk = _dot_nt(q, k) * decay
            o = jnp.dot(q * eg, s, preferred_element_type=f32) + jnp.dot(qk, v_new, preferred_element_type=f32)
            g_last = gcol[chunk - 1:chunk, :]
            s_new = s * jnp.exp(g_last) + _dot_tn(k * jnp.exp(g_last - gcol), v_new)
            s_sc[i, h] = s_new
            s_out[i, h] = s_new
            zh = z[rs, h * DV_C:(h + 1) * DV_C]
            o = o * lax.rsqrt(jnp.mean(o * o, axis=-1, keepdims=True) + EPS) * nw
            y_ref[rs, h * DV_C:(h + 1) * DV_C] = (o * (zh * _sigmoid(zh))).astype(y_ref.dtype)


def deltanet(proj, row0, nseq, t, sps, chunk, dn_buf, dn_s, wc, nega, dtb, nw):
    rows = sps * chunk
    nc = t // chunk
    rb0 = row0 // rows
    wq = 3 * W_C

    def rmap(s, c):
        return rb0 + s * nc + c

    def full(shape):
        return pl.BlockSpec(shape, lambda s, c: (0,) * len(shape))

    def per_seq(shape):
        return pl.BlockSpec((sps,) + shape, lambda s, c: (s,) + (0,) * len(shape))

    in_specs = [pl.BlockSpec((rows, wq), lambda s, c: (rmap(s, c), OFF_C // wq)),
                pl.BlockSpec((rows, W_C), lambda s, c: (rmap(s, c), (OFF_C + wq) // W_C)),
                pl.BlockSpec((rows, LANES), lambda s, c: (rmap(s, c), OFF_AB // LANES)),
                per_seq((CONV_C - 1, wq)), per_seq((H_C, DK_C, DV_C)),
                full((CONV_C, wq)), full((1, LANES)), full((1, LANES)), full((1, DV_C))]
    out_specs = [pl.BlockSpec((rows, W_C), lambda s, c: (s * nc + c, 0)),
                 per_seq((CONV_C - 1, wq)), per_seq((H_C, DK_C, DV_C))]
    out_shape = [jax.ShapeDtypeStruct((nseq * t, W_C), bf16),
                 jax.ShapeDtypeStruct((nseq, CONV_C - 1, wq), f32),
                 jax.ShapeDtypeStruct((nseq, H_C, DK_C, DV_C), f32)]
    return pl.pallas_call(
        functools.partial(_deltanet_kernel, sps=sps, chunk=chunk),
        grid=(nseq // sps, nc), in_specs=in_specs, out_specs=out_specs, out_shape=out_shape,
        scratch_shapes=[pltpu.VMEM((sps, chunk + SUBLANES, wq), f32), pltpu.VMEM((sps, H_C, DK_C, DV_C), f32)],
        compiler_params=_params(("arbitrary", "arbitrary")), name="deltanet",
    )(proj, proj, proj, dn_buf, dn_s, wc, nega, dtb, nw)


def _group_rows(m):
    return jnp.concatenate([jnp.repeat(m[:BATCH], SEQ // SUBLANES, axis=0), m[BATCH:]], axis=0)


def _lane_row(v):
    return jnp.concatenate([v.astype(f32), jnp.zeros((LANES - v.shape[0],), f32)])[None, :]


def kernel(x_prompt, x_sample, cache_k, cache_v, state_conv_b, state_dn_conv, state_dn, state_lru_conv, state_lru_h, page_table, c_prompt, c_sample, w_mod, b_mod, norm_mix, norm_ffn, w_in, w_out, lam_q1, lam_k1, lam_q2, lam_k2, subln_w, conv_b_w, dn_conv_w, dn_a_log, dn_dt_bias, dn_norm_w, lru_conv_w, lru_conv_b, lru_wa, lru_ba, lru_wx, lru_bx, lru_lambda, ffn_w_gate, ffn_w_up, ffn_w_down, moe_router, moe_w_gate, moe_w_up, moe_w_down, final_norm_w):
    x = jnp.concatenate([x_prompt.reshape(N_PROMPT, D_MODEL), x_sample.reshape(N_SAMPLE, D_MODEL)], axis=0)
    c_all = jnp.concatenate([c_prompt, c_sample], axis=0)
    n_pool, page = cache_k.shape[1], cache_k.shape[2]
    ckt = jnp.transpose(cache_k, (0, 1, 3, 4, 5, 2)).reshape(DEPTH, n_pool, W_A, page)
    cvr = cache_v.reshape(DEPTH, n_pool, page * H_A, DV_A)
    slopes = jnp.exp2(-8.0 * jnp.arange(1, H_A + 1, dtype=f32) / H_A)

    zero_cb = jnp.zeros((BATCH, CONV_B - 1, W_B), f32)
    zero_dc = jnp.zeros((BATCH, CONV_C - 1, 3 * W_C), f32)
    zero_ds = jnp.zeros((BATCH, H_C, DK_C, DV_C), f32)
    zero_lc = jnp.zeros((BATCH, CONV_D - 1, W_D), f32)
    zero_lh = jnp.zeros((BATCH, 1, W_D), f32)

    ks_p, vs_p, ks_s, vs_s = [], [], [], []
    st_p = [[] for _ in range(5)]
    st_s = [[] for _ in range(5)]
    for l in range(DEPTH):
        mod = modulation(c_all, w_mod[l], b_mod[l][None, :])
        sh1, sc1, g1, sh2, sc2, g2 = (_group_rows(m) for m in jnp.split(mod, N_MOD, axis=-1))

        hn = norm_mod(x, norm_mix[l][None, :], sc1, sh1)
        wl = w_in[l]
        n_cqkvz = 4 * W_C
        w_pack = jnp.concatenate(
            [wl[:, :N_A + N_B + n_cqkvz], wl[:, N_A + N_B + N_C:], wl[:, N_A + N_B + n_cqkvz:N_A + N_B + N_C],
             jnp.zeros((D_MODEL, LANES - 2 * H_C), f32)], axis=1).astype(bf16)
        proj = matmul(hn, w_pack, tm=1088, tn=896, out_dtype=f32)

        lam_init = 0.8 - 0.6 * math.exp(-0.3 * l)
        lam = (jnp.exp(jnp.sum(lam_q1[l] * lam_k1[l])) - jnp.exp(jnp.sum(lam_q2[l] * lam_k2[l])) + lam_init)
        lam = lam.astype(f32).reshape(1)
        gain = 1.0 - lam_init
        sw = subln_w[l][None, :]
        ya_p = attn_prompt(proj, lam, slopes, sw, gain, BATCH, SEQ, ATT_TQ)
        ya_s = attn_sample(proj, ckt, cvr, l, page_table, lam, slopes, sw, gain, N_PROMPT, DEC_BATCH, DEC_SEQ)

        sp = jax.nn.softplus(-lru_lambda[l])[None, :]
        lru_args = (conv_b_w[l], lru_conv_w[l], lru_conv_b[l][None, :], lru_wa[l], lru_ba[l][None, :],
                    lru_wx[l], lru_bx[l][None, :], sp)
        yb_p, yd_p, cb_p, lb_p, lh_p = conv_lru(proj, 0, BATCH, SEQ, 1, LRU_CHUNK, zero_cb, zero_lc, zero_lh,
                                                *lru_args)
        yb_s, yd_s, cb_s, lb_s, lh_s = conv_lru(proj, N_PROMPT, DEC_BATCH, DEC_SEQ, LRU_SPS, DEC_SEQ,
                                                state_conv_b[l], state_lru_conv[l], state_lru_h[l][:, None, :],
                                                *lru_args)

        dn_args = (dn_conv_w[l], _lane_row(-jnp.exp(dn_a_log[l])), _lane_row(dn_dt_bias[l]), dn_norm_w[l][None, :])
        yc_p, dc_p, ds_p = deltanet(proj, 0, BATCH, SEQ, 1, DN_CHUNK, zero_dc, zero_ds, *dn_args)
        yc_s, dc_s, ds_s = deltanet(proj, N_PROMPT, DEC_BATCH, DEC_SEQ, DN_SPS, math.gcd(DEC_SEQ, DN_CHUNK),
                                    state_dn_conv[l], state_dn[l], *dn_args)

        y = jnp.concatenate([jnp.concatenate([ya_p, yb_p, yc_p, yd_p], axis=1),
                             jnp.concatenate([ya_s, yb_s, yc_s, yd_s], axis=1)], axis=0)
        k_new = proj[:, W_A:2 * W_A]
        v_new = proj[:, 2 * W_A:3 * W_A]
        ks_p.append(k_new[:N_PROMPT].reshape(BATCH, SEQ, H_A, 2, DH_A))
        vs_p.append(v_new[:N_PROMPT].reshape(BATCH, SEQ, H_A, DV_A))
        ks_s.append(k_new[N_PROMPT:].reshape(DEC_BATCH, DEC_SEQ, H_A, 2, DH_A))
        vs_s.append(v_new[N_PROMPT:].reshape(DEC_BATCH, DEC_SEQ, H_A, DV_A))
        for lst, vals in ((st_p, (cb_p, dc_p, ds_p, lb_p, lh_p[:, 0, :])), (st_s, (cb_s, dc_s, ds_s, lb_s, lh_s[:, 0, :]))):
            for i, val in enumerate(vals):
                lst[i].append(val)

        x = matmul_gated_residual(y, w_out[l].astype(bf16), x, g1, tm=1088, tn=1024, tk=D_MODEL)

        j = l // 2
        if l % 2 == 0:
            hn = norm_mod(x, norm_ffn[l][None, :], sc2, sh2)
            h = swiglu_up(hn, ffn_w_gate[j].astype(bf16), ffn_w_up[j].astype(bf16), tm=1088, tn=512)
            x = matmul_gated_residual(h, ffn_w_down[j].astype(bf16), x, g2, tm=1088, tn=512, tk=D_FF // 2)
        else:
            router = jnp.concatenate([moe_router[j], jnp.zeros((D_MODEL, LANES - N_EXPERTS), f32)], axis=1)
            hn, logits = norm_mod(x, norm_ffn[l][None, :], sc2, sh2, router=router)
            plan = moe_route(logits[:, :N_EXPERTS])
            hn_words = lax.bitcast_convert_type(hn.reshape(N_TOK, D_MODEL // 2, 2), jnp.uint32)
            xs_words = jnp.take(hn_words, plan['row_src'], axis=0)
            xs = lax.bitcast_convert_type(xs_words, bf16).reshape(MOE_ROWS, D_MODEL)
            h = moe_up(plan['item_m'], plan['item_n'], plan['item_e'], plan['item_v'], xs,
                       moe_w_gate[j].astype(bf16), moe_w_up[j].astype(bf16))
            ys = moe_down(plan['tile_e'], plan['n_used'], h, moe_w_down[j].astype(bf16))
            gates = plan['gates']
            dest = plan['dest']
            f = (gates[:, 0:1] * jnp.take(ys, dest[:, 0], axis=0)
                 + gates[:, 1:2] * jnp.take(ys, dest[:, 1], axis=0))
            g2_tok = jnp.repeat(g2, SUBLANES, axis=0)
            x = x + g2_tok * f

    y = final_norm(x, final_norm_w[None, :])
    y_prompt = y[:N_PROMPT].reshape(BATCH, SEQ, D_MODEL)
    y_sample = y[N_PROMPT:].reshape(DEC_BATCH, DEC_SEQ, D_MODEL)
    sp_ = [jnp.stack(o) for o in st_p]
    ss_ = [jnp.stack(o) for o in st_s]
    return (y_prompt, y_sample, jnp.stack(ks_p), jnp.stack(vs_p), jnp.stack(ks_s), jnp.stack(vs_s),
            sp_[0], ss_[0], sp_[1], ss_[1], sp_[2], ss_[2], sp_[3], ss_[3], sp_[4], ss_[4])
```

```python
import functools
import math

import jax
import jax.numpy as jnp
from jax import lax
from jax.experimental import pallas as pl
from jax.experimental.pallas import tpu as pltpu

D_MODEL = 2048
BATCH = 8
SEQ = 2048
DEPTH = 2
DEC_BATCH = 128
DEC_SEQ = 8
W_GROUP = D_MODEL // 4
W_A = W_B = W_C = W_D = W_GROUP
H_A = 4
DH_A = W_A // (2 * H_A)
DV_A = 2 * DH_A
Q_BLOCK = 128
CONV_B = 3
H_C = 4
DK_C = W_C // H_C
DV_C = W_C // H_C
CONV_C = 4
DN_CHUNK = 64
H_D = 4
BW_D = W_D // H_D
CONV_D = 4
RG_C = 8.0
D_FF = 5632
N_EXPERTS = 8
TOP_K = 2
D_FF_EXPERT = 7168
N_MOD = 6
EPS = 1e-6
N_A = 3 * W_A
N_B = 3 * W_B
N_C = 4 * W_C + 2 * H_C
N_D = 2 * W_D

N_PROMPT = BATCH * SEQ
N_SAMPLE = DEC_BATCH * DEC_SEQ
N_TOK = N_PROMPT + N_SAMPLE
SUBLANES = 8
N_GROUPS = N_TOK // SUBLANES
LANES = 128

OFF_A = 0
OFF_B = N_A
OFF_C = N_A + N_B
OFF_D = OFF_C + 4 * W_C
OFF_AB = OFF_D + N_D
N_IN_PAD = OFF_AB + LANES

VMEM_LIMIT = 48 * 1024 * 1024
MOE_TM = 512
MOE_TN = 512
MOE_TK = 1024
MOE_TILES = (TOP_K * N_TOK) // MOE_TM + N_EXPERTS
MOE_ROWS = MOE_TILES * MOE_TM
ATT_TQ = 512
ATT_HPB = 2
LRU_CHUNK = 256
LRU_SPS = 16
DN_SPS = 8
DN_SPS_PROMPT = 1

f32 = jnp.float32
bf16 = jnp.bfloat16


def _params(sem):
    return pltpu.CompilerParams(dimension_semantics=sem, vmem_limit_bytes=VMEM_LIMIT)


def _silu(x):
    return x * (1.0 / (1.0 + jnp.exp(-x)))


def _mod_kernel(c_ref, w_ref, b_ref, o_ref):
    c = _silu(c_ref[...]).astype(bf16)
    o_ref[...] = jnp.dot(c, w_ref[0].astype(bf16), preferred_element_type=f32) + b_ref[0]


def modulation(c_all, w, b, layer):
    r = c_all.shape[0]
    n = w.shape[2]
    tn = 1024
    return pl.pallas_call(
        _mod_kernel,
        grid=(n // tn,),
        in_specs=[pl.BlockSpec((r, D_MODEL), lambda j: (0, 0)),
                  pl.BlockSpec((1, D_MODEL, tn), lambda j: (layer, 0, j)),
                  pl.BlockSpec((1, 1, tn), lambda j: (layer, 0, j))],
        out_specs=pl.BlockSpec((r, tn), lambda j: (0, j)),
        out_shape=jax.ShapeDtypeStruct((r, n), f32),
        compiler_params=_params(("arbitrary",)),
        name="modulation",
    )(c_all, w, b)


def _norm_mod_body(x_ref, nw_ref, sc_ref, sh_ref):
    x = x_ref[...]
    tr = x.shape[0]
    y = x * lax.rsqrt(jnp.mean(x * x, axis=-1, keepdims=True) + EPS) * nw_ref[...]
    y = y.reshape(tr // SUBLANES, SUBLANES, D_MODEL)
    y = y * (1.0 + sc_ref[...][:, None, :]) + sh_ref[...][:, None, :]
    return y.reshape(tr, D_MODEL)


def _norm_mod_kernel(x_ref, nw_ref, sc_ref, sh_ref, o_ref):
    o_ref[...] = _norm_mod_body(x_ref, nw_ref, sc_ref, sh_ref).astype(o_ref.dtype)


def _norm_mod_router_kernel(x_ref, nw_ref, sc_ref, sh_ref, r_ref, o_ref, lg_ref):
    hn = _norm_mod_body(x_ref, nw_ref, sc_ref, sh_ref)
    bits = lax.bitcast_convert_type(hn.astype(bf16).astype(f32), jnp.uint32)
    half = D_MODEL // 2
    o_ref[...] = (bits[:, :half] >> 16) | bits[:, half:]
    lg_ref[...] = jnp.dot(hn, r_ref[...], preferred_element_type=f32, precision=lax.Precision.HIGHEST)


def norm_mod(x, nw, sc_g, sh_g, router=None):
    tr = 512
    tg = tr // SUBLANES
    in_specs = [pl.BlockSpec((tr, D_MODEL), lambda i: (i, 0)),
                pl.BlockSpec((1, D_MODEL), lambda i: (0, 0)),
                pl.BlockSpec((tg, D_MODEL), lambda i: (i, 0)),
                pl.BlockSpec((tg, D_MODEL), lambda i: (i, 0))]
    out_spec = pl.BlockSpec((tr, D_MODEL), lambda i: (i, 0))
    out_shape = jax.ShapeDtypeStruct((N_TOK, D_MODEL), bf16)
    if router is not None:
        out_spec = pl.BlockSpec((tr, D_MODEL // 2), lambda i: (i, 0))
        out_shape = jax.ShapeDtypeStruct((N_TOK, D_MODEL // 2), jnp.uint32)
    if router is None:
        return pl.pallas_call(
            _norm_mod_kernel, grid=(N_TOK // tr,), in_specs=in_specs, out_specs=out_spec,
            out_shape=out_shape, compiler_params=_params(("arbitrary",)), name="norm_mod",
        )(x, nw, sc_g, sh_g)
    return pl.pallas_call(
        _norm_mod_router_kernel, grid=(N_TOK // tr,),
        in_specs=in_specs + [pl.BlockSpec((D_MODEL, LANES), lambda i: (0, 0))],
        out_specs=[out_spec, pl.BlockSpec((tr, LANES), lambda i: (i, 0))],
        out_shape=[out_shape, jax.ShapeDtypeStruct((N_TOK, LANES), f32)],
        compiler_params=_params(("arbitrary",)), name="norm_mod_router",
    )(x, nw, sc_g, sh_g, router)


def _final_norm_kernel(x_ref, nw_ref, o_ref):
    x = x_ref[...]
    o_ref[...] = x * lax.rsqrt(jnp.mean(x * x, axis=-1, keepdims=True) + EPS) * nw_ref[...]


def final_norm(x, nw):
    tr = 512
    return pl.pallas_call(
        _final_norm_kernel, grid=(N_TOK // tr,),
        in_specs=[pl.BlockSpec((tr, D_MODEL), lambda i: (i, 0)),
                  pl.BlockSpec((1, D_MODEL), lambda i: (0, 0))],
        out_specs=pl.BlockSpec((tr, D_MODEL), lambda i: (i, 0)),
        out_shape=jax.ShapeDtypeStruct((N_TOK, D_MODEL), f32),
        compiler_params=_params(("arbitrary",)), name="final_norm",
    )(x, nw)


def _mm_kernel(x_ref, w_ref, o_ref):
    o_ref[...] = jnp.dot(x_ref[...], w_ref[...], preferred_element_type=f32).astype(o_ref.dtype)


def matmul(x, w, tm, tn, out_dtype):
    m, k = x.shape
    n = w.shape[1]
    return pl.pallas_call(
        _mm_kernel, grid=(m // tm, n // tn),
        in_specs=[pl.BlockSpec((tm, k), lambda i, j: (i, 0)),
                  pl.BlockSpec((k, tn), lambda i, j: (0, j))],
        out_specs=pl.BlockSpec((tm, tn), lambda i, j: (i, j)),
        out_shape=jax.ShapeDtypeStruct((m, n), out_dtype),
        compiler_params=_params(("arbitrary", "arbitrary")), name="matmul",
    )(x, w)


def _mm_resid_kernel(y_ref, w_ref, x_ref, g_ref, o_ref, acc_ref, *, nk):
    k = pl.program_id(2)
    part = jnp.dot(y_ref[...], w_ref[...], preferred_element_type=f32)

    @pl.when(k == 0)
    def _():
        acc_ref[...] = part

    @pl.when(k > 0)
    def _():
        acc_ref[...] += part

    @pl.when(k == nk - 1)
    def _():
        tm, tn = acc_ref.shape
        a = acc_ref[...].reshape(tm // SUBLANES, SUBLANES, tn) * g_ref[...][:, None, :]
        o_ref[...] = x_ref[...] + a.reshape(tm, tn)


def matmul_gated_residual(y, w, x, g_g, tm, tn, tk):
    m, kdim = y.shape
    n = w.shape[1]
    nk = kdim // tk
    return pl.pallas_call(
        functools.partial(_mm_resid_kernel, nk=nk), grid=(m // tm, n // tn, nk),
        in_specs=[pl.BlockSpec((tm, tk), lambda i, j, k: (i, k)),
                  pl.BlockSpec((tk, tn), lambda i, j, k: (k, j)),
                  pl.BlockSpec((tm, tn), lambda i, j, k: (i, j)),
                  pl.BlockSpec((tm // SUBLANES, tn), lambda i, j, k: (i, j))],
        out_specs=pl.BlockSpec((tm, tn), lambda i, j, k: (i, j)),
        out_shape=jax.ShapeDtypeStruct((m, n), f32),
        scratch_shapes=[pltpu.VMEM((tm, tn), f32)],
        compiler_params=_params(("arbitrary", "arbitrary", "arbitrary")), name="matmul_gated_residual",
    )(y, w, x, g_g)


def _mix_out_kernel(ya_ref, yb_ref, yc_ref, yd_ref, w_ref, x_ref, g_ref, o_ref):
    acc = jnp.dot(ya_ref[...], w_ref[0:W_GROUP, :], preferred_element_type=f32)
    for gi, y_ref in enumerate((yb_ref, yc_ref, yd_ref), start=1):
        acc = acc + jnp.dot(y_ref[...], w_ref[gi * W_GROUP:(gi + 1) * W_GROUP, :], preferred_element_type=f32)
    tm, tn = acc.shape
    a = acc.reshape(tm // SUBLANES, SUBLANES, tn) * g_ref[...][:, None, :]
    o_ref[...] = x_ref[...] + a.reshape(tm, tn)


def mix_out_residual(ys, w, x, g_g, tm, tn):
    m, n = x.shape
    y_spec = pl.BlockSpec((tm, W_GROUP), lambda i, j: (i, 0))
    return pl.pallas_call(
        _mix_out_kernel, grid=(m // tm, n // tn),
        in_specs=[y_spec, y_spec, y_spec, y_spec,
                  pl.BlockSpec((D_MODEL, tn), lambda i, j: (0, j)),
                  pl.BlockSpec((tm, tn), lambda i, j: (i, j)),
                  pl.BlockSpec((tm // SUBLANES, tn), lambda i, j: (i, j))],
        out_specs=pl.BlockSpec((tm, tn), lambda i, j: (i, j)),
        out_shape=jax.ShapeDtypeStruct((m, n), f32),
        compiler_params=_params(("arbitrary", "arbitrary")), name="mix_out_residual",
    )(*ys, w, x, g_g)


def _swiglu_body(x, wg, wu):
    g = jnp.dot(x, wg, preferred_element_type=f32)
    u = jnp.dot(x, wu, preferred_element_type=f32)
    return _silu(g) * u


def _swiglu_kernel(x_ref, wg_ref, wu_ref, o_ref):
    o_ref[...] = _swiglu_body(x_ref[...], wg_ref[...], wu_ref[...]).astype(o_ref.dtype)


def swiglu_up(x, wg, wu, tm, tn):
    m, k = x.shape
    n = wg.shape[1]
    return pl.pallas_call(
        _swiglu_kernel, grid=(m // tm, n // tn),
        in_specs=[pl.BlockSpec((tm, k), lambda i, j: (i, 0)),
                  pl.BlockSpec((k, tn), lambda i, j: (0, j)),
                  pl.BlockSpec((k, tn), lambda i, j: (0, j))],
        out_specs=pl.BlockSpec((tm, tn), lambda i, j: (i, j)),
        out_shape=jax.ShapeDtypeStruct((m, n), bf16),
        compiler_params=_params(("arbitrary", "arbitrary")), name="swiglu_up",
    )(x, wg, wu)


def _unpack_tokens(words):
    lo = lax.bitcast_convert_type(words << 16, f32)
    hi = lax.bitcast_convert_type(words & jnp.uint32(0xFFFF0000), f32)
    return jnp.concatenate([lo, hi], axis=1).astype(bf16)


def _moe_up_kernel(m_ref, n_ref, e_ref, v_ref, x_ref, wg_ref, wu_ref, o_ref, wg_sc, wu_sc):
    i = pl.program_id(0)

    @pl.when(v_ref[i] == 2)
    def _():
        wg_sc[...] = wg_ref[0].astype(bf16)
        wu_sc[...] = wu_ref[0].astype(bf16)

    @pl.when(v_ref[i] > 0)
    def _():
        o_ref[...] = _swiglu_body(_unpack_tokens(x_ref[...]), wg_sc[...], wu_sc[...]).astype(o_ref.dtype)


def moe_up(item_m, item_n, item_e, item_v, xs, wg, wu):
    ff = wg.shape[2]
    nt = ff // MOE_TN
    n_items = MOE_TILES * nt
    grid_spec = pltpu.PrefetchScalarGridSpec(
        num_scalar_prefetch=4, grid=(n_items,),
        in_specs=[pl.BlockSpec((MOE_TM, D_MODEL // 2), lambda i, m, n, e, v: (m[i], 0)),
                  pl.BlockSpec((1, D_MODEL, MOE_TN), lambda i, m, n, e, v: (e[i], 0, n[i])),
                  pl.BlockSpec((1, D_MODEL, MOE_TN), lambda i, m, n, e, v: (e[i], 0, n[i]))],
        out_specs=pl.BlockSpec((MOE_TM, MOE_TN), lambda i, m, n, e, v: (m[i], n[i])),
        scratch_shapes=[pltpu.VMEM((D_MODEL, MOE_TN), bf16), pltpu.VMEM((D_MODEL, MOE_TN), bf16)])
    return pl.pallas_call(
        _moe_up_kernel, grid_spec=grid_spec,
        out_shape=jax.ShapeDtypeStruct((MOE_ROWS, ff), bf16),
        compiler_params=_params(("arbitrary",)), name="moe_up",
    )(item_m, item_n, item_e, item_v, xs, wg, wu)


def _moe_down_kernel(te_ref, nu_ref, h_ref, wd_ref, o_ref):
    m = pl.program_id(0)
    k = pl.program_id(1)

    @pl.when(m < nu_ref[0])
    def _():
        part = jnp.dot(h_ref[...], wd_ref[0], preferred_element_type=f32)

        @pl.when(k == 0)
        def _():
            o_ref[...] = part

        @pl.when(k > 0)
        def _():
            o_ref[...] += part


def moe_down(tile_e, n_used, h, wd):
    nk = wd.shape[1] // MOE_TK

    def mclamp(m, nu):
        return jnp.minimum(m, nu[0] - 1)

    def kclamp(m, k, nu):
        return jnp.where(m < nu[0], k, nk - 1)

    grid_spec = pltpu.PrefetchScalarGridSpec(
        num_scalar_prefetch=2, grid=(MOE_TILES, nk),
        in_specs=[pl.BlockSpec((MOE_TM, MOE_TK), lambda m, k, te, nu: (mclamp(m, nu), kclamp(m, k, nu))),
                  pl.BlockSpec((1, MOE_TK, D_MODEL),
                               lambda m, k, te, nu: (te[mclamp(m, nu)], kclamp(m, k, nu), 0))],
        out_specs=pl.BlockSpec((MOE_TM, D_MODEL), lambda m, k, te, nu: (mclamp(m, nu), 0)))
    return pl.pallas_call(
        _moe_down_kernel, grid_spec=grid_spec,
        out_shape=jax.ShapeDtypeStruct((MOE_ROWS, D_MODEL), f32),
        compiler_params=_params(("arbitrary", "arbitrary")), name="moe_down",
    )(tile_e, n_used, h, wd)


def moe_route(logits, n_ff_tiles):
    top_v, top_i = lax.top_k(logits, TOP_K)
    gates = jax.nn.softmax(top_v, axis=-1)
    e_flat = top_i.reshape(-1).astype(jnp.int32)
    onehot = (e_flat[:, None] == jnp.arange(N_EXPERTS, dtype=jnp.int32)[None, :]).astype(jnp.int32)
    csum = jnp.cumsum(onehot, axis=0)
    counts = csum[-1]
    rank = jnp.sum((csum - onehot) * onehot, axis=-1)
    tiles_e = (counts + MOE_TM - 1) // MOE_TM
    tile_end = jnp.cumsum(tiles_e)
    tile_start = tile_end - tiles_e
    dest = tile_start[e_flat] * MOE_TM + rank
    tok = jnp.arange(TOP_K * N_TOK, dtype=jnp.int32) // TOP_K
    row_src = jnp.zeros((MOE_ROWS,), jnp.int32).at[dest].set(tok)
    n_used = tile_end[-1]
    t_ids = jnp.arange(MOE_TILES, dtype=jnp.int32)
    tile_e = jnp.minimum(jnp.searchsorted(tile_end, t_ids, side="right"), N_EXPERTS - 1).astype(jnp.int32)
    nt = n_ff_tiles
    items_e = tiles_e * nt
    item_end = jnp.cumsum(items_e)
    item_start = item_end - items_e
    total = item_end[-1]
    idx = jnp.arange(MOE_TILES * nt, dtype=jnp.int32)
    valid = idx < total
    idc = jnp.minimum(idx, total - 1)
    ie = jnp.minimum(jnp.searchsorted(item_end, idc, side="right"), N_EXPERTS - 1).astype(jnp.int32)
    local = idc - item_start[ie]
    te = jnp.maximum(tiles_e[ie], 1)
    item_n = (local // te).astype(jnp.int32)
    item_m = (tile_start[ie] + local % te).astype(jnp.int32)
    return dict(gates=gates, dest=dest.reshape(N_TOK, TOP_K), row_src=row_src, tile_e=tile_e,
                n_used=n_used.reshape(1).astype(jnp.int32), item_m=item_m, item_n=item_n, item_e=ie,
                item_v=jnp.where(valid, jnp.where(local % te == 0, 2, 1), 0).astype(jnp.int32))


def _smem_spec():
    return pl.BlockSpec(memory_space=pltpu.SMEM)


def _subln(o, w, gain):
    return o * lax.rsqrt(jnp.mean(o * o, axis=-1, keepdims=True) + EPS) * w * gain


def _attn_prompt_kernel(lam_ref, slope_ref, q_ref, k_ref, v_ref, w_ref, o_ref, m_sc, acc_sc, *, tq, hpb, gain):
    hb = pl.program_id(1)
    qi = pl.program_id(2)
    lane = lax.broadcasted_iota(jnp.int32, (tq, DV_A), 1)
    row = lax.broadcasted_iota(jnp.int32, (2 * tq, tq), 0)
    col = lax.broadcasted_iota(jnp.int32, (2 * tq, tq), 1)
    rel = (col - jnp.where(row >= tq, row - tq, row)).astype(f32)
    ones_v = jnp.ones((tq, DV_A), bf16)
    qss, slopes, biases = [], [], []
    for hh in range(hpb):
        q = q_ref[:, hh * DV_A:(hh + 1) * DV_A] * (DH_A ** -0.5)
        qss.append(jnp.concatenate([jnp.where(lane < DH_A, q, 0.0), jnp.where(lane >= DH_A, q, 0.0)],
                                   axis=0).astype(bf16))
        slopes.append(slope_ref[hb * hpb + hh])
        biases.append(slopes[hh] * rel)
    m_sc[...] = jnp.full(m_sc.shape, -jnp.inf, f32)
    acc_sc[...] = jnp.zeros(acc_sc.shape, f32)

    def chunk_step(j, diagonal):
        off = pl.multiple_of(j * tq, tq)
        blocks = (jnp.zeros((1, 1), jnp.int32) + (qi - j) * tq).astype(f32)
        for hh in range(hpb):
            cs = slice(hh * DV_A, (hh + 1) * DV_A)
            kc = k_ref[pl.ds(off, tq), cs].astype(bf16)
            vc = jnp.concatenate([v_ref[pl.ds(off, tq), cs].astype(bf16), ones_v], axis=1)
            s = lax.dot_general(qss[hh], kc, (((1,), (1,)), ((), ())), preferred_element_type=f32)
            s = s + biases[hh] - slopes[hh] * blocks
            if diagonal:
                s = jnp.where(rel <= 0, s, -1e30)
            m_prev = m_sc[hh]
            m_new = jnp.maximum(m_prev, jnp.max(s, axis=-1, keepdims=True))
            pr = jnp.exp(s - m_new).astype(bf16)
            acc_sc[hh] = jnp.exp(m_prev - m_new) * acc_sc[hh] + jnp.dot(pr, vc, preferred_element_type=f32)
            m_sc[hh] = m_new

    def body(j, carry):
        chunk_step(j, False)
        return carry

    lax.fori_loop(0, qi, body, 0)
    chunk_step(qi, True)
    for hh in range(hpb):
        acc = acc_sc[hh]
        o = acc[:, :DV_A] / acc[:, DV_A:]
        o = o[:tq] - lam_ref[0] * o[tq:]
        o_ref[:, hh * DV_A:(hh + 1) * DV_A] = _subln(o, w_ref[...], gain).astype(o_ref.dtype)


def attn_prompt(proj, lam, slopes, subln_w, gain, batch, seq, tq, hpb):
    nq = seq // tq
    wb = hpb * DV_A
    nhb = H_A // hpb
    return pl.pallas_call(
        functools.partial(_attn_prompt_kernel, tq=tq, hpb=hpb, gain=gain),
        grid=(batch, nhb, nq),
        in_specs=[_smem_spec(), _smem_spec(),
                  pl.BlockSpec((tq, wb), lambda b, h, qi: (b * nq + qi, h)),
                  pl.BlockSpec((seq, wb), lambda b, h, qi: (b, nhb + h)),
                  pl.BlockSpec((seq, wb), lambda b, h, qi: (b, 2 * nhb + h)),
                  pl.BlockSpec((1, DV_A), lambda b, h, qi: (0, 0))],
        out_specs=pl.BlockSpec((tq, wb), lambda b, h, qi: (b * nq + qi, h)),
        out_shape=jax.ShapeDtypeStruct((N_TOK, W_A), bf16),
        scratch_shapes=[pltpu.VMEM((hpb, 2 * tq, 1), f32), pltpu.VMEM((hpb, 2 * tq, 2 * DV_A), f32)],
        compiler_params=_params(("arbitrary", "arbitrary", "arbitrary")), name="attn_prompt",
    )(lam, slopes, proj, proj, proj, subln_w)


def _attn_sample_kernel(pt_ref, lam_ref, slope_ref, q_ref, kn_ref, vn_ref, w_ref, *rest, t, n_pages, page, gain):
    k_refs = rest[:n_pages]
    v_refs = rest[n_pages:2 * n_pages]
    o_ref = rest[2 * n_pages + 1]
    s_sc = rest[2 * n_pages + 2]
    n_rep = 2 * H_A
    rows = n_rep * t
    n_past = n_pages * page
    q = q_ref[...] * (DH_A ** -0.5)
    q_rep = jnp.concatenate([q] * n_rep, axis=0)
    r_grp = lax.broadcasted_iota(jnp.int32, (rows, W_A), 0) // t
    c_grp = lax.broadcasted_iota(jnp.int32, (rows, W_A), 1) // DH_A
    qbd = jnp.where(r_grp == c_grp, q_rep, 0.0).astype(bf16)
    for j in range(n_pages):
        s_sc[:, j * page:(j + 1) * page] = jnp.dot(qbd, k_refs[j][0, 0].astype(bf16), preferred_element_type=f32)
    s_new = lax.dot_general(qbd, kn_ref[...].astype(bf16), (((1,), (1,)), ((), ())), preferred_element_type=f32)

    row1 = lax.broadcasted_iota(jnp.int32, (rows, 1), 0)
    head1 = row1 // (2 * t)
    slope = jnp.where(head1 == 0, slope_ref[0],
                      jnp.where(head1 == 1, slope_ref[1], jnp.where(head1 == 2, slope_ref[2], slope_ref[3])))
    t_row = (row1 % t)
    kpos = lax.broadcasted_iota(jnp.int32, (rows, n_past), 1)
    s_past = s_sc[...] - slope * (n_past + t_row - kpos).astype(f32)
    dist_new = (t_row - lax.broadcasted_iota(jnp.int32, (rows, t), 1)).astype(f32)
    s_new = jnp.where(dist_new >= 0, s_new - slope * dist_new, -1e30)
    m = jnp.maximum(jnp.max(s_past, axis=-1, keepdims=True), jnp.max(s_new, axis=-1, keepdims=True))
    p_past = jnp.exp(s_past - m)
    p_new = jnp.exp(s_new - m)
    denom = jnp.sum(p_past, axis=-1, keepdims=True) + jnp.sum(p_new, axis=-1, keepdims=True)
    p_past = p_past.astype(bf16)
    p_new = p_new.astype(bf16)
    lam = lam_ref[0]
    w = w_ref[...]
    for h in range(H_A):
        vh = jnp.concatenate([v_refs[j][0, 0, pl.ds(h, page, stride=H_A), :] for j in range(n_pages)], axis=0)
        rs = slice(h * 2 * t, (h + 1) * 2 * t)
        oh = jnp.dot(p_past[rs], vh.astype(bf16), preferred_element_type=f32)
        oh = oh + jnp.dot(p_new[rs], vn_ref[:, h * DV_A:(h + 1) * DV_A].astype(bf16), preferred_element_type=f32)
        oh = oh / denom[rs]
        o = oh[:t] - lam * oh[t:]
        o_ref[:, h * DV_A:(h + 1) * DV_A] = _subln(o, w, gain).astype(o_ref.dtype)


def attn_sample(proj, ckt, cvr, layer, page_table, lam, slopes, subln_w, gain, row0, nseq, t, y_rows):
    n_pages = page_table.shape[1]
    page = ckt.shape[-1]
    rb0 = row0 // t

    def kv_spec(j):
        return pl.BlockSpec((1, 1, W_A, page), lambda b, pt, j=j: (layer, pt[b * n_pages + j], 0, 0))

    in_specs = ([_smem_spec(), _smem_spec(),
                 pl.BlockSpec((t, W_A), lambda b, pt: (rb0 + b, 0)),
                 pl.BlockSpec((t, W_A), lambda b, pt: (rb0 + b, 1)),
                 pl.BlockSpec((t, W_A), lambda b, pt: (rb0 + b, 2)),
                 pl.BlockSpec((1, DV_A), lambda b, pt: (0, 0))]
                + [kv_spec(j) for j in range(n_pages)] + [kv_spec(j) for j in range(n_pages)]
                + [pl.BlockSpec(memory_space=pl.ANY)])
    grid_spec = pltpu.PrefetchScalarGridSpec(
        num_scalar_prefetch=1, grid=(nseq,), in_specs=in_specs,
        out_specs=pl.BlockSpec((t, W_A), lambda b, pt: (rb0 + b, 0)),
        scratch_shapes=[pltpu.VMEM((2 * H_A * t, n_pages * page), f32)])
    args = (page_table.reshape(-1), lam, slopes, proj, proj, proj, subln_w, *([ckt] * n_pages), *([cvr] * n_pages),
            y_rows)
    return pl.pallas_call(
        functools.partial(_attn_sample_kernel, t=t, n_pages=n_pages, page=page, gain=gain),
        grid_spec=grid_spec, out_shape=jax.ShapeDtypeStruct(y_rows.shape, y_rows.dtype),
        input_output_aliases={len(args) - 1: 0},
        compiler_params=_params(("arbitrary",)), name="attn_sample",
    )(*args)


def _conv_step(f_sc, x3, w_ref, buf0_ref, buf_out, width, chunk, first):
    lo = SUBLANES - (width - 1)

    @pl.when(first)
    def _():
        f_sc[:, 0:SUBLANES, :] = jnp.zeros((f_sc.shape[0], SUBLANES, f_sc.shape[2]), f32)
        f_sc[:, lo:SUBLANES, :] = buf0_ref[...]

    f_sc[:, SUBLANES:SUBLANES + chunk, :] = x3
    out = w_ref[0:1, :][None] * f_sc[:, lo:lo + chunk, :]
    for j in range(1, width):
        out = out + w_ref[j:j + 1, :][None] * f_sc[:, lo + j:lo + j + chunk, :]
    buf_out[...] = f_sc[:, chunk + lo:chunk + SUBLANES, :]
    f_sc[:, 0:SUBLANES, :] = f_sc[:, chunk:chunk + SUBLANES, :]
    return out


def _sigmoid(x):
    return 1.0 / (1.0 + jnp.exp(-x))


def _gelu_tanh(x):
    return 0.5 * x * (1.0 + jnp.tanh(0.7978845608028654 * (x + 0.044715 * (x * x * x))))


def _conv_lru_kernel(pb_ref, pd_ref, cb0_ref, lb0_ref, h0_ref, wb_ref, wd_ref, bd_ref, wa_ref, ba_ref, wx_ref,
                     bx_ref, sp_ref, *rest, sps, chunk):
    yb_ref, yd_ref, cb_out, lb_out, h_out, fb_sc, fd_sc, h_sc = rest[-8:]
    first = pl.program_id(1) == 0
    rows = sps * chunk
    pb = pb_ref[...]
    bg, cg, xt = pb[:, :W_B], pb[:, W_B:2 * W_B], pb[:, 2 * W_B:]
    conv_b = _conv_step(fb_sc, (cg * xt).reshape(sps, chunk, W_B), wb_ref, cb0_ref, cb_out, CONV_B, chunk, first)
    yb_ref[...] = (bg * conv_b.reshape(rows, W_B)).astype(yb_ref.dtype)

    pd = pd_ref[...]
    xd, gd = pd[:, :W_D], pd[:, W_D:]
    conv_d = _conv_step(fd_sc, xd.reshape(sps, chunk, W_D), wd_ref, lb0_ref, lb_out, CONV_D, chunk, first)
    xf = conv_d.reshape(rows, W_D) + bd_ref[...]
    xfb = xf.astype(bf16)
    ra = jnp.concatenate([jnp.dot(xfb[:, h * BW_D:(h + 1) * BW_D], wa_ref[h].astype(bf16),
                                  preferred_element_type=f32) for h in range(H_D)], axis=-1)
    rx = jnp.concatenate([jnp.dot(xfb[:, h * BW_D:(h + 1) * BW_D], wx_ref[h].astype(bf16),
                                  preferred_element_type=f32) for h in range(H_D)], axis=-1)
    r = _sigmoid(ra + ba_ref[...])
    ig = _sigmoid(rx + bx_ref[...])
    log_a = -RG_C * r * sp_ref[...]
    a = jnp.exp(log_a)
    th = jnp.tanh(log_a)
    bx = jnp.sqrt(-2.0 * th / (1.0 - th)) * (ig * xf)

    pos = lax.broadcasted_iota(jnp.int32, (rows, W_D), 0) % chunk
    step = 1
    while step < chunk:
        a_s = pltpu.roll(a, step, 0)
        b_s = pltpu.roll(bx, step, 0)
        live = pos >= step
        bx = jnp.where(live, a * b_s + bx, bx)
        a = jnp.where(live, a * a_s, a)
        step *= 2

    @pl.when(first)
    def _():
        h_sc[...] = h0_ref[...]

    h_in = jnp.broadcast_to(h_sc[...], (sps, chunk, W_D)).reshape(rows, W_D)
    hs = bx + a * h_in
    h_last = hs.reshape(sps, chunk, W_D)[:, chunk - 1:chunk, :]
    h_sc[...] = h_last
    h_out[...] = h_last
    yd_ref[...] = (hs * _gelu_tanh(gd)).astype(yd_ref.dtype)


def conv_lru(proj, row0, nseq, t, sps, chunk, conv_b_buf, lru_buf, lru_h, wb, wd, bd, wa, ba, wx, bx, sp, y_prev=()):
    rows = sps * chunk
    nc = t // chunk
    rb0 = row0 // rows

    def rmap(s, c):
        return rb0 + s * nc + c

    def full(shape):
        return pl.BlockSpec(shape, lambda s, c: (0,) * len(shape))

    def per_seq(shape):
        return pl.BlockSpec((sps,) + shape, lambda s, c: (s,) + (0,) * len(shape))

    in_specs = [pl.BlockSpec((rows, N_B), lambda s, c: (rmap(s, c), OFF_B // N_B)),
                pl.BlockSpec((rows, N_D), lambda s, c: (rmap(s, c), OFF_D // N_D)),
                per_seq((CONV_B - 1, W_B)), per_seq((CONV_D - 1, W_D)), per_seq((1, W_D)),
                full((CONV_B, W_B)), full((CONV_D, W_D)), full((1, W_D)),
                full((H_D, BW_D, BW_D)), full((1, W_D)), full((H_D, BW_D, BW_D)), full((1, W_D)), full((1, W_D))]
    in_specs = in_specs + [pl.BlockSpec(memory_space=pl.ANY)] * len(y_prev)
    out_specs = [pl.BlockSpec((rows, W_B), lambda s, c: (rmap(s, c), 0)),
                 pl.BlockSpec((rows, W_D), lambda s, c: (rmap(s, c), 0)),
                 per_seq((CONV_B - 1, W_B)), per_seq((CONV_D - 1, W_D)), per_seq((1, W_D))]
    out_shape = [jax.ShapeDtypeStruct((N_TOK, W_B), bf16), jax.ShapeDtypeStruct((N_TOK, W_D), bf16),
                 jax.ShapeDtypeStruct((nseq, CONV_B - 1, W_B), f32),
                 jax.ShapeDtypeStruct((nseq, CONV_D - 1, W_D), f32),
                 jax.ShapeDtypeStruct((nseq, 1, W_D), f32)]
    return pl.pallas_call(
        functools.partial(_conv_lru_kernel, sps=sps, chunk=chunk),
        grid=(nseq // sps, nc), in_specs=in_specs, out_specs=out_specs, out_shape=out_shape,
        scratch_shapes=[pltpu.VMEM((sps, chunk + SUBLANES, W_B), f32), pltpu.VMEM((sps, chunk + SUBLANES, W_D), f32),
                        pltpu.VMEM((sps, 1, W_D), f32)],
        input_output_aliases={13 + i: i for i in range(len(y_prev))},
        compiler_params=_params(("arbitrary", "arbitrary")), name="conv_lru",
    )(proj, proj, conv_b_buf, lru_buf, lru_h, wb, wd, bd, wa, ba, wx, bx, sp, *y_prev)


_HI = lax.Precision.HIGHEST


def _dot_nt(a, b, precision=None):
    return lax.dot_general(a, b, (((1,), (1,)), ((), ())), preferred_element_type=f32, precision=precision)


def _dot_tn(a, b):
    return lax.dot_general(a, b, (((0,), (0,)), ((), ())), preferred_element_type=f32)


def _hi_lo(x):
    hi = x.astype(bf16).astype(f32)
    return hi, x - hi


def _dot_f32(a, b, chunk):
    if chunk % 16:
        return jnp.dot(a, b, preferred_element_type=f32, precision=_HI)
    a_hi, a_lo = _hi_lo(a)
    b_hi, b_lo = _hi_lo(b)
    lhs = jnp.concatenate([a_hi, a_hi, a_lo], axis=1).astype(bf16)
    rhs = jnp.concatenate([b_hi, b_lo, b_hi], axis=0).astype(bf16)
    return jnp.dot(lhs, rhs, preferred_element_type=f32)


def _cumsum_rows(tri, x, chunk):
    if chunk % 16:
        return jnp.dot(tri, x, preferred_element_type=f32, precision=_HI)
    hi, r1 = _hi_lo(x)
    mid, lo = _hi_lo(r1)
    lhs = jnp.concatenate([tri, tri, tri], axis=1).astype(bf16)
    rhs = jnp.concatenate([hi, mid, lo], axis=0).astype(bf16)
    return jnp.dot(lhs, rhs, preferred_element_type=f32)


def _row_bcast(gc, pick, chunk):
    sel = pick.astype(f32)
    if chunk % 16:
        return _dot_nt(sel, gc, precision=_HI)
    hi, r1 = _hi_lo(gc)
    mid, lo = _hi_lo(r1)
    lhs = jnp.concatenate([sel, sel, sel], axis=1).astype(bf16)
    rhs = jnp.concatenate([hi, mid, lo], axis=1).astype(bf16)
    return _dot_nt(lhs, rhs)


def _deltanet_kernel(qkv_ref, z_ref, ab_ref, buf0_ref, s0_ref, wc_ref, nega_ref, dtb_ref, nw_ref,
                     *rest, sps, chunk):
    y_ref, buf_out, s_out, f_sc, s_sc = rest[-5:]
    first = pl.program_id(1) == 0
    rows = sps * chunk
    wq = 3 * W_C
    conv = _conv_step(f_sc, qkv_ref[...].reshape(sps, chunk, wq), wc_ref, buf0_ref, buf_out, CONV_C, chunk, first)
    conv = conv.reshape(rows, wq)
    qkv = conv * _sigmoid(conv)

    @pl.when(first)
    def _():
        s_sc[...] = s0_ref[...]

    ab = ab_ref[...]
    xg = ab + dtb_ref[...]
    g_all = nega_ref[...] * (jnp.maximum(xg, 0.0) + jnp.log1p(jnp.exp(-jnp.abs(xg))))
    beta_all = _sigmoid(ab)
    z = z_ref[...]
    ri = lax.broadcasted_iota(jnp.int32, (chunk, chunk), 0)
    ci = lax.broadcasted_iota(jnp.int32, (chunk, chunk), 1)
    causal = ri >= ci
    strict = ri > ci
    tri = causal.astype(f32)
    eye = (ri == ci).astype(f32)
    lane = lax.broadcasted_iota(jnp.int32, (chunk, LANES), 1)
    nw = nw_ref[...]
    units = [(i, h) for i in range(sps) for h in range(H_C)]
    gcs = [_cumsum_rows(tri, g_all[i * chunk:(i + 1) * chunk], chunk) for i in range(sps)]

    qs, ks, gcols, egs, decays, rhss, qks, negs = [], [], [], [], [], [], [], []
    for i, h in units:
        rs = slice(i * chunk, (i + 1) * chunk)
        q = qkv[rs, h * DK_C:(h + 1) * DK_C]
        k = qkv[rs, W_C + h * DK_C:W_C + (h + 1) * DK_C]
        v = qkv[rs, 2 * W_C + h * DV_C:2 * W_C + (h + 1) * DV_C]
        q = q * lax.rsqrt(jnp.sum(q * q, axis=-1, keepdims=True) + EPS) * (DK_C ** -0.5)
        k = k * lax.rsqrt(jnp.sum(k * k, axis=-1, keepdims=True) + EPS)
        beta = beta_all[rs, H_C + h:H_C + h + 1]
        gcol = gcs[i][:, h:h + 1]
        grow = _row_bcast(gcs[i], lane == h, chunk)
        decay = jnp.where(causal, jnp.exp(jnp.where(causal, gcol - grow, 0.0)), 0.0)
        eg = jnp.exp(gcol)
        kb = k * beta
        negs.append(jnp.where(strict, _dot_nt(kb, k) * decay, 0.0) * -1.0)
        qks.append(_dot_nt(q, k) * decay)
        rhss.append(jnp.concatenate([v * beta, kb * eg], axis=1))
        qs.append(q * eg)
        ks.append(k)
        gcols.append(gcol)
        egs.append(eg)
        decays.append(decay)

    invs = [eye + n for n in negs]
    span = 2
    while span < chunk:
        negs = [_dot_f32(n, n, chunk) for n in negs]
        invs = [inv + _dot_f32(inv, n, chunk) for inv, n in zip(invs, negs)]
        span *= 2
    sols = [_dot_f32(inv, rhs, chunk) for inv, rhs in zip(invs, rhss)]

    states = [s_sc[i, h] for i, h in units]
    ws_qs = [jnp.dot(jnp.concatenate([sol[:, DV_C:], q], axis=0), s, preferred_element_type=f32)
             for sol, q, s in zip(sols, qs, states)]
    v_news = [sol[:, :DV_C] - wq_s[:chunk] for sol, wq_s in zip(sols, ws_qs)]
    outs = [wq_s[chunk:] + jnp.dot(qk, v_new, preferred_element_type=f32)
            for wq_s, qk, v_new in zip(ws_qs, qks, v_news)]
    for idx, (i, h) in enumerate(units):
        gcol = gcols[idx]
        g_last = gcol[chunk - 1:chunk, :]
        s_new = states[idx] * jnp.exp(g_last) + _dot_tn(ks[idx] * jnp.exp(g_last - gcol), v_news[idx])
        s_sc[i, h] = s_new
        s_out[i, h] = s_new
    for idx, (i, h) in enumerate(units):
        rs = slice(i * chunk, (i + 1) * chunk)
        zh = z[rs, h * DV_C:(h + 1) * DV_C]
        o = outs[idx]
        o = o * lax.rsqrt(jnp.mean(o * o, axis=-1, keepdims=True) + EPS) * nw
        y_ref[rs, h * DV_C:(h + 1) * DV_C] = (o * (zh * _sigmoid(zh))).astype(y_ref.dtype)


def deltanet(proj, row0, nseq, t, sps, chunk, dn_buf, dn_s, wc, nega, dtb, nw, y_prev=()):
    rows = sps * chunk
    nc = t // chunk
    rb0 = row0 // rows
    wq = 3 * W_C

    def rmap(s, c):
        return rb0 + s * nc + c

    def full(shape):
        return pl.BlockSpec(shape, lambda s, c: (0,) * len(shape))

    def per_seq(shape):
        return pl.BlockSpec((sps,) + shape, lambda s, c: (s,) + (0,) * len(shape))

    in_specs = [pl.BlockSpec((rows, wq), lambda s, c: (rmap(s, c), OFF_C // wq)),
                pl.BlockSpec((rows, W_C), lambda s, c: (rmap(s, c), (OFF_C + wq) // W_C)),
                pl.BlockSpec((rows, LANES), lambda s, c: (rmap(s, c), OFF_AB // LANES)),
                per_seq((CONV_C - 1, wq)), per_seq((H_C, DK_C, DV_C)),
                full((CONV_C, wq)), full((1, LANES)), full((1, LANES)), full((1, DV_C))]
    in_specs = in_specs + [pl.BlockSpec(memory_space=pl.ANY)] * len(y_prev)
    out_specs = [pl.BlockSpec((rows, W_C), lambda s, c: (rmap(s, c), 0)),
                 per_seq((CONV_C - 1, wq)), per_seq((H_C, DK_C, DV_C))]
    out_shape = [jax.ShapeDtypeStruct((N_TOK, W_C), bf16),
                 jax.ShapeDtypeStruct((nseq, CONV_C - 1, wq), f32),
                 jax.ShapeDtypeStruct((nseq, H_C, DK_C, DV_C), f32)]
    return pl.pallas_call(
        functools.partial(_deltanet_kernel, sps=sps, chunk=chunk),
        grid=(nseq // sps, nc), in_specs=in_specs, out_specs=out_specs, out_shape=out_shape,
        scratch_shapes=[pltpu.VMEM((sps, chunk + SUBLANES, wq), f32), pltpu.VMEM((sps, H_C, DK_C, DV_C), f32)],
        input_output_aliases={9 + i: i for i in range(len(y_prev))},
        compiler_params=_params(("arbitrary", "arbitrary")), name="deltanet",
    )(proj, proj, proj, dn_buf, dn_s, wc, nega, dtb, nw, *y_prev)


def _group_rows(m):
    return jnp.concatenate([jnp.repeat(m[:BATCH], SEQ // SUBLANES, axis=0), m[BATCH:]], axis=0)


def _lane_row(v):
    return jnp.concatenate([v.astype(f32), jnp.zeros((LANES - v.shape[0],), f32)])[None, :]


def kernel(x_prompt, x_sample, cache_k, cache_v, state_conv_b, state_dn_conv, state_dn, state_lru_conv, state_lru_h, page_table, c_prompt, c_sample, w_mod, b_mod, norm_mix, norm_ffn, w_in, w_out, lam_q1, lam_k1, lam_q2, lam_k2, subln_w, conv_b_w, dn_conv_w, dn_a_log, dn_dt_bias, dn_norm_w, lru_conv_w, lru_conv_b, lru_wa, lru_ba, lru_wx, lru_bx, lru_lambda, ffn_w_gate, ffn_w_up, ffn_w_down, moe_router, moe_w_gate, moe_w_up, moe_w_down, final_norm_w):
    x = jnp.concatenate([x_prompt.reshape(N_PROMPT, D_MODEL), x_sample.reshape(N_SAMPLE, D_MODEL)], axis=0)
    c_all = jnp.concatenate([c_prompt, c_sample], axis=0)
    n_pool, page = cache_k.shape[1], cache_k.shape[2]
    ckt = jnp.transpose(cache_k, (0, 1, 3, 4, 5, 2)).reshape(DEPTH, n_pool, W_A, page)
    cvr = cache_v.reshape(DEPTH, n_pool, page * H_A, DV_A)
    slopes = jnp.exp2(-8.0 * jnp.arange(1, H_A + 1, dtype=f32) / H_A)

    zero_cb = jnp.zeros((BATCH, CONV_B - 1, W_B), f32)
    zero_dc = jnp.zeros((BATCH, CONV_C - 1, 3 * W_C), f32)
    zero_ds = jnp.zeros((BATCH, H_C, DK_C, DV_C), f32)
    zero_lc = jnp.zeros((BATCH, CONV_D - 1, W_D), f32)
    zero_lh = jnp.zeros((BATCH, 1, W_D), f32)

    ks_p, vs_p, ks_s, vs_s = [], [], [], []
    st_p = [[] for _ in range(5)]
    st_s = [[] for _ in range(5)]
    for l in range(DEPTH):
        mod = modulation(c_all, w_mod, b_mod[:, None, :], l)
        sh1, sc1, g1, sh2, sc2, g2 = (_group_rows(m) for m in jnp.split(mod, N_MOD, axis=-1))

        hn = norm_mod(x, norm_mix[l][None, :], sc1, sh1)
        wl = w_in[l]
        n_cqkvz = 4 * W_C
        w_pack = jnp.concatenate(
            [wl[:, :N_A + N_B + n_cqkvz], wl[:, N_A + N_B + N_C:], wl[:, N_A + N_B + n_cqkvz:N_A + N_B + N_C],
             jnp.zeros((D_MODEL, LANES - 2 * H_C), f32)], axis=1).astype(bf16)
        proj = matmul(hn, w_pack, tm=1088, tn=896, out_dtype=f32)

        lam_init = 0.8 - 0.6 * math.exp(-0.3 * l)
        lam = (jnp.exp(jnp.sum(lam_q1[l] * lam_k1[l])) - jnp.exp(jnp.sum(lam_q2[l] * lam_k2[l])) + lam_init)
        lam = lam.astype(f32).reshape(1)
        gain = 1.0 - lam_init
        sw = subln_w[l][None, :]
        ya = attn_prompt(proj, lam, slopes, sw, gain, BATCH, SEQ, ATT_TQ, ATT_HPB)
        ya = attn_sample(proj, ckt, cvr, l, page_table, lam, slopes, sw, gain, N_PROMPT, DEC_BATCH, DEC_SEQ, ya)

        sp = jax.nn.softplus(-lru_lambda[l])[None, :]
        lru_args = (conv_b_w[l], lru_conv_w[l], lru_conv_b[l][None, :], lru_wa[l], lru_ba[l][None, :],
                    lru_wx[l], lru_bx[l][None, :], sp)
        yb, yd, cb_p, lb_p, lh_p = conv_lru(proj, 0, BATCH, SEQ, 1, LRU_CHUNK, zero_cb, zero_lc, zero_lh, *lru_args)
        yb, yd, cb_s, lb_s, lh_s = conv_lru(proj, N_PROMPT, DEC_BATCH, DEC_SEQ, LRU_SPS, DEC_SEQ,
                                            state_conv_b[l], state_lru_conv[l], state_lru_h[l][:, None, :],
                                            *lru_args, y_prev=(yb, yd))

        dn_args = (dn_conv_w[l], _lane_row(-jnp.exp(dn_a_log[l])), _lane_row(dn_dt_bias[l]), dn_norm_w[l][None, :])
        yc, dc_p, ds_p = deltanet(proj, 0, BATCH, SEQ, DN_SPS_PROMPT, DN_CHUNK, zero_dc, zero_ds, *dn_args)
        yc, dc_s, ds_s = deltanet(proj, N_PROMPT, DEC_BATCH, DEC_SEQ, DN_SPS, math.gcd(DEC_SEQ, DN_CHUNK),
                                  state_dn_conv[l], state_dn[l], *dn_args, y_prev=(yc,))

        k_new = proj[:, W_A:2 * W_A]
        v_new = proj[:, 2 * W_A:3 * W_A]
        ks_p.append(k_new[:N_PROMPT].reshape(BATCH, SEQ, H_A, 2, DH_A))
        vs_p.append(v_new[:N_PROMPT].reshape(BATCH, SEQ, H_A, DV_A))
        ks_s.append(k_new[N_PROMPT:].reshape(DEC_BATCH, DEC_SEQ, H_A, 2, DH_A))
        vs_s.append(v_new[N_PROMPT:].reshape(DEC_BATCH, DEC_SEQ, H_A, DV_A))
        for lst, vals in ((st_p, (cb_p, dc_p, ds_p, lb_p, lh_p[:, 0, :])), (st_s, (cb_s, dc_s, ds_s, lb_s, lh_s[:, 0, :]))):
            for i, val in enumerate(vals):
                lst[i].append(val)

        x = mix_out_residual((ya, yb, yc, yd), w_out[l].astype(bf16), x, g1, tm=1088, tn=1024)

        j = l // 2
        if l % 2 == 0:
            hn = norm_mod(x, norm_ffn[l][None, :], sc2, sh2)
            h = swiglu_up(hn, ffn_w_gate[j].astype(bf16), ffn_w_up[j].astype(bf16), tm=1088, tn=512)
            x = matmul_gated_residual(h, ffn_w_down[j].astype(bf16), x, g2, tm=1088, tn=512, tk=D_FF // 2)
        else:
            router = jnp.concatenate([moe_router[j], jnp.zeros((D_MODEL, LANES - N_EXPERTS), f32)], axis=1)
            hn, logits = norm_mod(x, norm_ffn[l][None, :], sc2, sh2, router=router)
            plan = moe_route(logits[:, :N_EXPERTS], D_FF_EXPERT // MOE_TN)
            xs = jnp.take(hn, plan['row_src'], axis=0)
            h = moe_up(plan['item_m'], plan['item_n'], plan['item_e'], plan['item_v'], xs,
                       moe_w_gate[j], moe_w_up[j])
            ys = moe_down(plan['tile_e'], plan['n_used'], h, moe_w_down[j].astype(bf16))
            gates = plan['gates']
            dest = plan['dest']
            f = (gates[:, 0:1] * jnp.take(ys, dest[:, 0], axis=0)
                 + gates[:, 1:2] * jnp.take(ys, dest[:, 1], axis=0))
            g2_tok = jnp.repeat(g2, SUBLANES, axis=0)
            x = x + g2_tok * f

    y = final_norm(x, final_norm_w[None, :])
    y_prompt = y[:N_PROMPT].reshape(BATCH, SEQ, D_MODEL)
    y_sample = y[N_PROMPT:].reshape(DEC_BATCH, DEC_SEQ, D_MODEL)
    sp_ = [jnp.stack(o) for o in st_p]
    ss_ = [jnp.stack(o) for o in st_s]
    return (y_prompt, y_sample, jnp.stack(ks_p), jnp.stack(vs_p), jnp.stack(ks_s), jnp.stack(vs_s),
            sp_[0], ss_[0], sp_[1], ss_[1], sp_[2], ss_[2], sp_[3], ss_[3], sp_[4], ss_[4])
```

```python
import functools
import math

import jax
import jax.numpy as jnp
from jax import lax
from jax.experimental import pallas as pl
from jax.experimental.pallas import tpu as pltpu

D_MODEL = 2048
BATCH = 8
SEQ = 2048
DEPTH = 2
DEC_BATCH = 128
DEC_SEQ = 8
W_GROUP = D_MODEL // 4
W_A = W_B = W_C = W_D = W_GROUP
H_A = 4
DH_A = W_A // (2 * H_A)
DV_A = 2 * DH_A
Q_BLOCK = 128
CONV_B = 3
H_C = 4
DK_C = W_C // H_C
DV_C = W_C // H_C
CONV_C = 4
DN_CHUNK = 64
H_D = 4
BW_D = W_D // H_D
CONV_D = 4
RG_C = 8.0
D_FF = 5632
N_EXPERTS = 8
TOP_K = 2
D_FF_EXPERT = 7168
N_MOD = 6
EPS = 1e-6
N_A = 3 * W_A
N_B = 3 * W_B
N_C = 4 * W_C + 2 * H_C
N_D = 2 * W_D

N_PROMPT = BATCH * SEQ
N_SAMPLE = DEC_BATCH * DEC_SEQ
N_TOK = N_PROMPT + N_SAMPLE
SUBLANES = 8
LANES = 128

OFF_A = 0
OFF_B = N_A
OFF_C = N_A + N_B
OFF_D = OFF_C + 4 * W_C
OFF_AB = OFF_D + N_D
N_IN_PAD = OFF_AB + LANES

VMEM_LIMIT = 48 * 1024 * 1024
MOE_TM = 512
MOE_TN = 512
MOE_TK = 1024
MOE_TILES = (TOP_K * N_TOK) // MOE_TM + N_EXPERTS
MOE_ROWS = MOE_TILES * MOE_TM
ROW_TILE = 512
ATT_TQ = 512
ATT_HPB = 2
LRU_CHUNK = 256
LRU_SPS = 16
DN_SPS = 8
DN_SUB_PROMPT = 4

f32 = jnp.float32
bf16 = jnp.bfloat16


def _params(sem):
    return pltpu.CompilerParams(dimension_semantics=sem, vmem_limit_bytes=VMEM_LIMIT)


def _silu(x):
    return x * (1.0 / (1.0 + jnp.exp(-x)))


def _mod_kernel(c_ref, w_ref, b_ref, o_ref):
    c = _silu(c_ref[...]).astype(bf16)
    o_ref[...] = jnp.dot(c, w_ref[0].astype(bf16), preferred_element_type=f32) + b_ref[0]


def modulation(c_all, w, b, layer):
    r = c_all.shape[0]
    n = w.shape[2]
    tn = 1024
    return pl.pallas_call(
        _mod_kernel,
        grid=(n // tn,),
        in_specs=[pl.BlockSpec((r, D_MODEL), lambda j: (0, 0)),
                  pl.BlockSpec((1, D_MODEL, tn), lambda j: (layer, 0, j)),
                  pl.BlockSpec((1, 1, tn), lambda j: (layer, 0, j))],
        out_specs=pl.BlockSpec((r, tn), lambda j: (0, j)),
        out_shape=jax.ShapeDtypeStruct((r, n), f32),
        compiler_params=_params(("arbitrary",)),
        name="modulation",
    )(c_all, w, b)


def _mod_specs(idx, tile_rows, col_block):
    tiles_per_req = SEQ // tile_rows
    npt = N_PROMPT // tile_rows

    def p_map(*ids):
        return (idx, jnp.minimum(ids[0] // tiles_per_req, BATCH - 1), 0, col_block(*ids))

    def s_map(*ids):
        return (idx, jnp.maximum(ids[0] - npt, 0), col_block(*ids))

    return p_map, s_map


def _apply_rows(vals, i, tile_rows, fn_prompt, fn_sample):
    npt = N_PROMPT // tile_rows

    @pl.when(i < npt)
    def _():
        fn_prompt(*vals)

    @pl.when(i >= npt)
    def _():
        fn_sample(*vals)


def _bcast_groups(a, m):
    rows, c = a.shape
    return a.reshape(rows // SUBLANES, SUBLANES, c), m[:, None, :]


def _norm_mod_emit(x_ref, nw_ref, scp_ref, shp_ref, scs_ref, shs_ref, emit):
    x = x_ref[...]
    y = x * lax.rsqrt(jnp.mean(x * x, axis=-1, keepdims=True) + EPS) * nw_ref[...]
    tr = x.shape[0]

    def prompt(y):
        emit(y * (1.0 + scp_ref[0, 0]) + shp_ref[0, 0])

    def sample(y):
        y3, sc = _bcast_groups(y, scs_ref[0])
        emit((y3 * (1.0 + sc) + shs_ref[0][:, None, :]).reshape(tr, D_MODEL))

    _apply_rows((y,), pl.program_id(0), tr, prompt, sample)


def _norm_mod_kernel(x_ref, nw_ref, scp_ref, shp_ref, scs_ref, shs_ref, o_ref):
    def emit(hn):
        o_ref[...] = hn.astype(o_ref.dtype)

    _norm_mod_emit(x_ref, nw_ref, scp_ref, shp_ref, scs_ref, shs_ref, emit)


def _norm_mod_router_kernel(x_ref, nw_ref, scp_ref, shp_ref, scs_ref, shs_ref, r_ref, o_ref, lg_ref):
    def emit(hn):
        bits = lax.bitcast_convert_type(hn.astype(bf16).astype(f32), jnp.uint32)
        half = D_MODEL // 2
        o_ref[...] = (bits[:, :half] >> 16) | bits[:, half:]
        lg_ref[...] = jnp.dot(hn, r_ref[...], preferred_element_type=f32, precision=lax.Precision.HIGHEST)

    _norm_mod_emit(x_ref, nw_ref, scp_ref, shp_ref, scs_ref, shs_ref, emit)


def norm_mod(x, nw, mod_p, mod_s, i_scale, i_shift, router=None):
    tr = ROW_TILE
    tg = tr // SUBLANES
    zero = lambda i: 0
    scp_map, scs_map = _mod_specs(i_scale, tr, zero)
    shp_map, shs_map = _mod_specs(i_shift, tr, zero)
    in_specs = [pl.BlockSpec((tr, D_MODEL), lambda i: (i, 0)),
                pl.BlockSpec((1, D_MODEL), lambda i: (0, 0)),
                pl.BlockSpec((1, 1, 1, D_MODEL), scp_map), pl.BlockSpec((1, 1, 1, D_MODEL), shp_map),
                pl.BlockSpec((1, tg, D_MODEL), scs_map), pl.BlockSpec((1, tg, D_MODEL), shs_map)]
    out_spec = pl.BlockSpec((tr, D_MODEL), lambda i: (i, 0))
    out_shape = jax.ShapeDtypeStruct((N_TOK, D_MODEL), bf16)
    if router is not None:
        out_spec = pl.BlockSpec((tr, D_MODEL // 2), lambda i: (i, 0))
        out_shape = jax.ShapeDtypeStruct((N_TOK, D_MODEL // 2), jnp.uint32)
    if router is None:
        return pl.pallas_call(
            _norm_mod_kernel, grid=(N_TOK // tr,), in_specs=in_specs, out_specs=out_spec,
            out_shape=out_shape, compiler_params=_params(("arbitrary",)), name="norm_mod",
        )(x, nw, mod_p, mod_p, mod_s, mod_s)
    return pl.pallas_call(
        _norm_mod_router_kernel, grid=(N_TOK // tr,),
        in_specs=in_specs + [pl.BlockSpec((D_MODEL, LANES), lambda i: (0, 0))],
        out_specs=[out_spec, pl.BlockSpec((tr, LANES), lambda i: (i, 0))],
        out_shape=[out_shape, jax.ShapeDtypeStruct((N_TOK, LANES), f32)],
        compiler_params=_params(("arbitrary",)), name="norm_mod_router",
    )(x, nw, mod_p, mod_p, mod_s, mod_s, router)


def _final_norm_kernel(x_ref, nw_ref, op_ref, os_ref, *, n_prompt_tiles):
    x = x_ref[...]
    y = x * lax.rsqrt(jnp.mean(x * x, axis=-1, keepdims=True) + EPS) * nw_ref[...]
    i = pl.program_id(0)

    @pl.when(i < n_prompt_tiles)
    def _():
        op_ref[...] = y

    @pl.when(i >= n_prompt_tiles)
    def _():
        os_ref[...] = y


def final_norm(x, nw):
    tr = ROW_TILE
    npt = N_PROMPT // tr
    return pl.pallas_call(
        functools.partial(_final_norm_kernel, n_prompt_tiles=npt), grid=(N_TOK // tr,),
        in_specs=[pl.BlockSpec((tr, D_MODEL), lambda i: (i, 0)),
                  pl.BlockSpec((1, D_MODEL), lambda i: (0, 0))],
        out_specs=[pl.BlockSpec((tr, D_MODEL), lambda i: (jnp.minimum(i, npt - 1), 0)),
                   pl.BlockSpec((tr, D_MODEL), lambda i: (jnp.maximum(i - npt, 0), 0))],
        out_shape=[jax.ShapeDtypeStruct((N_PROMPT, D_MODEL), f32), jax.ShapeDtypeStruct((N_SAMPLE, D_MODEL), f32)],
        compiler_params=_params(("arbitrary",)), name="final_norm",
    )(x, nw)


def _mm_kernel(x_ref, w_ref, o_ref):
    o_ref[...] = jnp.dot(x_ref[...], w_ref[...], preferred_element_type=f32).astype(o_ref.dtype)


def matmul(x, w, tm, tn, out_dtype):
    m, k = x.shape
    n = w.shape[1]
    return pl.pallas_call(
        _mm_kernel, grid=(m // tm, n // tn),
        in_specs=[pl.BlockSpec((tm, k), lambda i, j: (i, 0)),
                  pl.BlockSpec((k, tn), lambda i, j: (0, j))],
        out_specs=pl.BlockSpec((tm, tn), lambda i, j: (i, j)),
        out_shape=jax.ShapeDtypeStruct((m, n), out_dtype),
        compiler_params=_params(("arbitrary", "arbitrary")), name="matmul",
    )(x, w)


def _gated_residual_store(acc, x_ref, gp_ref, gs_ref, o_ref):
    tm, tn = acc.shape

    def prompt(acc):
        o_ref[...] = x_ref[...] + acc * gp_ref[0, 0]

    def sample(acc):
        a3, g = _bcast_groups(acc, gs_ref[0])
        o_ref[...] = x_ref[...] + (a3 * g).reshape(tm, tn)

    _apply_rows((acc,), pl.program_id(0), tm, prompt, sample)


def _mm_resid_kernel(y_ref, w_ref, x_ref, gp_ref, gs_ref, o_ref, acc_ref, *, nk):
    k = pl.program_id(2)
    part = jnp.dot(y_ref[...], w_ref[...], preferred_element_type=f32)

    @pl.when(k == 0)
    def _():
        acc_ref[...] = part

    @pl.when(k > 0)
    def _():
        acc_ref[...] += part

    @pl.when(k == nk - 1)
    def _():
        _gated_residual_store(acc_ref[...], x_ref, gp_ref, gs_ref, o_ref)


def matmul_gated_residual(y, w, x, mod_p, mod_s, i_gate, tm, tn, tk):
    m, kdim = y.shape
    n = w.shape[1]
    nk = kdim // tk
    gp_map, gs_map = _mod_specs(i_gate, tm, lambda i, j, k: j)
    return pl.pallas_call(
        functools.partial(_mm_resid_kernel, nk=nk), grid=(m // tm, n // tn, nk),
        in_specs=[pl.BlockSpec((tm, tk), lambda i, j, k: (i, k)),
                  pl.BlockSpec((tk, tn), lambda i, j, k: (k, j)),
                  pl.BlockSpec((tm, tn), lambda i, j, k: (i, j)),
                  pl.BlockSpec((1, 1, 1, tn), gp_map),
                  pl.BlockSpec((1, tm // SUBLANES, tn), gs_map)],
        out_specs=pl.BlockSpec((tm, tn), lambda i, j, k: (i, j)),
        out_shape=jax.ShapeDtypeStruct((m, n), f32),
        scratch_shapes=[pltpu.VMEM((tm, tn), f32)],
        compiler_params=_params(("arbitrary", "arbitrary", "arbitrary")), name="matmul_gated_residual",
    )(y, w, x, mod_p, mod_s)


def _mix_out_kernel(ya_ref, yb_ref, yc_ref, yd_ref, w_ref, x_ref, gp_ref, gs_ref, o_ref):
    acc = jnp.dot(ya_ref[...], w_ref[0:W_GROUP, :], preferred_element_type=f32)
    for gi, y_ref in enumerate((yb_ref, yc_ref, yd_ref), start=1):
        acc = acc + jnp.dot(y_ref[...], w_ref[gi * W_GROUP:(gi + 1) * W_GROUP, :], preferred_element_type=f32)
    _gated_residual_store(acc, x_ref, gp_ref, gs_ref, o_ref)


def mix_out_residual(ys, w, x, mod_p, mod_s, i_gate, tm, tn):
    m, n = x.shape
    y_spec = pl.BlockSpec((tm, W_GROUP), lambda i, j: (i, 0))
    gp_map, gs_map = _mod_specs(i_gate, tm, lambda i, j: j)
    return pl.pallas_call(
        _mix_out_kernel, grid=(m // tm, n // tn),
        in_specs=[y_spec, y_spec, y_spec, y_spec,
                  pl.BlockSpec((D_MODEL, tn), lambda i, j: (0, j)),
                  pl.BlockSpec((tm, tn), lambda i, j: (i, j)),
                  pl.BlockSpec((1, 1, 1, tn), gp_map),
                  pl.BlockSpec((1, tm // SUBLANES, tn), gs_map)],
        out_specs=pl.BlockSpec((tm, tn), lambda i, j: (i, j)),
        out_shape=jax.ShapeDtypeStruct((m, n), f32),
        compiler_params=_params(("arbitrary", "arbitrary")), name="mix_out_residual",
    )(*ys, w, x, mod_p, mod_s)


def _swiglu_body(x, wg, wu):
    g = jnp.dot(x, wg, preferred_element_type=f32)
    u = jnp.dot(x, wu, preferred_element_type=f32)
    return _silu(g) * u


def _swiglu_kernel(x_ref, wg_ref, wu_ref, o_ref):
    o_ref[...] = _swiglu_body(x_ref[...], wg_ref[...], wu_ref[...]).astype(o_ref.dtype)


def swiglu_up(x, wg, wu, tm, tn):
    m, k = x.shape
    n = wg.shape[1]
    return pl.pallas_call(
        _swiglu_kernel, grid=(m // tm, n // tn),
        in_specs=[pl.BlockSpec((tm, k), lambda i, j: (i, 0)),
                  pl.BlockSpec((k, tn), lambda i, j: (0, j)),
                  pl.BlockSpec((k, tn), lambda i, j: (0, j))],
        out_specs=pl.BlockSpec((tm, tn), lambda i, j: (i, j)),
        out_shape=jax.ShapeDtypeStruct((m, n), bf16),
        compiler_params=_params(("arbitrary", "arbitrary")), name="swiglu_up",
    )(x, wg, wu)


def _unpack_tokens(words):
    lo = lax.bitcast_convert_type(words << 16, f32)
    hi = lax.bitcast_convert_type(words & jnp.uint32(0xFFFF0000), f32)
    return jnp.concatenate([lo, hi], axis=1).astype(bf16)


def _moe_up_kernel(m_ref, n_ref, e_ref, v_ref, x_ref, wg_ref, wu_ref, o_ref, wg_sc, wu_sc):
    i = pl.program_id(0)

    @pl.when(v_ref[i] == 2)
    def _():
        wg_sc[...] = wg_ref[0].astype(bf16)
        wu_sc[...] = wu_ref[0].astype(bf16)

    @pl.when(v_ref[i] > 0)
    def _():
        o_ref[...] = _swiglu_body(_unpack_tokens(x_ref[...]), wg_sc[...], wu_sc[...]).astype(o_ref.dtype)


def moe_up(item_m, item_n, item_e, item_v, xs, wg, wu):
    ff = wg.shape[2]
    nt = ff // MOE_TN
    n_items = MOE_TILES * nt
    grid_spec = pltpu.PrefetchScalarGridSpec(
        num_scalar_prefetch=4, grid=(n_items,),
        in_specs=[pl.BlockSpec((MOE_TM, D_MODEL // 2), lambda i, m, n, e, v: (m[i], 0)),
                  pl.BlockSpec((1, D_MODEL, MOE_TN), lambda i, m, n, e, v: (e[i], 0, n[i])),
                  pl.BlockSpec((1, D_MODEL, MOE_TN), lambda i, m, n, e, v: (e[i], 0, n[i]))],
        out_specs=pl.BlockSpec((MOE_TM, MOE_TN), lambda i, m, n, e, v: (m[i], n[i])),
        scratch_shapes=[pltpu.VMEM((D_MODEL, MOE_TN), bf16), pltpu.VMEM((D_MODEL, MOE_TN), bf16)])
    return pl.pallas_call(
        _moe_up_kernel, grid_spec=grid_spec,
        out_shape=jax.ShapeDtypeStruct((MOE_ROWS, ff), bf16),
        compiler_params=_params(("arbitrary",)), name="moe_up",
    )(item_m, item_n, item_e, item_v, xs, wg, wu)


def _moe_down_kernel(te_ref, nu_ref, h_ref, wd_ref, o_ref):
    m = pl.program_id(0)
    k = pl.program_id(1)

    @pl.when(m < nu_ref[0])
    def _():
        part = jnp.dot(h_ref[...], wd_ref[0].astype(bf16), preferred_element_type=f32)

        @pl.when(k == 0)
        def _():
            o_ref[...] = part

        @pl.when(k > 0)
        def _():
            o_ref[...] += part


def moe_down(tile_e, n_used, h, wd):
    nk = wd.shape[1] // MOE_TK

    def mclamp(m, nu):
        return jnp.minimum(m, nu[0] - 1)

    def kclamp(m, k, nu):
        return jnp.where(m < nu[0], k, nk - 1)

    grid_spec = pltpu.PrefetchScalarGridSpec(
        num_scalar_prefetch=2, grid=(MOE_TILES, nk),
        in_specs=[pl.BlockSpec((MOE_TM, MOE_TK), lambda m, k, te, nu: (mclamp(m, nu), kclamp(m, k, nu))),
                  pl.BlockSpec((1, MOE_TK, D_MODEL),
                               lambda m, k, te, nu: (te[mclamp(m, nu)], kclamp(m, k, nu), 0))],
        out_specs=pl.BlockSpec((MOE_TM, D_MODEL), lambda m, k, te, nu: (mclamp(m, nu), 0)))
    return pl.pallas_call(
        _moe_down_kernel, grid_spec=grid_spec,
        out_shape=jax.ShapeDtypeStruct((MOE_ROWS, D_MODEL), f32),
        compiler_params=_params(("arbitrary", "arbitrary")), name="moe_down",
    )(tile_e, n_used, h, wd)


def moe_route(logits, n_ff_tiles):
    top_v, top_i = lax.top_k(logits, TOP_K)
    gates = jax.nn.softmax(top_v, axis=-1)
    e_flat = top_i.reshape(-1).astype(jnp.int32)
    onehot = (e_flat[:, None] == jnp.arange(N_EXPERTS, dtype=jnp.int32)[None, :]).astype(jnp.int32)
    csum = jnp.cumsum(onehot, axis=0)
    counts = csum[-1]
    rank = jnp.sum((csum - onehot) * onehot, axis=-1)
    tiles_e = (counts + MOE_TM - 1) // MOE_TM
    tile_end = jnp.cumsum(tiles_e)
    tile_start = tile_end - tiles_e
    dest = tile_start[e_flat] * MOE_TM + rank
    tok = jnp.arange(TOP_K * N_TOK, dtype=jnp.int32) // TOP_K
    row_src = jnp.zeros((MOE_ROWS,), jnp.int32).at[dest].set(tok)
    n_used = tile_end[-1]
    t_ids = jnp.arange(MOE_TILES, dtype=jnp.int32)
    tile_e = jnp.minimum(jnp.searchsorted(tile_end, t_ids, side="right"), N_EXPERTS - 1).astype(jnp.int32)
    nt = n_ff_tiles
    items_e = tiles_e * nt
    item_end = jnp.cumsum(items_e)
    item_start = item_end - items_e
    total = item_end[-1]
    idx = jnp.arange(MOE_TILES * nt, dtype=jnp.int32)
    valid = idx < total
    idc = jnp.minimum(idx, total - 1)
    ie = jnp.minimum(jnp.searchsorted(item_end, idc, side="right"), N_EXPERTS - 1).astype(jnp.int32)
    local = idc - item_start[ie]
    te = jnp.maximum(tiles_e[ie], 1)
    item_n = (local // te).astype(jnp.int32)
    item_m = (tile_start[ie] + local % te).astype(jnp.int32)
    return dict(gates=gates, dest=dest.reshape(N_TOK, TOP_K), row_src=row_src, tile_e=tile_e,
                n_used=n_used.reshape(1).astype(jnp.int32), item_m=item_m, item_n=item_n, item_e=ie,
                item_v=jnp.where(valid, jnp.where(local % te == 0, 2, 1), 0).astype(jnp.int32))


def _smem_spec():
    return pl.BlockSpec(memory_space=pltpu.SMEM)


def _subln(o, w, gain):
    return o * lax.rsqrt(jnp.mean(o * o, axis=-1, keepdims=True) + EPS) * w * gain


def _attn_prompt_kernel(lam_ref, slope_ref, q_ref, k_ref, v_ref, w_ref, o_ref, m_sc, acc_sc, *, tq, hpb, gain):
    hb = pl.program_id(1)
    qi = pl.program_id(2)
    lane = lax.broadcasted_iota(jnp.int32, (tq, DV_A), 1)
    row = lax.broadcasted_iota(jnp.int32, (2 * tq, tq), 0)
    col = lax.broadcasted_iota(jnp.int32, (2 * tq, tq), 1)
    rel = (col - jnp.where(row >= tq, row - tq, row)).astype(f32)
    ones_v = jnp.ones((tq, DV_A), bf16)
    qss, slopes, biases = [], [], []
    for hh in range(hpb):
        q = q_ref[:, hh * DV_A:(hh + 1) * DV_A] * (DH_A ** -0.5)
        qss.append(jnp.concatenate([jnp.where(lane < DH_A, q, 0.0), jnp.where(lane >= DH_A, q, 0.0)],
                                   axis=0).astype(bf16))
        slopes.append(slope_ref[hb * hpb + hh])
        biases.append(slopes[hh] * rel)
    m_sc[...] = jnp.full(m_sc.shape, -jnp.inf, f32)
    acc_sc[...] = jnp.zeros(acc_sc.shape, f32)

    def chunk_step(j, diagonal):
        off = pl.multiple_of(j * tq, tq)
        blocks = (jnp.zeros((1, 1), jnp.int32) + (qi - j) * tq).astype(f32)
        for hh in range(hpb):
            cs = slice(hh * DV_A, (hh + 1) * DV_A)
            kc = k_ref[pl.ds(off, tq), cs].astype(bf16)
            vc = jnp.concatenate([v_ref[pl.ds(off, tq), cs].astype(bf16), ones_v], axis=1)
            s = lax.dot_general(qss[hh], kc, (((1,), (1,)), ((), ())), preferred_element_type=f32)
            s = s + biases[hh] - slopes[hh] * blocks
            if diagonal:
                s = jnp.where(rel <= 0, s, -1e30)
            m_prev = m_sc[hh]
            m_new = jnp.maximum(m_prev, jnp.max(s, axis=-1, keepdims=True))
            pr = jnp.exp(s - m_new).astype(bf16)
            acc_sc[hh] = jnp.exp(m_prev - m_new) * acc_sc[hh] + jnp.dot(pr, vc, preferred_element_type=f32)
            m_sc[hh] = m_new

    def body(j, carry):
        chunk_step(j, False)
        return carry

    lax.fori_loop(0, qi, body, 0)
    chunk_step(qi, True)
    for hh in range(hpb):
        acc = acc_sc[hh]
        o = acc[:, :DV_A] / acc[:, DV_A:]
        o = o[:tq] - lam_ref[0] * o[tq:]
        o_ref[:, hh * DV_A:(hh + 1) * DV_A] = _subln(o, w_ref[...], gain).astype(o_ref.dtype)


def attn_prompt(proj, lam, slopes, subln_w, gain, batch, seq, tq, hpb):
    nq = seq // tq
    wb = hpb * DV_A
    nhb = H_A // hpb
    return pl.pallas_call(
        functools.partial(_attn_prompt_kernel, tq=tq, hpb=hpb, gain=gain),
        grid=(batch, nhb, nq),
        in_specs=[_smem_spec(), _smem_spec(),
                  pl.BlockSpec((tq, wb), lambda b, h, qi: (b * nq + qi, h)),
                  pl.BlockSpec((seq, wb), lambda b, h, qi: (b, nhb + h)),
                  pl.BlockSpec((seq, wb), lambda b, h, qi: (b, 2 * nhb + h)),
                  pl.BlockSpec((1, DV_A), lambda b, h, qi: (0, 0))],
        out_specs=pl.BlockSpec((tq, wb), lambda b, h, qi: (b * nq + qi, h)),
        out_shape=jax.ShapeDtypeStruct((N_TOK, W_A), bf16),
        scratch_shapes=[pltpu.VMEM((hpb, 2 * tq, 1), f32), pltpu.VMEM((hpb, 2 * tq, 2 * DV_A), f32)],
        compiler_params=_params(("arbitrary", "arbitrary", "arbitrary")), name="attn_prompt",
    )(lam, slopes, proj, proj, proj, subln_w)


def _attn_sample_kernel(pt_ref, lam_ref, slope_ref, q_ref, kn_ref, vn_ref, w_ref, *rest, t, n_pages, page, gain):
    k_refs = rest[:n_pages]
    v_refs = rest[n_pages:2 * n_pages]
    o_ref = rest[2 * n_pages + 1]
    s_sc = rest[2 * n_pages + 2]
    n_rep = 2 * H_A
    rows = n_rep * t
    n_past = n_pages * page
    q = q_ref[...] * (DH_A ** -0.5)
    q_rep = jnp.concatenate([q] * n_rep, axis=0)
    r_grp = lax.broadcasted_iota(jnp.int32, (rows, W_A), 0) // t
    c_grp = lax.broadcasted_iota(jnp.int32, (rows, W_A), 1) // DH_A
    qbd = jnp.where(r_grp == c_grp, q_rep, 0.0).astype(bf16)
    for j in range(n_pages):
        s_sc[:, j * page:(j + 1) * page] = jnp.dot(qbd, k_refs[j][0, 0].astype(bf16), preferred_element_type=f32)
    s_new = lax.dot_general(qbd, kn_ref[...].astype(bf16), (((1,), (1,)), ((), ())), preferred_element_type=f32)

    row1 = lax.broadcasted_iota(jnp.int32, (rows, 1), 0)
    head1 = row1 // (2 * t)
    slope = jnp.where(head1 == 0, slope_ref[0],
                      jnp.where(head1 == 1, slope_ref[1], jnp.where(head1 == 2, slope_ref[2], slope_ref[3])))
    t_row = (row1 % t)
    kpos = lax.broadcasted_iota(jnp.int32, (rows, n_past), 1)
    s_past = s_sc[...] - slope * (n_past + t_row - kpos).astype(f32)
    dist_new = (t_row - lax.broadcasted_iota(jnp.int32, (rows, t), 1)).astype(f32)
    s_new = jnp.where(dist_new >= 0, s_new - slope * dist_new, -1e30)
    m = jnp.maximum(jnp.max(s_past, axis=-1, keepdims=True), jnp.max(s_new, axis=-1, keepdims=True))
    p_past = jnp.exp(s_past - m)
    p_new = jnp.exp(s_new - m)
    denom = jnp.sum(p_past, axis=-1, keepdims=True) + jnp.sum(p_new, axis=-1, keepdims=True)
    p_past = p_past.astype(bf16)
    p_new = p_new.astype(bf16)
    lam = lam_ref[0]
    w = w_ref[...]
    for h in range(H_A):
        vh = jnp.concatenate([v_refs[j][0, 0, pl.ds(h, page, stride=H_A), :] for j in range(n_pages)], axis=0)
        rs = slice(h * 2 * t, (h + 1) * 2 * t)
        oh = jnp.dot(p_past[rs], vh.astype(bf16), preferred_element_type=f32)
        oh = oh + jnp.dot(p_new[rs], vn_ref[:, h * DV_A:(h + 1) * DV_A].astype(bf16), preferred_element_type=f32)
        oh = oh / denom[rs]
        o = oh[:t] - lam * oh[t:]
        o_ref[:, h * DV_A:(h + 1) * DV_A] = _subln(o, w, gain).astype(o_ref.dtype)


def attn_sample(proj, ckt, cvr, layer, page_table, lam, slopes, subln_w, gain, row0, nseq, t, y_rows):
    n_pages = page_table.shape[1]
    page = ckt.shape[-1]
    rb0 = row0 // t

    def kv_spec(j):
        return pl.BlockSpec((1, 1, W_A, page), lambda b, pt, j=j: (layer, pt[b * n_pages + j], 0, 0))

    in_specs = ([_smem_spec(), _smem_spec(),
                 pl.BlockSpec((t, W_A), lambda b, pt: (rb0 + b, 0)),
                 pl.BlockSpec((t, W_A), lambda b, pt: (rb0 + b, 1)),
                 pl.BlockSpec((t, W_A), lambda b, pt: (rb0 + b, 2)),
                 pl.BlockSpec((1, DV_A), lambda b, pt: (0, 0))]
                + [kv_spec(j) for j in range(n_pages)] + [kv_spec(j) for j in range(n_pages)]
                + [pl.BlockSpec(memory_space=pl.ANY)])
    grid_spec = pltpu.PrefetchScalarGridSpec(
        num_scalar_prefetch=1, grid=(nseq,), in_specs=in_specs,
        out_specs=pl.BlockSpec((t, W_A), lambda b, pt: (rb0 + b, 0)),
        scratch_shapes=[pltpu.VMEM((2 * H_A * t, n_pages * page), f32)])
    args = (page_table.reshape(-1), lam, slopes, proj, proj, proj, subln_w, *([ckt] * n_pages), *([cvr] * n_pages),
            y_rows)
    return pl.pallas_call(
        functools.partial(_attn_sample_kernel, t=t, n_pages=n_pages, page=page, gain=gain),
        grid_spec=grid_spec, out_shape=jax.ShapeDtypeStruct(y_rows.shape, y_rows.dtype),
        input_output_aliases={len(args) - 1: 0},
        compiler_params=_params(("arbitrary",)), name="attn_sample",
    )(*args)


def _conv_step(f_sc, x3, w_ref, buf0_ref, buf_out, width, chunk, first):
    lo = SUBLANES - (width - 1)

    @pl.when(first)
    def _():
        f_sc[:, 0:SUBLANES, :] = jnp.zeros((f_sc.shape[0], SUBLANES, f_sc.shape[2]), f32)
        f_sc[:, lo:SUBLANES, :] = buf0_ref[...]

    f_sc[:, SUBLANES:SUBLANES + chunk, :] = x3
    out = w_ref[0:1, :][None] * f_sc[:, lo:lo + chunk, :]
    for j in range(1, width):
        out = out + w_ref[j:j + 1, :][None] * f_sc[:, lo + j:lo + j + chunk, :]
    buf_out[...] = f_sc[:, chunk + lo:chunk + SUBLANES, :]
    f_sc[:, 0:SUBLANES, :] = f_sc[:, chunk:chunk + SUBLANES, :]
    return out


def _sigmoid(x):
    return 1.0 / (1.0 + jnp.exp(-x))


def _gelu_tanh(x):
    return 0.5 * x * (1.0 + jnp.tanh(0.7978845608028654 * (x + 0.044715 * (x * x * x))))


def _conv_lru_kernel(pb_ref, pd_ref, cb0_ref, lb0_ref, h0_ref, wb_ref, wd_ref, bd_ref, wa_ref, ba_ref, wx_ref,
                     bx_ref, sp_ref, *rest, sps, chunk):
    yb_ref, yd_ref, cb_out, lb_out, h_out, fb_sc, fd_sc, h_sc = rest[-8:]
    first = pl.program_id(1) == 0
    rows = sps * chunk
    pb = pb_ref[...]
    bg, cg, xt = pb[:, :W_B], pb[:, W_B:2 * W_B], pb[:, 2 * W_B:]
    conv_b = _conv_step(fb_sc, (cg * xt).reshape(sps, chunk, W_B), wb_ref, cb0_ref, cb_out, CONV_B, chunk, first)
    yb_ref[...] = (bg * conv_b.reshape(rows, W_B)).astype(yb_ref.dtype)

    pd = pd_ref[...]
    xd, gd = pd[:, :W_D], pd[:, W_D:]
    conv_d = _conv_step(fd_sc, xd.reshape(sps, chunk, W_D), wd_ref, lb0_ref, lb_out, CONV_D, chunk, first)
    xf = conv_d.reshape(rows, W_D) + bd_ref[...]
    xfb = xf.astype(bf16)
    ra = jnp.concatenate([jnp.dot(xfb[:, h * BW_D:(h + 1) * BW_D], wa_ref[h].astype(bf16),
                                  preferred_element_type=f32) for h in range(H_D)], axis=-1)
    rx = jnp.concatenate([jnp.dot(xfb[:, h * BW_D:(h + 1) * BW_D], wx_ref[h].astype(bf16),
                                  preferred_element_type=f32) for h in range(H_D)], axis=-1)
    r = _sigmoid(ra + ba_ref[...])
    ig = _sigmoid(rx + bx_ref[...])
    log_a = -RG_C * r * sp_ref[...]
    a = jnp.exp(log_a)
    th = jnp.tanh(log_a)
    bx = jnp.sqrt(-2.0 * th / (1.0 - th)) * (ig * xf)

    pos = lax.broadcasted_iota(jnp.int32, (rows, W_D), 0) % chunk
    step = 1
    while step < chunk:
        a_s = pltpu.roll(a, step, 0)
        b_s = pltpu.roll(bx, step, 0)
        live = pos >= step
        bx = jnp.where(live, a * b_s + bx, bx)
        a = jnp.where(live, a * a_s, a)
        step *= 2

    @pl.when(first)
    def _():
        h_sc[...] = h0_ref[...]

    h_in = jnp.broadcast_to(h_sc[...], (sps, chunk, W_D)).reshape(rows, W_D)
    hs = bx + a * h_in
    h_last = hs.reshape(sps, chunk, W_D)[:, chunk - 1:chunk, :]
    h_sc[...] = h_last
    h_out[...] = h_last
    yd_ref[...] = (hs * _gelu_tanh(gd)).astype(yd_ref.dtype)


def conv_lru(proj, row0, nseq, t, sps, chunk, conv_b_buf, lru_buf, lru_h, wb, wd, bd, wa, ba, wx, bx, sp, y_prev=()):
    rows = sps * chunk
    nc = t // chunk
    rb0 = row0 // rows

    def rmap(s, c):
        return rb0 + s * nc + c

    def full(shape):
        return pl.BlockSpec(shape, lambda s, c: (0,) * len(shape))

    def per_seq(shape):
        return pl.BlockSpec((sps,) + shape, lambda s, c: (s,) + (0,) * len(shape))

    in_specs = [pl.BlockSpec((rows, N_B), lambda s, c: (rmap(s, c), OFF_B // N_B)),
                pl.BlockSpec((rows, N_D), lambda s, c: (rmap(s, c), OFF_D // N_D)),
                per_seq((CONV_B - 1, W_B)), per_seq((CONV_D - 1, W_D)), per_seq((1, W_D)),
                full((CONV_B, W_B)), full((CONV_D, W_D)), full((1, W_D)),
                full((H_D, BW_D, BW_D)), full((1, W_D)), full((H_D, BW_D, BW_D)), full((1, W_D)), full((1, W_D))]
    in_specs = in_specs + [pl.BlockSpec(memory_space=pl.ANY)] * len(y_prev)
    out_specs = [pl.BlockSpec((rows, W_B), lambda s, c: (rmap(s, c), 0)),
                 pl.BlockSpec((rows, W_D), lambda s, c: (rmap(s, c), 0)),
                 per_seq((CONV_B - 1, W_B)), per_seq((CONV_D - 1, W_D)), per_seq((1, W_D))]
    out_shape = [jax.ShapeDtypeStruct((N_TOK, W_B), bf16), jax.ShapeDtypeStruct((N_TOK, W_D), bf16),
                 jax.ShapeDtypeStruct((nseq, CONV_B - 1, W_B), f32),
                 jax.ShapeDtypeStruct((nseq, CONV_D - 1, W_D), f32),
                 jax.ShapeDtypeStruct((nseq, 1, W_D), f32)]
    return pl.pallas_call(
        functools.partial(_conv_lru_kernel, sps=sps, chunk=chunk),
        grid=(nseq // sps, nc), in_specs=in_specs, out_specs=out_specs, out_shape=out_shape,
        scratch_shapes=[pltpu.VMEM((sps, chunk + SUBLANES, W_B), f32), pltpu.VMEM((sps, chunk + SUBLANES, W_D), f32),
                        pltpu.VMEM((sps, 1, W_D), f32)],
        input_output_aliases={13 + i: i for i in range(len(y_prev))},
        compiler_params=_params(("arbitrary", "arbitrary")), name="conv_lru",
    )(proj, proj, conv_b_buf, lru_buf, lru_h, wb, wd, bd, wa, ba, wx, bx, sp, *y_prev)


_HI = lax.Precision.HIGHEST


def _dot_nt(a, b, precision=None):
    return lax.dot_general(a, b, (((1,), (1,)), ((), ())), preferred_element_type=f32, precision=precision)


def _dot_tn(a, b):
    return lax.dot_general(a, b, (((0,), (0,)), ((), ())), preferred_element_type=f32)


def _hi_lo(x):
    hi = x.astype(bf16).astype(f32)
    return hi, x - hi


def _dot_f32(a, b, chunk):
    if chunk % 16:
        return jnp.dot(a, b, preferred_element_type=f32, precision=_HI)
    a_hi, a_lo = _hi_lo(a)
    b_hi, b_lo = _hi_lo(b)
    lhs = jnp.concatenate([a_hi, a_hi, a_lo], axis=1).astype(bf16)
    rhs = jnp.concatenate([b_hi, b_lo, b_hi], axis=0).astype(bf16)
    return jnp.dot(lhs, rhs, preferred_element_type=f32)


def _cumsum_rows(tri, x, chunk):
    if chunk % 16:
        return jnp.dot(tri, x, preferred_element_type=f32, precision=_HI)
    hi, r1 = _hi_lo(x)
    mid, lo = _hi_lo(r1)
    lhs = jnp.concatenate([tri, tri, tri], axis=1).astype(bf16)
    rhs = jnp.concatenate([hi, mid, lo], axis=0).astype(bf16)
    return jnp.dot(lhs, rhs, preferred_element_type=f32)


def _row_bcast(gc, pick, chunk):
    sel = pick.astype(f32)
    if chunk % 16:
        return _dot_nt(sel, gc, precision=_HI)
    hi, r1 = _hi_lo(gc)
    mid, lo = _hi_lo(r1)
    lhs = jnp.concatenate([sel, sel, sel], axis=1).astype(bf16)
    rhs = jnp.concatenate([hi, mid, lo], axis=1).astype(bf16)
    return _dot_nt(lhs, rhs)


def _deltanet_kernel(qkv_ref, z_ref, ab_ref, buf0_ref, s0_ref, wc_ref, nega_ref, dtb_ref, nw_ref,
                     *rest, sps, nsub, chunk):
    y_ref, buf_out, s_out, f_sc, s_sc = rest[-5:]
    first = pl.program_id(1) == 0
    span_rows = nsub * chunk
    rows = sps * span_rows
    wq = 3 * W_C
    conv = _conv_step(f_sc, qkv_ref[...].reshape(sps, span_rows, wq), wc_ref, buf0_ref, buf_out, CONV_C,
                      span_rows, first)
    conv = conv.reshape(rows, wq)
    qkv = conv * _sigmoid(conv)

    @pl.when(first)
    def _():
        s_sc[...] = s0_ref[...]

    ab = ab_ref[...]
    xg = ab + dtb_ref[...]
    g_all = nega_ref[...] * (jnp.maximum(xg, 0.0) + jnp.log1p(jnp.exp(-jnp.abs(xg))))
    beta_all = _sigmoid(ab)
    z = z_ref[...]
    ri = lax.broadcasted_iota(jnp.int32, (chunk, chunk), 0)
    ci = lax.broadcasted_iota(jnp.int32, (chunk, chunk), 1)
    causal = ri >= ci
    strict = ri > ci
    tri = causal.astype(f32)
    eye = (ri == ci).astype(f32)
    lane = lax.broadcasted_iota(jnp.int32, (chunk, LANES), 1)
    nw = nw_ref[...]
    spans = [(i, c) for i in range(sps) for c in range(nsub)]
    units = [(i, c, h) for i, c in spans for h in range(H_C)]

    def rows_of(i, c):
        r0 = (i * nsub + c) * chunk
        return slice(r0, r0 + chunk)

    gcs = {sp: _cumsum_rows(tri, g_all[rows_of(*sp)], chunk) for sp in spans}

    qs, ks, gcols, rhss, qks, negs = [], [], [], [], [], []
    for i, c, h in units:
        rs = rows_of(i, c)
        gc = gcs[(i, c)]
        q = qkv[rs, h * DK_C:(h + 1) * DK_C]
        k = qkv[rs, W_C + h * DK_C:W_C + (h + 1) * DK_C]
        v = qkv[rs, 2 * W_C + h * DV_C:2 * W_C + (h + 1) * DV_C]
        q = q * lax.rsqrt(jnp.sum(q * q, axis=-1, keepdims=True) + EPS) * (DK_C ** -0.5)
        k = k * lax.rsqrt(jnp.sum(k * k, axis=-1, keepdims=True) + EPS)
        beta = beta_all[rs, H_C + h:H_C + h + 1]
        gcol = gc[:, h:h + 1]
        grow = _row_bcast(gc, lane == h, chunk)
        decay = jnp.where(causal, jnp.exp(jnp.where(causal, gcol - grow, 0.0)), 0.0)
        eg = jnp.exp(gcol)
        kb = k * beta
        negs.append(jnp.where(strict, _dot_nt(kb, k) * decay, 0.0) * -1.0)
        qks.append(_dot_nt(q, k) * decay)
        rhss.append(jnp.concatenate([v * beta, kb * eg], axis=1))
        qs.append(q * eg)
        ks.append(k)
        gcols.append(gcol)

    invs = [eye + n for n in negs]
    span = 2
    while span < chunk:
        negs = [_dot_f32(n, n, chunk) for n in negs]
        invs = [inv + _dot_f32(inv, n, chunk) for inv, n in zip(invs, negs)]
        span *= 2
    sols = [_dot_f32(inv, rhs, chunk) for inv, rhs in zip(invs, rhss)]

    states = {(i, h): s_sc[i, h] for i in range(sps) for h in range(H_C)}
    outs = {}
    for c in range(nsub):
        for idx, (i, cc, h) in enumerate(units):
            if cc != c:
                continue
            st = states[(i, h)]
            sol, gcol = sols[idx], gcols[idx]
            wq_s = jnp.dot(jnp.concatenate([sol[:, DV_C:], qs[idx]], axis=0), st, preferred_element_type=f32)
            v_new = sol[:, :DV_C] - wq_s[:chunk]
            outs[idx] = wq_s[chunk:] + jnp.dot(qks[idx], v_new, preferred_element_type=f32)
            g_last = gcol[chunk - 1:chunk, :]
            states[(i, h)] = st * jnp.exp(g_last) + _dot_tn(ks[idx] * jnp.exp(g_last - gcol), v_new)
    for (i, h), st in states.items():
        s_sc[i, h] = st
        s_out[i, h] = st
    for idx, (i, c, h) in enumerate(units):
        rs = rows_of(i, c)
        zh = z[rs, h * DV_C:(h + 1) * DV_C]
        o = outs[idx]
        o = o * lax.rsqrt(jnp.mean(o * o, axis=-1, keepdims=True) + EPS) * nw
        y_ref[rs, h * DV_C:(h + 1) * DV_C] = (o * (zh * _sigmoid(zh))).astype(y_ref.dtype)


def deltanet(proj, row0, nseq, t, sps, nsub, chunk, dn_buf, dn_s, wc, nega, dtb, nw, y_prev=()):
    rows = sps * nsub * chunk
    nc = t // (nsub * chunk)
    rb0 = row0 // rows
    wq = 3 * W_C

    def rmap(s, c):
        return rb0 + s * nc + c

    def full(shape):
        return pl.BlockSpec(shape, lambda s, c: (0,) * len(shape))

    def per_seq(shape):
        return pl.BlockSpec((sps,) + shape, lambda s, c: (s,) + (0,) * len(shape))

    in_specs = [pl.BlockSpec((rows, wq), lambda s, c: (rmap(s, c), OFF_C // wq)),
                pl.BlockSpec((rows, W_C), lambda s, c: (rmap(s, c), (OFF_C + wq) // W_C)),
                pl.BlockSpec((rows, LANES), lambda s, c: (rmap(s, c), OFF_AB // LANES)),
                per_seq((CONV_C - 1, wq)), per_seq((H_C, DK_C, DV_C)),
                full((CONV_C, wq)), full((1, LANES)), full((1, LANES)), full((1, DV_C))]
    in_specs = in_specs + [pl.BlockSpec(memory_space=pl.ANY)] * len(y_prev)
    out_specs = [pl.BlockSpec((rows, W_C), lambda s, c: (rmap(s, c), 0)),
                 per_seq((CONV_C - 1, wq)), per_seq((H_C, DK_C, DV_C))]
    out_shape = [jax.ShapeDtypeStruct((N_TOK, W_C), bf16),
                 jax.ShapeDtypeStruct((nseq, CONV_C - 1, wq), f32),
                 jax.ShapeDtypeStruct((nseq, H_C, DK_C, DV_C), f32)]
    return pl.pallas_call(
        functools.partial(_deltanet_kernel, sps=sps, nsub=nsub, chunk=chunk),
        grid=(nseq // sps, nc), in_specs=in_specs, out_specs=out_specs, out_shape=out_shape,
        scratch_shapes=[pltpu.VMEM((sps, nsub * chunk + SUBLANES, wq), f32),
                        pltpu.VMEM((sps, H_C, DK_C, DV_C), f32)],
        input_output_aliases={9 + i: i for i in range(len(y_prev))},
        compiler_params=_params(("arbitrary", "arbitrary")), name="deltanet",
    )(proj, proj, proj, dn_buf, dn_s, wc, nega, dtb, nw, *y_prev)


def _lane_row(v):
    return jnp.concatenate([v.astype(f32), jnp.zeros((LANES - v.shape[0],), f32)])[None, :]


def kernel(x_prompt, x_sample, cache_k, cache_v, state_conv_b, state_dn_conv, state_dn, state_lru_conv, state_lru_h, page_table, c_prompt, c_sample, w_mod, b_mod, norm_mix, norm_ffn, w_in, w_out, lam_q1, lam_k1, lam_q2, lam_k2, subln_w, conv_b_w, dn_conv_w, dn_a_log, dn_dt_bias, dn_norm_w, lru_conv_w, lru_conv_b, lru_wa, lru_ba, lru_wx, lru_bx, lru_lambda, ffn_w_gate, ffn_w_up, ffn_w_down, moe_router, moe_w_gate, moe_w_up, moe_w_down, final_norm_w):
    x = jnp.concatenate([x_prompt.reshape(N_PROMPT, D_MODEL), x_sample.reshape(N_SAMPLE, D_MODEL)], axis=0)
    c_all = jnp.concatenate([c_prompt, c_sample], axis=0)
    n_pool, page = cache_k.shape[1], cache_k.shape[2]
    ckt = jnp.transpose(cache_k, (0, 1, 3, 4, 5, 2)).reshape(DEPTH, n_pool, W_A, page)
    cvr = cache_v.reshape(DEPTH, n_pool, page * H_A, DV_A)
    slopes = jnp.exp2(-8.0 * jnp.arange(1, H_A + 1, dtype=f32) / H_A)

    zero_cb = jnp.zeros((BATCH, CONV_B - 1, W_B), f32)
    zero_dc = jnp.zeros((BATCH, CONV_C - 1, 3 * W_C), f32)
    zero_ds = jnp.zeros((BATCH, H_C, DK_C, DV_C), f32)
    zero_lc = jnp.zeros((BATCH, CONV_D - 1, W_D), f32)
    zero_lh = jnp.zeros((BATCH, 1, W_D), f32)

    ks_p, vs_p, ks_s, vs_s = [], [], [], []
    st_p = [[] for _ in range(5)]
    st_s = [[] for _ in range(5)]
    for l in range(DEPTH):
        mod = modulation(c_all, w_mod, b_mod[:, None, :], l)
        mod_p = jnp.transpose(mod[:BATCH].reshape(BATCH, N_MOD, D_MODEL), (1, 0, 2))[:, :, None, :]
        mod_s = jnp.transpose(mod[BATCH:].reshape(DEC_BATCH, N_MOD, D_MODEL), (1, 0, 2))

        hn = norm_mod(x, norm_mix[l][None, :], mod_p, mod_s, 1, 0)
        wl = w_in[l]
        n_cqkvz = 4 * W_C
        w_pack = jnp.concatenate(
            [wl[:, :N_A + N_B + n_cqkvz], wl[:, N_A + N_B + N_C:], wl[:, N_A + N_B + n_cqkvz:N_A + N_B + N_C],
             jnp.zeros((D_MODEL, LANES - 2 * H_C), f32)], axis=1).astype(bf16)
        proj = matmul(hn, w_pack, tm=1088, tn=896, out_dtype=f32)

        lam_init = 0.8 - 0.6 * math.exp(-0.3 * l)
        lam = (jnp.exp(jnp.sum(lam_q1[l] * lam_k1[l])) - jnp.exp(jnp.sum(lam_q2[l] * lam_k2[l])) + lam_init)
        lam = lam.astype(f32).reshape(1)
        gain = 1.0 - lam_init
        sw = subln_w[l][None, :]
        ya = attn_prompt(proj, lam, slopes, sw, gain, BATCH, SEQ, ATT_TQ, ATT_HPB)
        ya = attn_sample(proj, ckt, cvr, l, page_table, lam, slopes, sw, gain, N_PROMPT, DEC_BATCH, DEC_SEQ, ya)

        sp = jax.nn.softplus(-lru_lambda[l])[None, :]
        lru_args = (conv_b_w[l], lru_conv_w[l], lru_conv_b[l][None, :], lru_wa[l], lru_ba[l][None, :],
                    lru_wx[l], lru_bx[l][None, :], sp)
        yb, yd, cb_p, lb_p, lh_p = conv_lru(proj, 0, BATCH, SEQ, 1, LRU_CHUNK, zero_cb, zero_lc, zero_lh, *lru_args)
        yb, yd, cb_s, lb_s, lh_s = conv_lru(proj, N_PROMPT, DEC_BATCH, DEC_SEQ, LRU_SPS, DEC_SEQ,
                                            state_conv_b[l], state_lru_conv[l], state_lru_h[l][:, None, :],
                                            *lru_args, y_prev=(yb, yd))

        dn_args = (dn_conv_w[l], _lane_row(-jnp.exp(dn_a_log[l])), _lane_row(dn_dt_bias[l]), dn_norm_w[l][None, :])
        yc, dc_p, ds_p = deltanet(proj, 0, BATCH, SEQ, 1, DN_SUB_PROMPT, DN_CHUNK, zero_dc, zero_ds, *dn_args)
        yc, dc_s, ds_s = deltanet(proj, N_PROMPT, DEC_BATCH, DEC_SEQ, DN_SPS, 1, math.gcd(DEC_SEQ, DN_CHUNK),
                                  state_dn_conv[l], state_dn[l], *dn_args, y_prev=(yc,))

        k_new = proj[:, W_A:2 * W_A]
        v_new = proj[:, 2 * W_A:3 * W_A]
        ks_p.append(k_new[:N_PROMPT].reshape(BATCH, SEQ, H_A, 2, DH_A))
        vs_p.append(v_new[:N_PROMPT].reshape(BATCH, SEQ, H_A, DV_A))
        ks_s.append(k_new[N_PROMPT:].reshape(DEC_BATCH, DEC_SEQ, H_A, 2, DH_A))
        vs_s.append(v_new[N_PROMPT:].reshape(DEC_BATCH, DEC_SEQ, H_A, DV_A))
        for lst, vals in ((st_p, (cb_p, dc_p, ds_p, lb_p, lh_p[:, 0, :])), (st_s, (cb_s, dc_s, ds_s, lb_s, lh_s[:, 0, :]))):
            for i, val in enumerate(vals):
                lst[i].append(val)

        x = mix_out_residual((ya, yb, yc, yd), w_out[l].astype(bf16), x, mod_p, mod_s, 2, tm=1024, tn=1024)

        j = l // 2
        if l % 2 == 0:
            hn = norm_mod(x, norm_ffn[l][None, :], mod_p, mod_s, 4, 3)
            h = swiglu_up(hn, ffn_w_gate[j].astype(bf16), ffn_w_up[j].astype(bf16), tm=1088, tn=512)
            x = matmul_gated_residual(h, ffn_w_down[j].astype(bf16), x, mod_p, mod_s, 5, tm=1024, tn=512,
                                      tk=D_FF // 2)
        else:
            router = jnp.concatenate([moe_router[j], jnp.zeros((D_MODEL, LANES - N_EXPERTS), f32)], axis=1)
            hn, logits = norm_mod(x, norm_ffn[l][None, :], mod_p, mod_s, 4, 3, router=router)
            plan = moe_route(logits[:, :N_EXPERTS], D_FF_EXPERT // MOE_TN)
            xs = jnp.take(hn, plan['row_src'], axis=0)
            h = moe_up(plan['item_m'], plan['item_n'], plan['item_e'], plan['item_v'], xs,
                       moe_w_gate[j], moe_w_up[j])
            ys = moe_down(plan['tile_e'], plan['n_used'], h, moe_w_down[j])
            gates = plan['gates']
            dest = plan['dest']
            f = (gates[:, 0:1] * jnp.take(ys, dest[:, 0], axis=0)
                 + gates[:, 1:2] * jnp.take(ys, dest[:, 1], axis=0))
            g2_tok = jnp.concatenate([jnp.repeat(mod_p[5, :, 0, :], SEQ, axis=0),
                                      jnp.repeat(mod_s[5], DEC_SEQ, axis=0)], axis=0)
            x = x + g2_tok * f

    y_prompt, y_sample = final_norm(x, final_norm_w[None, :])
    y_prompt = y_prompt.reshape(BATCH, SEQ, D_MODEL)
    y_sample = y_sample.reshape(DEC_BATCH, DEC_SEQ, D_MODEL)
    sp_ = [jnp.stack(o) for o in st_p]
    ss_ = [jnp.stack(o) for o in st_s]
    return (y_prompt, y_sample, jnp.stack(ks_p), jnp.stack(vs_p), jnp.stack(ks_s), jnp.stack(vs_s),
            sp_[0], ss_[0], sp_[1], ss_[1], sp_[2], ss_[2], sp_[3], ss_[3], sp_[4], ss_[4])
```

```python
import functools
import math

import jax
import jax.numpy as jnp
from jax import lax
from jax.experimental import pallas as pl
from jax.experimental.pallas import tpu as pltpu

D_MODEL = 2048
BATCH = 8
SEQ = 2048
DEPTH = 2
DEC_BATCH = 128
DEC_SEQ = 8
W_GROUP = D_MODEL // 4
W_A = W_B = W_C = W_D = W_GROUP
H_A = 4
DH_A = W_A // (2 * H_A)
DV_A = 2 * DH_A
Q_BLOCK = 128
CONV_B = 3
H_C = 4
DK_C = W_C // H_C
DV_C = W_C // H_C
CONV_C = 4
DN_CHUNK = 64
H_D = 4
BW_D = W_D // H_D
CONV_D = 4
RG_C = 8.0
D_FF = 5632
N_EXPERTS = 8
TOP_K = 2
D_FF_EXPERT = 7168
N_MOD = 6
EPS = 1e-6
N_A = 3 * W_A
N_B = 3 * W_B
N_C = 4 * W_C + 2 * H_C
N_D = 2 * W_D

N_PROMPT = BATCH * SEQ
N_SAMPLE = DEC_BATCH * DEC_SEQ
N_TOK = N_PROMPT + N_SAMPLE
SUBLANES = 8
LANES = 128

OFF_A = 0
OFF_B = N_A
OFF_C = N_A + N_B
OFF_D = OFF_C + 4 * W_C
OFF_AB = OFF_D + N_D
N_IN_PAD = OFF_AB + LANES

VMEM_LIMIT = 48 * 1024 * 1024
MOE_TM = 512
MOE_TN = 512
MOE_TK = 1024
MOE_CAST_ROWS = 112
MOE_TILES = (TOP_K * N_TOK) // MOE_TM + N_EXPERTS
MOE_ROWS = MOE_TILES * MOE_TM
ROW_TILE = 512
ATT_TQ = 512
ATT_HPB = 2
LRU_CHUNK = 256
LRU_SPS = 16
DN_SPS = 8
DN_SUB_PROMPT = 4

f32 = jnp.float32
bf16 = jnp.bfloat16


def _params(sem):
    return pltpu.CompilerParams(dimension_semantics=sem, vmem_limit_bytes=VMEM_LIMIT)


def _silu(x):
    return x * (1.0 / (1.0 + jnp.exp(-x)))


def _mod_kernel(c_ref, w_ref, b_ref, o_ref):
    c = _silu(c_ref[...]).astype(bf16)
    o_ref[...] = jnp.dot(c, w_ref[0].astype(bf16), preferred_element_type=f32) + b_ref[0]


def modulation(c_all, w, b, layer):
    r = c_all.shape[0]
    n = w.shape[2]
    tn = 1024
    return pl.pallas_call(
        _mod_kernel,
        grid=(n // tn,),
        in_specs=[pl.BlockSpec((r, D_MODEL), lambda j: (0, 0)),
                  pl.BlockSpec((1, D_MODEL, tn), lambda j: (layer, 0, j)),
                  pl.BlockSpec((1, 1, tn), lambda j: (layer, 0, j))],
        out_specs=pl.BlockSpec((r, tn), lambda j: (0, j)),
        out_shape=jax.ShapeDtypeStruct((r, n), f32),
        compiler_params=_params(("arbitrary",)),
        name="modulation",
    )(c_all, w, b)


def _mod_specs(idx, tile_rows, col_block):
    tiles_per_req = SEQ // tile_rows
    npt = N_PROMPT // tile_rows

    def p_map(*ids):
        return (idx, jnp.minimum(ids[0] // tiles_per_req, BATCH - 1), 0, col_block(*ids))

    def s_map(*ids):
        return (idx, jnp.maximum(ids[0] - npt, 0), col_block(*ids))

    return p_map, s_map


def _row_pair(x, tile_rows, cols, col_block):
    npt = N_PROMPT // tile_rows
    if isinstance(x, tuple):
        xp, xs = x
        s_map = lambda *ids: (jnp.maximum(ids[0] - npt, 0), col_block(*ids))
    else:
        xp = xs = x
        s_map = lambda *ids: (jnp.maximum(ids[0], npt), col_block(*ids))
    p_map = lambda *ids: (jnp.minimum(ids[0], npt - 1), col_block(*ids))
    return (xp, xs), [pl.BlockSpec((tile_rows, cols), p_map), pl.BlockSpec((tile_rows, cols), s_map)]


def _apply_rows(vals, i, tile_rows, fn_prompt, fn_sample):
    npt = N_PROMPT // tile_rows

    @pl.when(i < npt)
    def _():
        fn_prompt(*vals)

    @pl.when(i >= npt)
    def _():
        fn_sample(*vals)


def _bcast_groups(a, m):
    rows, c = a.shape
    return a.reshape(rows // SUBLANES, SUBLANES, c), m[:, None, :]


def _norm_mod_emit(xp_ref, xs_ref, nw_ref, scp_ref, shp_ref, scs_ref, shs_ref, emit):
    tr = xp_ref.shape[0]

    def normed(x):
        return x * lax.rsqrt(jnp.mean(x * x, axis=-1, keepdims=True) + EPS) * nw_ref[...]

    def prompt():
        emit(normed(xp_ref[...]) * (1.0 + scp_ref[0, 0]) + shp_ref[0, 0])

    def sample():
        y3, sc = _bcast_groups(normed(xs_ref[...]), scs_ref[0])
        emit((y3 * (1.0 + sc) + shs_ref[0][:, None, :]).reshape(tr, D_MODEL))

    _apply_rows((), pl.program_id(0), tr, prompt, sample)


def _norm_mod_kernel(xp_ref, xs_ref, nw_ref, scp_ref, shp_ref, scs_ref, shs_ref, o_ref):
    def emit(hn):
        o_ref[...] = hn.astype(o_ref.dtype)

    _norm_mod_emit(xp_ref, xs_ref, nw_ref, scp_ref, shp_ref, scs_ref, shs_ref, emit)


def _norm_mod_router_kernel(xp_ref, xs_ref, nw_ref, scp_ref, shp_ref, scs_ref, shs_ref, r_ref, o_ref, lg_ref):
    def emit(hn):
        bits = lax.bitcast_convert_type(hn.astype(bf16).astype(f32), jnp.uint32)
        half = D_MODEL // 2
        o_ref[...] = (bits[:, :half] >> 16) | bits[:, half:]
        lg_ref[...] = jnp.dot(hn, r_ref[...], preferred_element_type=f32, precision=lax.Precision.HIGHEST)

    _norm_mod_emit(xp_ref, xs_ref, nw_ref, scp_ref, shp_ref, scs_ref, shs_ref, emit)


def norm_mod(x, nw, mod_p, mod_s, i_scale, i_shift, router=None):
    tr = ROW_TILE
    tg = tr // SUBLANES
    zero = lambda i: 0
    scp_map, scs_map = _mod_specs(i_scale, tr, zero)
    shp_map, shs_map = _mod_specs(i_shift, tr, zero)
    x_ops, x_specs = _row_pair(x, tr, D_MODEL, zero)
    in_specs = x_specs + [
                pl.BlockSpec((1, D_MODEL), lambda i: (0, 0)),
                pl.BlockSpec((1, 1, 1, D_MODEL), scp_map), pl.BlockSpec((1, 1, 1, D_MODEL), shp_map),
                pl.BlockSpec((1, tg, D_MODEL), scs_map), pl.BlockSpec((1, tg, D_MODEL), shs_map)]
    out_spec = pl.BlockSpec((tr, D_MODEL), lambda i: (i, 0))
    out_shape = jax.ShapeDtypeStruct((N_TOK, D_MODEL), bf16)
    if router is not None:
        out_spec = pl.BlockSpec((tr, D_MODEL // 2), lambda i: (i, 0))
        out_shape = jax.ShapeDtypeStruct((N_TOK, D_MODEL // 2), jnp.uint32)
    if router is None:
        return pl.pallas_call(
            _norm_mod_kernel, grid=(N_TOK // tr,), in_specs=in_specs, out_specs=out_spec,
            out_shape=out_shape, compiler_params=_params(("arbitrary",)), name="norm_mod",
        )(*x_ops, nw, mod_p, mod_p, mod_s, mod_s)
    return pl.pallas_call(
        _norm_mod_router_kernel, grid=(N_TOK // tr,),
        in_specs=in_specs + [pl.BlockSpec((D_MODEL, LANES), lambda i: (0, 0))],
        out_specs=[out_spec, pl.BlockSpec((tr, LANES), lambda i: (i, 0))],
        out_shape=[out_shape, jax.ShapeDtypeStruct((N_TOK, LANES), f32)],
        compiler_params=_params(("arbitrary",)), name="norm_mod_router",
    )(*x_ops, nw, mod_p, mod_p, mod_s, mod_s, router)


def _final_norm_kernel(x_ref, nw_ref, op_ref, os_ref, *, n_prompt_tiles):
    x = x_ref[...]
    y = x * lax.rsqrt(jnp.mean(x * x, axis=-1, keepdims=True) + EPS) * nw_ref[...]
    i = pl.program_id(0)

    @pl.when(i < n_prompt_tiles)
    def _():
        op_ref[...] = y

    @pl.when(i >= n_prompt_tiles)
    def _():
        os_ref[...] = y


def final_norm(x, nw):
    tr = ROW_TILE
    npt = N_PROMPT // tr
    return pl.pallas_call(
        functools.partial(_final_norm_kernel, n_prompt_tiles=npt), grid=(N_TOK // tr,),
        in_specs=[pl.BlockSpec((tr, D_MODEL), lambda i: (i, 0)),
                  pl.BlockSpec((1, D_MODEL), lambda i: (0, 0))],
        out_specs=[pl.BlockSpec((tr, D_MODEL), lambda i: (jnp.minimum(i, npt - 1), 0)),
                   pl.BlockSpec((tr, D_MODEL), lambda i: (jnp.maximum(i - npt, 0), 0))],
        out_shape=[jax.ShapeDtypeStruct((N_PROMPT, D_MODEL), f32), jax.ShapeDtypeStruct((N_SAMPLE, D_MODEL), f32)],
        compiler_params=_params(("arbitrary",)), name="final_norm",
    )(x, nw)


def _mm_kernel(x_ref, w_ref, o_ref):
    o_ref[...] = jnp.dot(x_ref[...], w_ref[...], preferred_element_type=f32).astype(o_ref.dtype)


def matmul(x, w, tm, tn, out_dtype):
    m, k = x.shape
    n = w.shape[1]
    return pl.pallas_call(
        _mm_kernel, grid=(m // tm, n // tn),
        in_specs=[pl.BlockSpec((tm, k), lambda i, j: (i, 0)),
                  pl.BlockSpec((k, tn), lambda i, j: (0, j))],
        out_specs=pl.BlockSpec((tm, tn), lambda i, j: (i, j)),
        out_shape=jax.ShapeDtypeStruct((m, n), out_dtype),
        compiler_params=_params(("arbitrary", "arbitrary")), name="matmul",
    )(x, w)


def _gated_residual_store(acc, xp_ref, xs_ref, gp_ref, gs_ref, o_ref):
    tm, tn = acc.shape

    def prompt(acc):
        o_ref[...] = xp_ref[...] + acc * gp_ref[0, 0]

    def sample(acc):
        a3, g = _bcast_groups(acc, gs_ref[0])
        o_ref[...] = xs_ref[...] + (a3 * g).reshape(tm, tn)

    _apply_rows((acc,), pl.program_id(0), tm, prompt, sample)


def _mm_resid_kernel(y_ref, w_ref, xp_ref, xs_ref, gp_ref, gs_ref, o_ref, acc_ref, *, nk):
    k = pl.program_id(2)
    part = jnp.dot(y_ref[...], w_ref[...], preferred_element_type=f32)

    @pl.when(k == 0)
    def _():
        acc_ref[...] = part

    @pl.when(k > 0)
    def _():
        acc_ref[...] += part

    @pl.when(k == nk - 1)
    def _():
        _gated_residual_store(acc_ref[...], xp_ref, xs_ref, gp_ref, gs_ref, o_ref)


def matmul_gated_residual(y, w, x, mod_p, mod_s, i_gate, tm, tn, tk):
    m, kdim = y.shape
    n = w.shape[1]
    nk = kdim // tk
    gp_map, gs_map = _mod_specs(i_gate, tm, lambda i, j, k: j)
    x_ops, x_specs = _row_pair(x, tm, tn, lambda i, j, k: j)
    return pl.pallas_call(
        functools.partial(_mm_resid_kernel, nk=nk), grid=(m // tm, n // tn, nk),
        in_specs=[pl.BlockSpec((tm, tk), lambda i, j, k: (i, k)),
                  pl.BlockSpec((tk, tn), lambda i, j, k: (k, j)),
                  *x_specs,
                  pl.BlockSpec((1, 1, 1, tn), gp_map),
                  pl.BlockSpec((1, tm // SUBLANES, tn), gs_map)],
        out_specs=pl.BlockSpec((tm, tn), lambda i, j, k: (i, j)),
        out_shape=jax.ShapeDtypeStruct((m, n), f32),
        scratch_shapes=[pltpu.VMEM((tm, tn), f32)],
        compiler_params=_params(("arbitrary", "arbitrary", "arbitrary")), name="matmul_gated_residual",
    )(y, w, *x_ops, mod_p, mod_s)


def _mix_out_kernel(ya_ref, yb_ref, yc_ref, yd_ref, w_ref, xp_ref, xs_ref, gp_ref, gs_ref, o_ref):
    acc = jnp.dot(ya_ref[...], w_ref[0:W_GROUP, :], preferred_element_type=f32)
    for gi, y_ref in enumerate((yb_ref, yc_ref, yd_ref), start=1):
        acc = acc + jnp.dot(y_ref[...], w_ref[gi * W_GROUP:(gi + 1) * W_GROUP, :], preferred_element_type=f32)
    _gated_residual_store(acc, xp_ref, xs_ref, gp_ref, gs_ref, o_ref)


def mix_out_residual(ys, w, x, mod_p, mod_s, i_gate, tm, tn):
    m, n = N_TOK, D_MODEL
    y_spec = pl.BlockSpec((tm, W_GROUP), lambda i, j: (i, 0))
    gp_map, gs_map = _mod_specs(i_gate, tm, lambda i, j: j)
    x_ops, x_specs = _row_pair(x, tm, tn, lambda i, j: j)
    return pl.pallas_call(
        _mix_out_kernel, grid=(m // tm, n // tn),
        in_specs=[y_spec, y_spec, y_spec, y_spec,
                  pl.BlockSpec((D_MODEL, tn), lambda i, j: (0, j)),
                  *x_specs,
                  pl.BlockSpec((1, 1, 1, tn), gp_map),
                  pl.BlockSpec((1, tm // SUBLANES, tn), gs_map)],
        out_specs=pl.BlockSpec((tm, tn), lambda i, j: (i, j)),
        out_shape=jax.ShapeDtypeStruct((m, n), f32),
        compiler_params=_params(("arbitrary", "arbitrary")), name="mix_out_residual",
    )(*ys, w, *x_ops, mod_p, mod_s)


def _swiglu_body(x, wg, wu):
    g = jnp.dot(x, wg, preferred_element_type=f32)
    u = jnp.dot(x, wu, preferred_element_type=f32)
    return _silu(g) * u


def _swiglu_kernel(x_ref, wg_ref, wu_ref, o_ref):
    o_ref[...] = _swiglu_body(x_ref[...], wg_ref[...], wu_ref[...]).astype(o_ref.dtype)


def swiglu_up(x, wg, wu, tm, tn):
    m, k = x.shape
    n = wg.shape[1]
    return pl.pallas_call(
        _swiglu_kernel, grid=(m // tm, n // tn),
        in_specs=[pl.BlockSpec((tm, k), lambda i, j: (i, 0)),
                  pl.BlockSpec((k, tn), lambda i, j: (0, j)),
                  pl.BlockSpec((k, tn), lambda i, j: (0, j))],
        out_specs=pl.BlockSpec((tm, tn), lambda i, j: (i, j)),
        out_shape=jax.ShapeDtypeStruct((m, n), bf16),
        compiler_params=_params(("arbitrary", "arbitrary")), name="swiglu_up",
    )(x, wg, wu)


def _unpack_tokens(words):
    lo = lax.bitcast_convert_type(words << 16, f32)
    hi = lax.bitcast_convert_type(words & jnp.uint32(0xFFFF0000), f32)
    return jnp.concatenate([lo, hi], axis=1).astype(bf16)


def _moe_up_kernel(m_ref, n_ref, e_ref, v_ref, x_ref, wg_ref, wu_ref, wd_ref, o_ref, wdb_ref, wg_sc, wu_sc, *,
                   n_cast):
    i = pl.program_id(0)

    @pl.when(i < n_cast)
    def _():
        wdb_ref[...] = wd_ref[...].astype(bf16)

    @pl.when(v_ref[i] == 2)
    def _():
        wg_sc[...] = wg_ref[0].astype(bf16)
        wu_sc[...] = wu_ref[0].astype(bf16)

    @pl.when(v_ref[i] > 0)
    def _():
        o_ref[...] = _swiglu_body(_unpack_tokens(x_ref[...]), wg_sc[...], wu_sc[...]).astype(o_ref.dtype)


def moe_up(item_m, item_n, item_e, item_v, xs, wg, wu, wd):
    ff = wg.shape[2]
    nt = ff // MOE_TN
    n_items = MOE_TILES * nt
    wd_rows = wd.shape[0] * wd.shape[1]
    n_cast = wd_rows // MOE_CAST_ROWS
    assert n_cast <= (TOP_K * N_TOK // MOE_TM) * nt and wd_rows % MOE_CAST_ROWS == 0
    cast_spec = pl.BlockSpec((MOE_CAST_ROWS, D_MODEL), lambda i, m, n, e, v: (jnp.minimum(i, n_cast - 1), 0))
    grid_spec = pltpu.PrefetchScalarGridSpec(
        num_scalar_prefetch=4, grid=(n_items,),
        in_specs=[pl.BlockSpec((MOE_TM, D_MODEL // 2), lambda i, m, n, e, v: (m[i], 0)),
                  pl.BlockSpec((1, D_MODEL, MOE_TN), lambda i, m, n, e, v: (e[i], 0, n[i])),
                  pl.BlockSpec((1, D_MODEL, MOE_TN), lambda i, m, n, e, v: (e[i], 0, n[i])),
                  cast_spec],
        out_specs=[pl.BlockSpec((MOE_TM, MOE_TN), lambda i, m, n, e, v: (m[i], n[i])), cast_spec],
        scratch_shapes=[pltpu.VMEM((D_MODEL, MOE_TN), bf16), pltpu.VMEM((D_MODEL, MOE_TN), bf16)])
    h, wd_bf = pl.pallas_call(
        functools.partial(_moe_up_kernel, n_cast=n_cast), grid_spec=grid_spec,
        out_shape=[jax.ShapeDtypeStruct((MOE_ROWS, ff), bf16), jax.ShapeDtypeStruct((wd_rows, D_MODEL), bf16)],
        compiler_params=_params(("arbitrary",)), name="moe_up",
    )(item_m, item_n, item_e, item_v, xs, wg, wu, wd.reshape(wd_rows, D_MODEL))
    return h, wd_bf.reshape(wd.shape)


def _moe_down_kernel(te_ref, nu_ref, h_ref, wd_ref, o_ref):
    m = pl.program_id(0)
    k = pl.program_id(1)

    @pl.when(m < nu_ref[0])
    def _():
        part = jnp.dot(h_ref[...], wd_ref[0], preferred_element_type=f32)

        @pl.when(k == 0)
        def _():
            o_ref[...] = part

        @pl.when(k > 0)
        def _():
            o_ref[...] += part


def moe_down(tile_e, n_used, h, wd):
    nk = wd.shape[1] // MOE_TK

    def mclamp(m, nu):
        return jnp.minimum(m, nu[0] - 1)

    def kclamp(m, k, nu):
        return jnp.where(m < nu[0], k, nk - 1)

    grid_spec = pltpu.PrefetchScalarGridSpec(
        num_scalar_prefetch=2, grid=(MOE_TILES, nk),
        in_specs=[pl.BlockSpec((MOE_TM, MOE_TK), lambda m, k, te, nu: (mclamp(m, nu), kclamp(m, k, nu))),
                  pl.BlockSpec((1, MOE_TK, D_MODEL),
                               lambda m, k, te, nu: (te[mclamp(m, nu)], kclamp(m, k, nu), 0))],
        out_specs=pl.BlockSpec((MOE_TM, D_MODEL), lambda m, k, te, nu: (mclamp(m, nu), 0)))
    return pl.pallas_call(
        _moe_down_kernel, grid_spec=grid_spec,
        out_shape=jax.ShapeDtypeStruct((MOE_ROWS, D_MODEL), f32),
        compiler_params=_params(("arbitrary", "arbitrary")), name="moe_down",
    )(tile_e, n_used, h, wd)


def moe_route(logits, n_ff_tiles):
    top_v, top_i = lax.top_k(logits, TOP_K)
    gates = jax.nn.softmax(top_v, axis=-1)
    e_flat = top_i.reshape(-1).astype(jnp.int32)
    onehot = (e_flat[:, None] == jnp.arange(N_EXPERTS, dtype=jnp.int32)[None, :]).astype(jnp.int32)
    csum = jnp.cumsum(onehot, axis=0)
    counts = csum[-1]
    rank = jnp.sum((csum - onehot) * onehot, axis=-1)
    tiles_e = (counts + MOE_TM - 1) // MOE_TM
    tile_end = jnp.cumsum(tiles_e)
    tile_start = tile_end - tiles_e
    dest = tile_start[e_flat] * MOE_TM + rank
    tok = jnp.arange(TOP_K * N_TOK, dtype=jnp.int32) // TOP_K
    row_src = jnp.zeros((MOE_ROWS,), jnp.int32).at[dest].set(tok)
    n_used = tile_end[-1]
    t_ids = jnp.arange(MOE_TILES, dtype=jnp.int32)
    tile_e = jnp.minimum(jnp.searchsorted(tile_end, t_ids, side="right"), N_EXPERTS - 1).astype(jnp.int32)
    nt = n_ff_tiles
    items_e = tiles_e * nt
    item_end = jnp.cumsum(items_e)
    item_start = item_end - items_e
    total = item_end[-1]
    idx = jnp.arange(MOE_TILES * nt, dtype=jnp.int32)
    valid = idx < total
    idc = jnp.minimum(idx, total - 1)
    ie = jnp.minimum(jnp.searchsorted(item_end, idc, side="right"), N_EXPERTS - 1).astype(jnp.int32)
    local = idc - item_start[ie]
    te = jnp.maximum(tiles_e[ie], 1)
    item_n = (local // te).astype(jnp.int32)
    item_m = (tile_start[ie] + local % te).astype(jnp.int32)
    return dict(gates=gates, dest=dest.reshape(N_TOK, TOP_K), row_src=row_src, tile_e=tile_e,
                n_used=n_used.reshape(1).astype(jnp.int32), item_m=item_m, item_n=item_n, item_e=ie,
                item_v=jnp.where(valid, jnp.where(local % te == 0, 2, 1), 0).astype(jnp.int32))


def _smem_spec():
    return pl.BlockSpec(memory_space=pltpu.SMEM)


def _subln(o, w, gain):
    return o * lax.rsqrt(jnp.mean(o * o, axis=-1, keepdims=True) + EPS) * w * gain


def _attn_prompt_kernel(lam_ref, slope_ref, q_ref, k_ref, v_ref, w_ref, o_ref, m_sc, acc_sc, *, tq, hpb, gain):
    hb = pl.program_id(1)
    qi = pl.program_id(2)
    lane = lax.broadcasted_iota(jnp.int32, (tq, DV_A), 1)
    row = lax.broadcasted_iota(jnp.int32, (2 * tq, tq), 0)
    col = lax.broadcasted_iota(jnp.int32, (2 * tq, tq), 1)
    rel = (col - jnp.where(row >= tq, row - tq, row)).astype(f32)
    ones_v = jnp.ones((tq, DV_A), bf16)
    qss, slopes, biases = [], [], []
    for hh in range(hpb):
        q = q_ref[:, hh * DV_A:(hh + 1) * DV_A] * (DH_A ** -0.5)
        qss.append(jnp.concatenate([jnp.where(lane < DH_A, q, 0.0), jnp.where(lane >= DH_A, q, 0.0)],
                                   axis=0).astype(bf16))
        slopes.append(slope_ref[hb * hpb + hh])
        biases.append(slopes[hh] * rel)
    m_sc[...] = jnp.full(m_sc.shape, -jnp.inf, f32)
    acc_sc[...] = jnp.zeros(acc_sc.shape, f32)

    def chunk_step(j, diagonal):
        off = pl.multiple_of(j * tq, tq)
        blocks = (jnp.zeros((1, 1), jnp.int32) + (qi - j) * tq).astype(f32)
        for hh in range(hpb):
            cs = slice(hh * DV_A, (hh + 1) * DV_A)
            kc = k_ref[pl.ds(off, tq), cs].astype(bf16)
            vc = jnp.concatenate([v_ref[pl.ds(off, tq), cs].astype(bf16), ones_v], axis=1)
            s = lax.dot_general(qss[hh], kc, (((1,), (1,)), ((), ())), preferred_element_type=f32)
            s = s + biases[hh] - slopes[hh] * blocks
            if diagonal:
                s = jnp.where(rel <= 0, s, -1e30)
            m_prev = m_sc[hh]
            m_new = jnp.maximum(m_prev, jnp.max(s, axis=-1, keepdims=True))
            pr = jnp.exp(s - m_new).astype(bf16)
            acc_sc[hh] = jnp.exp(m_prev - m_new) * acc_sc[hh] + jnp.dot(pr, vc, preferred_element_type=f32)
            m_sc[hh] = m_new

    def body(j, carry):
        chunk_step(j, False)
        return carry

    lax.fori_loop(0, qi, body, 0)
    chunk_step(qi, True)
    for hh in range(hpb):
        acc = acc_sc[hh]
        o = acc[:, :DV_A] / acc[:, DV_A:]
        o = o[:tq] - lam_ref[0] * o[tq:]
        o_ref[:, hh * DV_A:(hh + 1) * DV_A] = _subln(o, w_ref[...], gain).astype(o_ref.dtype)


def attn_prompt(proj, lam, slopes, subln_w, gain, batch, seq, tq, hpb):
    nq = seq // tq
    wb = hpb * DV_A
    nhb = H_A // hpb
    return pl.pallas_call(
        functools.partial(_attn_prompt_kernel, tq=tq, hpb=hpb, gain=gain),
        grid=(batch, nhb, nq),
        in_specs=[_smem_spec(), _smem_spec(),
                  pl.BlockSpec((tq, wb), lambda b, h, qi: (b * nq + qi, h)),
                  pl.BlockSpec((seq, wb), lambda b, h, qi: (b, nhb + h)),
                  pl.BlockSpec((seq, wb), lambda b, h, qi: (b, 2 * nhb + h)),
                  pl.BlockSpec((1, DV_A), lambda b, h, qi: (0, 0))],
        out_specs=pl.BlockSpec((tq, wb), lambda b, h, qi: (b * nq + qi, h)),
        out_shape=jax.ShapeDtypeStruct((N_TOK, W_A), bf16),
        scratch_shapes=[pltpu.VMEM((hpb, 2 * tq, 1), f32), pltpu.VMEM((hpb, 2 * tq, 2 * DV_A), f32)],
        compiler_params=_params(("arbitrary", "arbitrary", "arbitrary")), name="attn_prompt",
    )(lam, slopes, proj, proj, proj, subln_w)


def _attn_sample_kernel(pt_ref, lam_ref, slope_ref, q_ref, kn_ref, vn_ref, w_ref, *rest, t, n_pages, page, gain):
    k_refs = rest[:n_pages]
    v_refs = rest[n_pages:2 * n_pages]
    o_ref = rest[2 * n_pages + 1]
    s_sc = rest[2 * n_pages + 2]
    n_rep = 2 * H_A
    rows = n_rep * t
    n_past = n_pages * page
    q = q_ref[...] * (DH_A ** -0.5)
    q_rep = jnp.concatenate([q] * n_rep, axis=0)
    r_grp = lax.broadcasted_iota(jnp.int32, (rows, W_A), 0) // t
    c_grp = lax.broadcasted_iota(jnp.int32, (rows, W_A), 1) // DH_A
    qbd = jnp.where(r_grp == c_grp, q_rep, 0.0).astype(bf16)
    for j in range(n_pages):
        s_sc[:, j * page:(j + 1) * page] = jnp.dot(qbd, k_refs[j][0, 0].astype(bf16), preferred_element_type=f32)
    s_new = lax.dot_general(qbd, kn_ref[...].astype(bf16), (((1,), (1,)), ((), ())), preferred_element_type=f32)

    row1 = lax.broadcasted_iota(jnp.int32, (rows, 1), 0)
    head1 = row1 // (2 * t)
    slope = jnp.where(head1 == 0, slope_ref[0],
                      jnp.where(head1 == 1, slope_ref[1], jnp.where(head1 == 2, slope_ref[2], slope_ref[3])))
    t_row = (row1 % t)
    kpos = lax.broadcasted_iota(jnp.int32, (rows, n_past), 1)
    s_past = s_sc[...] - slope * (n_past + t_row - kpos).astype(f32)
    dist_new = (t_row - lax.broadcasted_iota(jnp.int32, (rows, t), 1)).astype(f32)
    s_new = jnp.where(dist_new >= 0, s_new - slope * dist_new, -1e30)
    m = jnp.maximum(jnp.max(s_past, axis=-1, keepdims=True), jnp.max(s_new, axis=-1, keepdims=True))
    p_past = jnp.exp(s_past - m)
    p_new = jnp.exp(s_new - m)
    denom = jnp.sum(p_past, axis=-1, keepdims=True) + jnp.sum(p_new, axis=-1, keepdims=True)
    p_past = p_past.astype(bf16)
    p_new = p_new.astype(bf16)
    lam = lam_ref[0]
    w = w_ref[...]
    for h in range(H_A):
        vh = jnp.concatenate([v_refs[j][0, 0, pl.ds(h, page, stride=H_A), :] for j in range(n_pages)], axis=0)
        rs = slice(h * 2 * t, (h + 1) * 2 * t)
        oh = jnp.dot(p_past[rs], vh.astype(bf16), preferred_element_type=f32)
        oh = oh + jnp.dot(p_new[rs], vn_ref[:, h * DV_A:(h + 1) * DV_A].astype(bf16), preferred_element_type=f32)
        oh = oh / denom[rs]
        o = oh[:t] - lam * oh[t:]
        o_ref[:, h * DV_A:(h + 1) * DV_A] = _subln(o, w, gain).astype(o_ref.dtype)


def attn_sample(proj, ckt, cvr, layer, page_table, lam, slopes, subln_w, gain, row0, nseq, t, y_rows):
    n_pages = page_table.shape[1]
    page = ckt.shape[-1]
    rb0 = row0 // t

    def kv_spec(j):
        return pl.BlockSpec((1, 1, W_A, page), lambda b, pt, j=j: (layer, pt[b * n_pages + j], 0, 0))

    in_specs = ([_smem_spec(), _smem_spec(),
                 pl.BlockSpec((t, W_A), lambda b, pt: (rb0 + b, 0)),
                 pl.BlockSpec((t, W_A), lambda b, pt: (rb0 + b, 1)),
                 pl.BlockSpec((t, W_A), lambda b, pt: (rb0 + b, 2)),
                 pl.BlockSpec((1, DV_A), lambda b, pt: (0, 0))]
                + [kv_spec(j) for j in range(n_pages)] + [kv_spec(j) for j in range(n_pages)]
                + [pl.BlockSpec(memory_space=pl.ANY)])
    grid_spec = pltpu.PrefetchScalarGridSpec(
        num_scalar_prefetch=1, grid=(nseq,), in_specs=in_specs,
        out_specs=pl.BlockSpec((t, W_A), lambda b, pt: (rb0 + b, 0)),
        scratch_shapes=[pltpu.VMEM((2 * H_A * t, n_pages * page), f32)])
    args = (page_table.reshape(-1), lam, slopes, proj, proj, proj, subln_w, *([ckt] * n_pages), *([cvr] * n_pages),
            y_rows)
    return pl.pallas_call(
        functools.partial(_attn_sample_kernel, t=t, n_pages=n_pages, page=page, gain=gain),
        grid_spec=grid_spec, out_shape=jax.ShapeDtypeStruct(y_rows.shape, y_rows.dtype),
        input_output_aliases={len(args) - 1: 0},
        compiler_params=_params(("arbitrary",)), name="attn_sample",
    )(*args)


def _conv_step(f_sc, x3, w_ref, buf0_ref, buf_out, width, chunk, first):
    lo = SUBLANES - (width - 1)

    @pl.when(first)
    def _():
        f_sc[:, 0:SUBLANES, :] = jnp.zeros((f_sc.shape[0], SUBLANES, f_sc.shape[2]), f32)
        f_sc[:, lo:SUBLANES, :] = buf0_ref[...]

    f_sc[:, SUBLANES:SUBLANES + chunk, :] = x3
    out = w_ref[0:1, :][None] * f_sc[:, lo:lo + chunk, :]
    for j in range(1, width):
        out = out + w_ref[j:j + 1, :][None] * f_sc[:, lo + j:lo + j + chunk, :]
    buf_out[...] = f_sc[:, chunk + lo:chunk + SUBLANES, :]
    f_sc[:, 0:SUBLANES, :] = f_sc[:, chunk:chunk + SUBLANES, :]
    return out


def _sigmoid(x):
    return 1.0 / (1.0 + jnp.exp(-x))


def _gelu_tanh(x):
    return 0.5 * x * (1.0 + jnp.tanh(0.7978845608028654 * (x + 0.044715 * (x * x * x))))


def _conv_lru_kernel(pb_ref, pd_ref, cb0_ref, lb0_ref, h0_ref, wb_ref, wd_ref, bd_ref, wa_ref, ba_ref, wx_ref,
                     bx_ref, sp_ref, *rest, sps, chunk):
    yb_ref, yd_ref, cb_out, lb_out, h_out, fb_sc, fd_sc, h_sc = rest[-8:]
    first = pl.program_id(1) == 0
    rows = sps * chunk
    pb = pb_ref[...]
    bg, cg, xt = pb[:, :W_B], pb[:, W_B:2 * W_B], pb[:, 2 * W_B:]
    conv_b = _conv_step(fb_sc, (cg * xt).reshape(sps, chunk, W_B), wb_ref, cb0_ref, cb_out, CONV_B, chunk, first)
    yb_ref[...] = (bg * conv_b.reshape(rows, W_B)).astype(yb_ref.dtype)

    pd = pd_ref[...]
    xd, gd = pd[:, :W_D], pd[:, W_D:]
    conv_d = _conv_step(fd_sc, xd.reshape(sps, chunk, W_D), wd_ref, lb0_ref, lb_out, CONV_D, chunk, first)
    xf = conv_d.reshape(rows, W_D) + bd_ref[...]
    xfb = xf.astype(bf16)
    ra = jnp.concatenate([jnp.dot(xfb[:, h * BW_D:(h + 1) * BW_D], wa_ref[h].astype(bf16),
                                  preferred_element_type=f32) for h in range(H_D)], axis=-1)
    rx = jnp.concatenate([jnp.dot(xfb[:, h * BW_D:(h + 1) * BW_D], wx_ref[h].astype(bf16),
                                  preferred_element_type=f32) for h in range(H_D)], axis=-1)
    r = _sigmoid(ra + ba_ref[...])
    ig = _sigmoid(rx + bx_ref[...])
    log_a = -RG_C * r * sp_ref[...]
    a = jnp.exp(log_a)
    th = jnp.tanh(log_a)
    bx = jnp.sqrt(-2.0 * th / (1.0 - th)) * (ig * xf)

    pos = lax.broadcasted_iota(jnp.int32, (rows, W_D), 0) % chunk
    step = 1
    while step < chunk:
        a_s = pltpu.roll(a, step, 0)
        b_s = pltpu.roll(bx, step, 0)
        live = pos >= step
        bx = jnp.where(live, a * b_s + bx, bx)
        a = jnp.where(live, a * a_s, a)
        step *= 2

    @pl.when(first)
    def _():
        h_sc[...] = h0_ref[...]

    h_in = jnp.broadcast_to(h_sc[...], (sps, chunk, W_D)).reshape(rows, W_D)
    hs = bx + a * h_in
    h_last = hs.reshape(sps, chunk, W_D)[:, chunk - 1:chunk, :]
    h_sc[...] = h_last
    h_out[...] = h_last
    yd_ref[...] = (hs * _gelu_tanh(gd)).astype(yd_ref.dtype)


def conv_lru(proj, row0, nseq, t, sps, chunk, conv_b_buf, lru_buf, lru_h, wb, wd, bd, wa, ba, wx, bx, sp, y_prev=()):
    rows = sps * chunk
    nc = t // chunk
    rb0 = row0 // rows

    def rmap(s, c):
        return rb0 + s * nc + c

    def full(shape):
        return pl.BlockSpec(shape, lambda s, c: (0,) * len(shape))

    def per_seq(shape):
        return pl.BlockSpec((sps,) + shape, lambda s, c: (s,) + (0,) * len(shape))

    in_specs = [pl.BlockSpec((rows, N_B), lambda s, c: (rmap(s, c), OFF_B // N_B)),
                pl.BlockSpec((rows, N_D), lambda s, c: (rmap(s, c), OFF_D // N_D)),
                per_seq((CONV_B - 1, W_B)), per_seq((CONV_D - 1, W_D)), per_seq((1, W_D)),
                full((CONV_B, W_B)), full((CONV_D, W_D)), full((1, W_D)),
                full((H_D, BW_D, BW_D)), full((1, W_D)), full((H_D, BW_D, BW_D)), full((1, W_D)), full((1, W_D))]
    in_specs = in_specs + [pl.BlockSpec(memory_space=pl.ANY)] * len(y_prev)
    out_specs = [pl.BlockSpec((rows, W_B), lambda s, c: (rmap(s, c), 0)),
                 pl.BlockSpec((rows, W_D), lambda s, c: (rmap(s, c), 0)),
                 per_seq((CONV_B - 1, W_B)), per_seq((CONV_D - 1, W_D)), per_seq((1, W_D))]
    out_shape = [jax.ShapeDtypeStruct((N_TOK, W_B), bf16), jax.ShapeDtypeStruct((N_TOK, W_D), bf16),
                 jax.ShapeDtypeStruct((nseq, CONV_B - 1, W_B), f32),
                 jax.ShapeDtypeStruct((nseq, CONV_D - 1, W_D), f32),
                 jax.ShapeDtypeStruct((nseq, 1, W_D), f32)]
    return pl.pallas_call(
        functools.partial(_conv_lru_kernel, sps=sps, chunk=chunk),
        grid=(nseq // sps, nc), in_specs=in_specs, out_specs=out_specs, out_shape=out_shape,
        scratch_shapes=[pltpu.VMEM((sps, chunk + SUBLANES, W_B), f32), pltpu.VMEM((sps, chunk + SUBLANES, W_D), f32),
                        pltpu.VMEM((sps, 1, W_D), f32)],
        input_output_aliases={13 + i: i for i in range(len(y_prev))},
        compiler_params=_params(("arbitrary", "arbitrary")), name="conv_lru",
    )(proj, proj, conv_b_buf, lru_buf, lru_h, wb, wd, bd, wa, ba, wx, bx, sp, *y_prev)


_HI = lax.Precision.HIGHEST


def _dot_nt(a, b, precision=None):
    return lax.dot_general(a, b, (((1,), (1,)), ((), ())), preferred_element_type=f32, precision=precision)


def _dot_tn(a, b):
    return lax.dot_general(a, b, (((0,), (0,)), ((), ())), preferred_element_type=f32)


def _hi_lo(x):
    hi = x.astype(bf16).astype(f32)
    return hi, x - hi


def _dot_f32(a, b, chunk):
    if chunk % 16:
        return jnp.dot(a, b, preferred_element_type=f32, precision=_HI)
    a_hi, a_lo = _hi_lo(a)
    b_hi, b_lo = _hi_lo(b)
    lhs = jnp.concatenate([a_hi, a_hi, a_lo], axis=1).astype(bf16)
    rhs = jnp.concatenate([b_hi, b_lo, b_hi], axis=0).astype(bf16)
    return jnp.dot(lhs, rhs, preferred_element_type=f32)


def _cumsum_rows(tri, x, chunk):
    if chunk % 16:
        return jnp.dot(tri, x, preferred_element_type=f32, precision=_HI)
    hi, r1 = _hi_lo(x)
    mid, lo = _hi_lo(r1)
    lhs = jnp.concatenate([tri, tri, tri], axis=1).astype(bf16)
    rhs = jnp.concatenate([hi, mid, lo], axis=0).astype(bf16)
    return jnp.dot(lhs, rhs, preferred_element_type=f32)


def _row_bcast(gc, pick, chunk):
    sel = pick.astype(f32)
    if chunk % 16:
        return _dot_nt(sel, gc, precision=_HI)
    hi, r1 = _hi_lo(gc)
    mid, lo = _hi_lo(r1)
    lhs = jnp.concatenate([sel, sel, sel], axis=1).astype(bf16)
    rhs = jnp.concatenate([hi, mid, lo], axis=1).astype(bf16)
    return _dot_nt(lhs, rhs)


def _deltanet_kernel(qkv_ref, z_ref, ab_ref, buf0_ref, s0_ref, wc_ref, nega_ref, dtb_ref, nw_ref,
                     *rest, sps, nsub, chunk):
    y_ref, buf_out, s_out, f_sc, s_sc = rest[-5:]
    first = pl.program_id(1) == 0
    span_rows = nsub * chunk
    rows = sps * span_rows
    wq = 3 * W_C
    conv = _conv_step(f_sc, qkv_ref[...].reshape(sps, span_rows, wq), wc_ref, buf0_ref, buf_out, CONV_C,
                      span_rows, first)
    conv = conv.reshape(rows, wq)
    qkv = conv * _sigmoid(conv)

    @pl.when(first)
    def _():
        s_sc[...] = s0_ref[...]

    ab = ab_ref[...]
    xg = ab + dtb_ref[...]
    g_all = nega_ref[...] * (jnp.maximum(xg, 0.0) + jnp.log1p(jnp.exp(-jnp.abs(xg))))
    beta_all = _sigmoid(ab)
    z = z_ref[...]
    ri = lax.broadcasted_iota(jnp.int32, (chunk, chunk), 0)
    ci = lax.broadcasted_iota(jnp.int32, (chunk, chunk), 1)
    causal = ri >= ci
    strict = ri > ci
    tri = causal.astype(f32)
    eye = (ri == ci).astype(f32)
    lane = lax.broadcasted_iota(jnp.int32, (chunk, LANES), 1)
    nw = nw_ref[...]
    spans = [(i, c) for i in range(sps) for c in range(nsub)]
    units = [(i, c, h) for i, c in spans for h in range(H_C)]

    def rows_of(i, c):
        r0 = (i * nsub + c) * chunk
        return slice(r0, r0 + chunk)

    gcs = {sp: _cumsum_rows(tri, g_all[rows_of(*sp)], chunk) for sp in spans}

    qs, ks, gcols, rhss, qks, negs = [], [], [], [], [], []
    for i, c, h in units:
        rs = rows_of(i, c)
        gc = gcs[(i, c)]
        q = qkv[rs, h * DK_C:(h + 1) * DK_C]
        k = qkv[rs, W_C + h * DK_C:W_C + (h + 1) * DK_C]
        v = qkv[rs, 2 * W_C + h * DV_C:2 * W_C + (h + 1) * DV_C]
        q = q * lax.rsqrt(jnp.sum(q * q, axis=-1, keepdims=True) + EPS) * (DK_C ** -0.5)
        k = k * lax.rsqrt(jnp.sum(k * k, axis=-1, keepdims=True) + EPS)
        beta = beta_all[rs, H_C + h:H_C + h + 1]
        gcol = gc[:, h:h + 1]
        grow = _row_bcast(gc, lane == h, chunk)
        decay = jnp.where(causal, jnp.exp(jnp.where(causal, gcol - grow, 0.0)), 0.0)
        eg = jnp.exp(gcol)
        kb = k * beta
        negs.append(jnp.where(strict, _dot_nt(kb, k) * decay, 0.0) * -1.0)
        qks.append(_dot_nt(q, k) * decay)
        rhss.append(jnp.concatenate([v * beta, kb * eg], axis=1))
        qs.append(q * eg)
        ks.append(k)
        gcols.append(gcol)

    invs = [eye + n for n in negs]
    span = 2
    while span < chunk:
        negs = [_dot_f32(n, n, chunk) for n in negs]
        invs = [inv + _dot_f32(inv, n, chunk) for inv, n in zip(invs, negs)]
        span *= 2
    sols = [_dot_f32(inv, rhs, chunk) for inv, rhs in zip(invs, rhss)]

    states = {(i, h): s_sc[i, h] for i in range(sps) for h in range(H_C)}
    outs = {}
    for c in range(nsub):
        cur = [(idx, (i, h)) for idx, (i, cc, h) in enumerate(units) if cc == c]
        wq_s = {idx: jnp.dot(jnp.concatenate([sols[idx][:, DV_C:], qs[idx]], axis=0), states[key],
                             preferred_element_type=f32) for idx, key in cur}
        v_new = {idx: sols[idx][:, :DV_C] - wq_s[idx][:chunk] for idx, _ in cur}
        for idx, _ in cur:
            outs[idx] = wq_s[idx][chunk:] + jnp.dot(qks[idx], v_new[idx], preferred_element_type=f32)
        for idx, key in cur:
            gcol = gcols[idx]
            g_last = gcol[chunk - 1:chunk, :]
            states[key] = (states[key] * jnp.exp(g_last)
                           + _dot_tn(ks[idx] * jnp.exp(g_last - gcol), v_new[idx]))
    for (i, h), st in states.items():
        s_sc[i, h] = st
        s_out[i, h] = st
    for idx, (i, c, h) in enumerate(units):
        rs = rows_of(i, c)
        zh = z[rs, h * DV_C:(h + 1) * DV_C]
        o = outs[idx]
        o = o * lax.rsqrt(jnp.mean(o * o, axis=-1, keepdims=True) + EPS) * nw
        y_ref[rs, h * DV_C:(h + 1) * DV_C] = (o * (zh * _sigmoid(zh))).astype(y_ref.dtype)


def deltanet(proj, row0, nseq, t, sps, nsub, chunk, dn_buf, dn_s, wc, nega, dtb, nw, y_prev=()):
    rows = sps * nsub * chunk
    nc = t // (nsub * chunk)
    rb0 = row0 // rows
    wq = 3 * W_C

    def rmap(s, c):
        return rb0 + s * nc + c

    def full(shape):
        return pl.BlockSpec(shape, lambda s, c: (0,) * len(shape))

    def per_seq(shape):
        return pl.BlockSpec((sps,) + shape, lambda s, c: (s,) + (0,) * len(shape))

    in_specs = [pl.BlockSpec((rows, wq), lambda s, c: (rmap(s, c), OFF_C // wq)),
                pl.BlockSpec((rows, W_C), lambda s, c: (rmap(s, c), (OFF_C + wq) // W_C)),
                pl.BlockSpec((rows, LANES), lambda s, c: (rmap(s, c), OFF_AB // LANES)),
                per_seq((CONV_C - 1, wq)), per_seq((H_C, DK_C, DV_C)),
                full((CONV_C, wq)), full((1, LANES)), full((1, LANES)), full((1, DV_C))]
    in_specs = in_specs + [pl.BlockSpec(memory_space=pl.ANY)] * len(y_prev)
    out_specs = [pl.BlockSpec((rows, W_C), lambda s, c: (rmap(s, c), 0)),
                 per_seq((CONV_C - 1, wq)), per_seq((H_C, DK_C, DV_C))]
    out_shape = [jax.ShapeDtypeStruct((N_TOK, W_C), bf16),
                 jax.ShapeDtypeStruct((nseq, CONV_C - 1, wq), f32),
                 jax.ShapeDtypeStruct((nseq, H_C, DK_C, DV_C), f32)]
    return pl.pallas_call(
        functools.partial(_deltanet_kernel, sps=sps, nsub=nsub, chunk=chunk),
        grid=(nseq // sps, nc), in_specs=in_specs, out_specs=out_specs, out_shape=out_shape,
        scratch_shapes=[pltpu.VMEM((sps, nsub * chunk + SUBLANES, wq), f32),
                        pltpu.VMEM((sps, H_C, DK_C, DV_C), f32)],
        input_output_aliases={9 + i: i for i in range(len(y_prev))},
        compiler_params=_params(("arbitrary", "arbitrary")), name="deltanet",
    )(proj, proj, proj, dn_buf, dn_s, wc, nega, dtb, nw, *y_prev)


def _lane_row(v):
    return jnp.concatenate([v.astype(f32), jnp.zeros((LANES - v.shape[0],), f32)])[None, :]


def kernel(x_prompt, x_sample, cache_k, cache_v, state_conv_b, state_dn_conv, state_dn, state_lru_conv, state_lru_h, page_table, c_prompt, c_sample, w_mod, b_mod, norm_mix, norm_ffn, w_in, w_out, lam_q1, lam_k1, lam_q2, lam_k2, subln_w, conv_b_w, dn_conv_w, dn_a_log, dn_dt_bias, dn_norm_w, lru_conv_w, lru_conv_b, lru_wa, lru_ba, lru_wx, lru_bx, lru_lambda, ffn_w_gate, ffn_w_up, ffn_w_down, moe_router, moe_w_gate, moe_w_up, moe_w_down, final_norm_w):
    x = (x_prompt.reshape(N_PROMPT, D_MODEL), x_sample.reshape(N_SAMPLE, D_MODEL))
    c_all = jnp.concatenate([c_prompt, c_sample], axis=0)
    n_pool, page = cache_k.shape[1], cache_k.shape[2]
    ckt = jnp.transpose(cache_k, (0, 1, 3, 4, 5, 2)).reshape(DEPTH, n_pool, W_A, page)
    cvr = cache_v.reshape(DEPTH, n_pool, page * H_A, DV_A)
    slopes = jnp.exp2(-8.0 * jnp.arange(1, H_A + 1, dtype=f32) / H_A)

    zero_cb = jnp.zeros((BATCH, CONV_B - 1, W_B), f32)
    zero_dc = jnp.zeros((BATCH, CONV_C - 1, 3 * W_C), f32)
    zero_ds = jnp.zeros((BATCH, H_C, DK_C, DV_C), f32)
    zero_lc = jnp.zeros((BATCH, CONV_D - 1, W_D), f32)
    zero_lh = jnp.zeros((BATCH, 1, W_D), f32)

    ks_p, vs_p, ks_s, vs_s = [], [], [], []
    st_p = [[] for _ in range(5)]
    st_s = [[] for _ in range(5)]
    for l in range(DEPTH):
        mod = modulation(c_all, w_mod, b_mod[:, None, :], l)
        mod_p = jnp.transpose(mod[:BATCH].reshape(BATCH, N_MOD, D_MODEL), (1, 0, 2))[:, :, None, :]
        mod_s = jnp.transpose(mod[BATCH:].reshape(DEC_BATCH, N_MOD, D_MODEL), (1, 0, 2))

        hn = norm_mod(x, norm_mix[l][None, :], mod_p, mod_s, 1, 0)
        wl = w_in[l]
        n_cqkvz = 4 * W_C
        w_pack = jnp.concatenate(
            [wl[:, :N_A + N_B + n_cqkvz], wl[:, N_A + N_B + N_C:], wl[:, N_A + N_B + n_cqkvz:N_A + N_B + N_C],
             jnp.zeros((D_MODEL, LANES - 2 * H_C), f32)], axis=1).astype(bf16)
        proj = matmul(hn, w_pack, tm=1088, tn=896, out_dtype=f32)

        lam_init = 0.8 - 0.6 * math.exp(-0.3 * l)
        lam = (jnp.exp(jnp.sum(lam_q1[l] * lam_k1[l])) - jnp.exp(jnp.sum(lam_q2[l] * lam_k2[l])) + lam_init)
        lam = lam.astype(f32).reshape(1)
        gain = 1.0 - lam_init
        sw = subln_w[l][None, :]
        ya = attn_prompt(proj, lam, slopes, sw, gain, BATCH, SEQ, ATT_TQ, ATT_HPB)
        ya = attn_sample(proj, ckt, cvr, l, page_table, lam, slopes, sw, gain, N_PROMPT, DEC_BATCH, DEC_SEQ, ya)

        sp = jax.nn.softplus(-lru_lambda[l])[None, :]
        lru_args = (conv_b_w[l], lru_conv_w[l], lru_conv_b[l][None, :], lru_wa[l], lru_ba[l][None, :],
                    lru_wx[l], lru_bx[l][None, :], sp)
        yb, yd, cb_p, lb_p, lh_p = conv_lru(proj, 0, BATCH, SEQ, 1, LRU_CHUNK, zero_cb, zero_lc, zero_lh, *lru_args)
        yb, yd, cb_s, lb_s, lh_s = conv_lru(proj, N_PROMPT, DEC_BATCH, DEC_SEQ, LRU_SPS, DEC_SEQ,
                                            state_conv_b[l], state_lru_conv[l], state_lru_h[l][:, None, :],
                                            *lru_args, y_prev=(yb, yd))

        dn_args = (dn_conv_w[l], _lane_row(-jnp.exp(dn_a_log[l])), _lane_row(dn_dt_bias[l]), dn_norm_w[l][None, :])
        yc, dc_p, ds_p = deltanet(proj, 0, BATCH, SEQ, 1, DN_SUB_PROMPT, DN_CHUNK, zero_dc, zero_ds, *dn_args)
        yc, dc_s, ds_s = deltanet(proj, N_PROMPT, DEC_BATCH, DEC_SEQ, DN_SPS, 1, math.gcd(DEC_SEQ, DN_CHUNK),
                                  state_dn_conv[l], state_dn[l], *dn_args, y_prev=(yc,))

        k_new = proj[:, W_A:2 * W_A]
        v_new = proj[:, 2 * W_A:3 * W_A]
        ks_p.append(k_new[:N_PROMPT].reshape(BATCH, SEQ, H_A, 2, DH_A))
        vs_p.append(v_new[:N_PROMPT].reshape(BATCH, SEQ, H_A, DV_A))
        ks_s.append(k_new[N_PROMPT:].reshape(DEC_BATCH, DEC_SEQ, H_A, 2, DH_A))
        vs_s.append(v_new[N_PROMPT:].reshape(DEC_BATCH, DEC_SEQ, H_A, DV_A))
        for lst, vals in ((st_p, (cb_p, dc_p, ds_p, lb_p, lh_p[:, 0, :])), (st_s, (cb_s, dc_s, ds_s, lb_s, lh_s[:, 0, :]))):
            for i, val in enumerate(vals):
                lst[i].append(val)

        x = mix_out_residual((ya, yb, yc, yd), w_out[l].astype(bf16), x, mod_p, mod_s, 2, tm=1024, tn=1024)

        j = l // 2
        if l % 2 == 0:
            hn = norm_mod(x, norm_ffn[l][None, :], mod_p, mod_s, 4, 3)
            h = swiglu_up(hn, ffn_w_gate[j].astype(bf16), ffn_w_up[j].astype(bf16), tm=1088, tn=512)
            x = matmul_gated_residual(h, ffn_w_down[j].astype(bf16), x, mod_p, mod_s, 5, tm=1024, tn=512,
                                      tk=D_FF // 2)
        else:
            router = jnp.concatenate([moe_router[j], jnp.zeros((D_MODEL, LANES - N_EXPERTS), f32)], axis=1)
            hn, logits = norm_mod(x, norm_ffn[l][None, :], mod_p, mod_s, 4, 3, router=router)
            plan = moe_route(logits[:, :N_EXPERTS], D_FF_EXPERT // MOE_TN)
            xs = jnp.take(hn, plan['row_src'], axis=0, mode='clip')
            h, wd_bf = moe_up(plan['item_m'], plan['item_n'], plan['item_e'], plan['item_v'], xs,
                              moe_w_gate[j], moe_w_up[j], moe_w_down[j])
            ys = moe_down(plan['tile_e'], plan['n_used'], h, wd_bf)
            gates = plan['gates']
            dest = plan['dest']
            f = (gates[:, 0:1] * jnp.take(ys, dest[:, 0], axis=0, mode='clip')
                 + gates[:, 1:2] * jnp.take(ys, dest[:, 1], axis=0, mode='clip'))
            g2_tok = jnp.concatenate([jnp.repeat(mod_p[5, :, 0, :], SEQ, axis=0),
                                      jnp.repeat(mod_s[5], DEC_SEQ, axis=0)], axis=0)
            x = x + g2_tok * f

    y_prompt, y_sample = final_norm(x, final_norm_w[None, :])
    y_prompt = y_prompt.reshape(BATCH, SEQ, D_MODEL)
    y_sample = y_sample.reshape(DEC_BATCH, DEC_SEQ, D_MODEL)
    sp_ = [jnp.stack(o) for o in st_p]
    ss_ = [jnp.stack(o) for o in st_s]
    return (y_prompt, y_sample, jnp.stack(ks_p), jnp.stack(vs_p), jnp.stack(ks_s), jnp.stack(vs_s),
            sp_[0], ss_[0], sp_[1], ss_[1], sp_[2], ss_[2], sp_[3], ss_[3], sp_[4], ss_[4])
```

```python
import functools
import math

import jax
import jax.numpy as jnp
from jax import lax
from jax.experimental import pallas as pl
from jax.experimental.pallas import tpu as pltpu

D_MODEL = 2048
BATCH = 8
SEQ = 2048
DEPTH = 2
DEC_BATCH = 128
DEC_SEQ = 8
W_GROUP = D_MODEL // 4
W_A = W_B = W_C = W_D = W_GROUP
H_A = 4
DH_A = W_A // (2 * H_A)
DV_A = 2 * DH_A
Q_BLOCK = 128
CONV_B = 3
H_C = 4
DK_C = W_C // H_C
DV_C = W_C // H_C
CONV_C = 4
DN_CHUNK = 64
H_D = 4
BW_D = W_D // H_D
CONV_D = 4
RG_C = 8.0
D_FF = 5632
N_EXPERTS = 8
TOP_K = 2
D_FF_EXPERT = 7168
N_MOD = 6
N_MIX = 4
EPS = 1e-6
N_A = 3 * W_A
N_B = 3 * W_B
N_C = 4 * W_C + 2 * H_C
N_D = 2 * W_D

N_PROMPT = BATCH * SEQ
N_SAMPLE = DEC_BATCH * DEC_SEQ
N_TOK = N_PROMPT + N_SAMPLE
SUBLANES = 8
LANES = 128

OFF_A = 0
OFF_B = N_A
OFF_C = N_A + N_B
OFF_D = OFF_C + 4 * W_C
N_PROJ = OFF_D + N_D

VMEM_LIMIT = 48 * 1024 * 1024
MOE_TM = 512
MOE_TN = 512
MOE_TK = 1024
MOE_CAST_ROWS = 112
MOE_TILES = (TOP_K * N_TOK) // MOE_TM + N_EXPERTS
MOE_ROWS = MOE_TILES * MOE_TM
ROW_TILE = 512
ATT_TQ = 512
ATT_HPB = 2
LRU_CHUNK = 256
LRU_SPS = 16
DN_SPS = 8
DN_SUB_PROMPT = 4

f32 = jnp.float32
bf16 = jnp.bfloat16


def _params(sem):
    return pltpu.CompilerParams(dimension_semantics=sem, vmem_limit_bytes=VMEM_LIMIT)


def _silu(x):
    return x * (1.0 / (1.0 + jnp.exp(-x)))


def _mod_kernel(c_ref, w_ref, b_ref, o_ref):
    c = _silu(c_ref[...]).astype(bf16)
    o_ref[...] = jnp.dot(c, w_ref[0].astype(bf16), preferred_element_type=f32) + b_ref[0]


def modulation(c_all, w, b, layer):
    r = c_all.shape[0]
    n = w.shape[2]
    tn = 1024
    return pl.pallas_call(
        _mod_kernel,
        grid=(n // tn,),
        in_specs=[pl.BlockSpec((r, D_MODEL), lambda j: (0, 0)),
                  pl.BlockSpec((1, D_MODEL, tn), lambda j: (layer, 0, j)),
                  pl.BlockSpec((1, 1, tn), lambda j: (layer, 0, j))],
        out_specs=pl.BlockSpec((r, tn), lambda j: (0, j)),
        out_shape=jax.ShapeDtypeStruct((r, n), f32),
        compiler_params=_params(("arbitrary",)),
        name="modulation",
    )(c_all, w, b)


def _mod_specs(idx, tile_rows, col_block):
    tiles_per_req = SEQ // tile_rows
    npt = N_PROMPT // tile_rows

    def p_map(*ids):
        return (idx, jnp.minimum(ids[0] // tiles_per_req, BATCH - 1), 0, col_block(*ids))

    def s_map(*ids):
        return (idx, jnp.maximum(ids[0] - npt, 0), col_block(*ids))

    return p_map, s_map


def _row_pair(x, tile_rows, cols, col_block):
    npt = N_PROMPT // tile_rows
    if isinstance(x, tuple):
        xp, xs = x
        s_map = lambda *ids: (jnp.maximum(ids[0] - npt, 0), col_block(*ids))
    else:
        xp = xs = x
        s_map = lambda *ids: (jnp.maximum(ids[0], npt), col_block(*ids))
    p_map = lambda *ids: (jnp.minimum(ids[0], npt - 1), col_block(*ids))
    return (xp, xs), [pl.BlockSpec((tile_rows, cols), p_map), pl.BlockSpec((tile_rows, cols), s_map)]


def _apply_rows(vals, i, tile_rows, fn_prompt, fn_sample):
    npt = N_PROMPT // tile_rows

    @pl.when(i < npt)
    def _():
        fn_prompt(*vals)

    @pl.when(i >= npt)
    def _():
        fn_sample(*vals)


def _bcast_groups(a, m):
    rows, c = a.shape
    return a.reshape(rows // SUBLANES, SUBLANES, c), m[:, None, :]


def _norm_mod_emit(xp_ref, xs_ref, nw_ref, scp_ref, shp_ref, scs_ref, shs_ref, emit):
    tr = xp_ref.shape[0]

    def normed(x):
        return x * lax.rsqrt(jnp.mean(x * x, axis=-1, keepdims=True) + EPS) * nw_ref[...]

    def prompt():
        emit(normed(xp_ref[...]) * (1.0 + scp_ref[0, 0]) + shp_ref[0, 0])

    def sample():
        y3, sc = _bcast_groups(normed(xs_ref[...]), scs_ref[0])
        emit((y3 * (1.0 + sc) + shs_ref[0][:, None, :]).reshape(tr, D_MODEL))

    _apply_rows((), pl.program_id(0), tr, prompt, sample)


def _norm_mod_kernel(xp_ref, xs_ref, nw_ref, scp_ref, shp_ref, scs_ref, shs_ref, o_ref):
    def emit(hn):
        o_ref[...] = hn.astype(o_ref.dtype)

    _norm_mod_emit(xp_ref, xs_ref, nw_ref, scp_ref, shp_ref, scs_ref, shs_ref, emit)


def _norm_mod_router_kernel(xp_ref, xs_ref, nw_ref, scp_ref, shp_ref, scs_ref, shs_ref, r_ref, o_ref, lg_ref):
    def emit(hn):
        bits = lax.bitcast_convert_type(hn.astype(bf16).astype(f32), jnp.uint32)
        half = D_MODEL // 2
        o_ref[...] = (bits[:, :half] >> 16) | bits[:, half:]
        lg_ref[...] = jnp.dot(hn, r_ref[...], preferred_element_type=f32, precision=lax.Precision.HIGHEST)

    _norm_mod_emit(xp_ref, xs_ref, nw_ref, scp_ref, shp_ref, scs_ref, shs_ref, emit)


def norm_mod(x, nw, mod_p, mod_s, i_scale, i_shift, router=None):
    tr = ROW_TILE
    tg = tr // SUBLANES
    zero = lambda i: 0
    scp_map, scs_map = _mod_specs(i_scale, tr, zero)
    shp_map, shs_map = _mod_specs(i_shift, tr, zero)
    x_ops, x_specs = _row_pair(x, tr, D_MODEL, zero)
    in_specs = x_specs + [
                pl.BlockSpec((1, D_MODEL), lambda i: (0, 0)),
                pl.BlockSpec((1, 1, 1, D_MODEL), scp_map), pl.BlockSpec((1, 1, 1, D_MODEL), shp_map),
                pl.BlockSpec((1, tg, D_MODEL), scs_map), pl.BlockSpec((1, tg, D_MODEL), shs_map)]
    out_spec = pl.BlockSpec((tr, D_MODEL), lambda i: (i, 0))
    out_shape = jax.ShapeDtypeStruct((N_TOK, D_MODEL), bf16)
    if router is not None:
        out_spec = pl.BlockSpec((tr, D_MODEL // 2), lambda i: (i, 0))
        out_shape = jax.ShapeDtypeStruct((N_TOK, D_MODEL // 2), jnp.uint32)
    if router is None:
        return pl.pallas_call(
            _norm_mod_kernel, grid=(N_TOK // tr,), in_specs=in_specs, out_specs=out_spec,
            out_shape=out_shape, compiler_params=_params(("arbitrary",)), name="norm_mod",
        )(*x_ops, nw, mod_p, mod_p, mod_s, mod_s)
    return pl.pallas_call(
        _norm_mod_router_kernel, grid=(N_TOK // tr,),
        in_specs=in_specs + [pl.BlockSpec((D_MODEL, LANES), lambda i: (0, 0))],
        out_specs=[out_spec, pl.BlockSpec((tr, LANES), lambda i: (i, 0))],
        out_shape=[out_shape, jax.ShapeDtypeStruct((N_TOK, LANES), f32)],
        compiler_params=_params(("arbitrary",)), name="norm_mod_router",
    )(*x_ops, nw, mod_p, mod_p, mod_s, mod_s, router)


def _final_norm_kernel(x_ref, nw_ref, op_ref, os_ref, *, n_prompt_tiles):
    x = x_ref[...]
    y = x * lax.rsqrt(jnp.mean(x * x, axis=-1, keepdims=True) + EPS) * nw_ref[...]
    i = pl.program_id(0)

    @pl.when(i < n_prompt_tiles)
    def _():
        op_ref[...] = y

    @pl.when(i >= n_prompt_tiles)
    def _():
        os_ref[...] = y


def final_norm(x, nw):
    tr = ROW_TILE
    npt = N_PROMPT // tr
    return pl.pallas_call(
        functools.partial(_final_norm_kernel, n_prompt_tiles=npt), grid=(N_TOK // tr,),
        in_specs=[pl.BlockSpec((tr, D_MODEL), lambda i: (i, 0)),
                  pl.BlockSpec((1, D_MODEL), lambda i: (0, 0))],
        out_specs=[pl.BlockSpec((tr, D_MODEL), lambda i: (jnp.minimum(i, npt - 1), 0)),
                   pl.BlockSpec((tr, D_MODEL), lambda i: (jnp.maximum(i - npt, 0), 0))],
        out_shape=[jax.ShapeDtypeStruct((N_PROMPT, D_MODEL), f32), jax.ShapeDtypeStruct((N_SAMPLE, D_MODEL), f32)],
        compiler_params=_params(("arbitrary",)), name="final_norm",
    )(x, nw)


def _in_proj_kernel(x_ref, w_ref, wab_ref, o_ref, oab_ref):
    x = x_ref[...]
    o_ref[...] = jnp.dot(x, w_ref[...], preferred_element_type=f32)

    @pl.when(pl.program_id(1) == 0)
    def _():
        oab_ref[...] = jnp.dot(x, wab_ref[...], preferred_element_type=f32)


def in_proj(x, w, w_ab, tm, tn):
    m, k = x.shape
    n = w.shape[1]
    return pl.pallas_call(
        _in_proj_kernel, grid=(m // tm, n // tn),
        in_specs=[pl.BlockSpec((tm, k), lambda i, j: (i, 0)),
                  pl.BlockSpec((k, tn), lambda i, j: (0, j)),
                  pl.BlockSpec((k, LANES), lambda i, j: (0, 0))],
        out_specs=[pl.BlockSpec((tm, tn), lambda i, j: (i, j)), pl.BlockSpec((tm, LANES), lambda i, j: (i, 0))],
        out_shape=[jax.ShapeDtypeStruct((m, n), f32), jax.ShapeDtypeStruct((m, LANES), f32)],
        compiler_params=_params(("arbitrary", "arbitrary")), name="in_proj",
    )(x, w, w_ab)


def _gated_residual_store(acc, xp_ref, xs_ref, gp_ref, gs_ref, o_ref):
    tm, tn = acc.shape

    def prompt(acc):
        o_ref[...] = xp_ref[...] + acc * gp_ref[0, 0]

    def sample(acc):
        a3, g = _bcast_groups(acc, gs_ref[0])
        o_ref[...] = xs_ref[...] + (a3 * g).reshape(tm, tn)

    _apply_rows((acc,), pl.program_id(0), tm, prompt, sample)


def _mm_resid_kernel(y_ref, w_ref, xp_ref, xs_ref, gp_ref, gs_ref, o_ref, acc_ref, *, nk):
    k = pl.program_id(2)
    part = jnp.dot(y_ref[...], w_ref[...], preferred_element_type=f32)

    @pl.when(k == 0)
    def _():
        acc_ref[...] = part

    @pl.when(k > 0)
    def _():
        acc_ref[...] += part

    @pl.when(k == nk - 1)
    def _():
        _gated_residual_store(acc_ref[...], xp_ref, xs_ref, gp_ref, gs_ref, o_ref)


def matmul_gated_residual(y, w, x, mod_p, mod_s, i_gate, tm, tn, tk):
    m, kdim = y.shape
    n = w.shape[1]
    nk = kdim // tk
    gp_map, gs_map = _mod_specs(i_gate, tm, lambda i, j, k: j)
    x_ops, x_specs = _row_pair(x, tm, tn, lambda i, j, k: j)
    return pl.pallas_call(
        functools.partial(_mm_resid_kernel, nk=nk), grid=(m // tm, n // tn, nk),
        in_specs=[pl.BlockSpec((tm, tk), lambda i, j, k: (i, k)),
                  pl.BlockSpec((tk, tn), lambda i, j, k: (k, j)),
                  *x_specs,
                  pl.BlockSpec((1, 1, 1, tn), gp_map),
                  pl.BlockSpec((1, tm // SUBLANES, tn), gs_map)],
        out_specs=pl.BlockSpec((tm, tn), lambda i, j, k: (i, j)),
        out_shape=jax.ShapeDtypeStruct((m, n), f32),
        scratch_shapes=[pltpu.VMEM((tm, tn), f32)],
        compiler_params=_params(("arbitrary", "arbitrary", "arbitrary")), name="matmul_gated_residual",
    )(y, w, *x_ops, mod_p, mod_s)


def _mix_out_kernel(*refs):
    y_refs, (w_ref, xp_ref, xs_ref, gp_ref, gs_ref, o_ref) = refs[:2 * N_MIX], refs[2 * N_MIX:]
    tm = o_ref.shape[0]

    def project(group_refs):
        acc = jnp.dot(group_refs[0][...], w_ref[0:W_GROUP, :], preferred_element_type=f32)
        for gi in range(1, N_MIX):
            acc = acc + jnp.dot(group_refs[gi][...], w_ref[gi * W_GROUP:(gi + 1) * W_GROUP, :],
                                preferred_element_type=f32)
        return acc

    def prompt():
        o_ref[...] = xp_ref[...] + project(y_refs[0::2]) * gp_ref[0, 0]

    def sample():
        a3, g = _bcast_groups(project(y_refs[1::2]), gs_ref[0])
        o_ref[...] = xs_ref[...] + (a3 * g).reshape(o_ref.shape)

    _apply_rows((), pl.program_id(0), tm, prompt, sample)


def mix_out_residual(ys, w, x, mod_p, mod_s, i_gate, tm, tn):
    m, n = N_TOK, D_MODEL
    gp_map, gs_map = _mod_specs(i_gate, tm, lambda i, j: j)
    x_ops, x_specs = _row_pair(x, tm, tn, lambda i, j: j)
    y_ops, y_specs = [], []
    for pair in ys:
        ops, specs = _row_pair(pair, tm, W_GROUP, lambda i, j: 0)
        y_ops += ops
        y_specs += specs
    return pl.pallas_call(
        _mix_out_kernel, grid=(m // tm, n // tn),
        in_specs=[*y_specs,
                  pl.BlockSpec((D_MODEL, tn), lambda i, j: (0, j)),
                  *x_specs,
                  pl.BlockSpec((1, 1, 1, tn), gp_map),
                  pl.BlockSpec((1, tm // SUBLANES, tn), gs_map)],
        out_specs=pl.BlockSpec((tm, tn), lambda i, j: (i, j)),
        out_shape=jax.ShapeDtypeStruct((m, n), f32),
        compiler_params=_params(("arbitrary", "arbitrary")), name="mix_out_residual",
    )(*y_ops, w, *x_ops, mod_p, mod_s)


def _swiglu_body(x, wg, wu):
    g = jnp.dot(x, wg, preferred_element_type=f32)
    u = jnp.dot(x, wu, preferred_element_type=f32)
    return _silu(g) * u


def _swiglu_kernel(x_ref, wg_ref, wu_ref, o_ref):
    o_ref[...] = _swiglu_body(x_ref[...], wg_ref[...], wu_ref[...]).astype(o_ref.dtype)


def swiglu_up(x, wg, wu, tm, tn):
    m, k = x.shape
    n = wg.shape[1]
    return pl.pallas_call(
        _swiglu_kernel, grid=(m // tm, n // tn),
        in_specs=[pl.BlockSpec((tm, k), lambda i, j: (i, 0)),
                  pl.BlockSpec((k, tn), lambda i, j: (0, j)),
                  pl.BlockSpec((k, tn), lambda i, j: (0, j))],
        out_specs=pl.BlockSpec((tm, tn), lambda i, j: (i, j)),
        out_shape=jax.ShapeDtypeStruct((m, n), bf16),
        compiler_params=_params(("arbitrary", "arbitrary")), name="swiglu_up",
    )(x, wg, wu)


def _unpack_tokens(words):
    lo = lax.bitcast_convert_type(words << 16, f32)
    hi = lax.bitcast_convert_type(words & jnp.uint32(0xFFFF0000), f32)
    return jnp.concatenate([lo, hi], axis=1).astype(bf16)


def _moe_up_kernel(m_ref, n_ref, wn_ref, e_ref, v_ref, x_ref, wg_ref, wu_ref, wd_ref, o_ref, wdb_ref, wg_sc, wu_sc,
                   *, n_cast):
    i = pl.program_id(0)

    @pl.when(i < n_cast)
    def _():
        wdb_ref[...] = wd_ref[...].astype(bf16)

    @pl.when(v_ref[i] == 2)
    def _():
        wg_sc[...] = wg_ref[0].astype(bf16)
        wu_sc[...] = wu_ref[0].astype(bf16)

    @pl.when(v_ref[i] > 0)
    def _():
        o_ref[...] = _swiglu_body(_unpack_tokens(x_ref[...]), wg_sc[...], wu_sc[...]).astype(o_ref.dtype)

    @pl.when(v_ref[i] == 0)
    def _():
        o_ref[...] = jnp.zeros(o_ref.shape, o_ref.dtype)


def moe_up(item_m, item_n, item_wn, item_e, item_v, xs, wg, wu, wd):
    ff = wg.shape[2]
    nt = ff // MOE_TN
    n_items = MOE_TILES * nt
    wd_rows = wd.shape[0] * wd.shape[1]
    n_cast = wd_rows // MOE_CAST_ROWS
    assert n_cast <= (TOP_K * N_TOK // MOE_TM) * nt and wd_rows % MOE_CAST_ROWS == 0
    cast_spec = pl.BlockSpec((MOE_CAST_ROWS, D_MODEL), lambda i, m, n, wn, e, v: (jnp.minimum(i, n_cast - 1), 0))
    grid_spec = pltpu.PrefetchScalarGridSpec(
        num_scalar_prefetch=5, grid=(n_items,),
        in_specs=[pl.BlockSpec((MOE_TM, D_MODEL // 2), lambda i, m, n, wn, e, v: (m[i], 0)),
                  pl.BlockSpec((1, D_MODEL, MOE_TN), lambda i, m, n, wn, e, v: (e[i], 0, wn[i])),
                  pl.BlockSpec((1, D_MODEL, MOE_TN), lambda i, m, n, wn, e, v: (e[i], 0, wn[i])),
                  cast_spec],
        out_specs=[pl.BlockSpec((MOE_TM, MOE_TN), lambda i, m, n, wn, e, v: (m[i], n[i])), cast_spec],
        scratch_shapes=[pltpu.VMEM((D_MODEL, MOE_TN), bf16), pltpu.VMEM((D_MODEL, MOE_TN), bf16)])
    h, wd_bf = pl.pallas_call(
        functools.partial(_moe_up_kernel, n_cast=n_cast), grid_spec=grid_spec,
        out_shape=[jax.ShapeDtypeStruct((MOE_ROWS, ff), bf16), jax.ShapeDtypeStruct((wd_rows, D_MODEL), bf16)],
        compiler_params=_params(("arbitrary",)), name="moe_up",
    )(item_m, item_n, item_wn, item_e, item_v, xs, wg, wu, wd.reshape(wd_rows, D_MODEL))
    return h, wd_bf.reshape(wd.shape)


def _moe_down_kernel(te_ref, nu_ref, h_ref, wd_ref, o_ref):
    m = pl.program_id(0)
    k = pl.program_id(1)

    @pl.when(m < nu_ref[0])
    def _():
        part = jnp.dot(h_ref[...], wd_ref[0], preferred_element_type=f32)

        @pl.when(k == 0)
        def _():
            o_ref[...] = part

        @pl.when(k > 0)
        def _():
            o_ref[...] += part

    @pl.when(jnp.logical_and(m >= nu_ref[0], k == 0))
    def _():
        o_ref[...] = jnp.zeros(o_ref.shape, o_ref.dtype)


def moe_down(tile_e, n_used, h, wd):
    nk = wd.shape[1] // MOE_TK

    def mclamp(m, nu):
        return jnp.minimum(m, nu[0] - 1)

    def kclamp(m, k, nu):
        return jnp.where(m < nu[0], k, nk - 1)

    grid_spec = pltpu.PrefetchScalarGridSpec(
        num_scalar_prefetch=2, grid=(MOE_TILES, nk),
        in_specs=[pl.BlockSpec((MOE_TM, MOE_TK), lambda m, k, te, nu: (mclamp(m, nu), kclamp(m, k, nu))),
                  pl.BlockSpec((1, MOE_TK, D_MODEL),
                               lambda m, k, te, nu: (te[mclamp(m, nu)], kclamp(m, k, nu), 0))],
        out_specs=pl.BlockSpec((MOE_TM, D_MODEL), lambda m, k, te, nu: (m, 0)))
    return pl.pallas_call(
        _moe_down_kernel, grid_spec=grid_spec,
        out_shape=jax.ShapeDtypeStruct((MOE_ROWS, D_MODEL), f32),
        compiler_params=_params(("arbitrary", "arbitrary")), name="moe_down",
    )(tile_e, n_used, h, wd)


def moe_route(logits, n_ff_tiles):
    top_v, top_i = lax.top_k(logits, TOP_K)
    gates = jax.nn.softmax(top_v, axis=-1)
    e_flat = top_i.reshape(-1).astype(jnp.int32)
    onehot = (e_flat[:, None] == jnp.arange(N_EXPERTS, dtype=jnp.int32)[None, :]).astype(jnp.int32)
    csum = jnp.cumsum(onehot, axis=0)
    counts = csum[-1]
    rank = jnp.sum((csum - onehot) * onehot, axis=-1)
    tiles_e = (counts + MOE_TM - 1) // MOE_TM
    tile_end = jnp.cumsum(tiles_e)
    tile_start = tile_end - tiles_e
    dest = tile_start[e_flat] * MOE_TM + rank
    tok = jnp.arange(TOP_K * N_TOK, dtype=jnp.int32) // TOP_K
    row_src = jnp.zeros((MOE_ROWS,), jnp.int32).at[dest].set(tok)
    n_used = tile_end[-1]
    t_ids = jnp.arange(MOE_TILES, dtype=jnp.int32)
    tile_e = jnp.minimum(jnp.searchsorted(tile_end, t_ids, side="right"), N_EXPERTS - 1).astype(jnp.int32)
    nt = n_ff_tiles
    items_e = tiles_e * nt
    item_end = jnp.cumsum(items_e)
    item_start = item_end - items_e
    total = item_end[-1]
    idx = jnp.arange(MOE_TILES * nt, dtype=jnp.int32)
    valid = idx < total
    idc = jnp.minimum(idx, total - 1)
    ie = jnp.minimum(jnp.searchsorted(item_end, idc, side="right"), N_EXPERTS - 1).astype(jnp.int32)
    local = idc - item_start[ie]
    te = jnp.maximum(tiles_e[ie], 1)
    item_wn = (local // te).astype(jnp.int32)
    spare = idx - total
    item_n = jnp.where(valid, item_wn, spare % nt).astype(jnp.int32)
    item_m = jnp.where(valid, tile_start[ie] + local % te, n_used + spare // nt).astype(jnp.int32)
    return dict(gates=gates, dest=dest.reshape(N_TOK, TOP_K), row_src=row_src, tile_e=tile_e,
                n_used=n_used.reshape(1).astype(jnp.int32), item_m=item_m, item_n=item_n, item_wn=item_wn, item_e=ie,
                item_v=jnp.where(valid, jnp.where(local % te == 0, 2, 1), 0).astype(jnp.int32))


def _smem_spec():
    return pl.BlockSpec(memory_space=pltpu.SMEM)


def _subln(o, w, gain):
    return o * lax.rsqrt(jnp.mean(o * o, axis=-1, keepdims=True) + EPS) * w * gain


def _attn_prompt_kernel(lam_ref, slope_ref, q_ref, k_ref, v_ref, w_ref, o_ref, m_sc, acc_sc, *, tq, hpb, gain):
    hb = pl.program_id(1)
    qi = pl.program_id(2)
    lane = lax.broadcasted_iota(jnp.int32, (tq, DV_A), 1)
    row = lax.broadcasted_iota(jnp.int32, (2 * tq, tq), 0)
    col = lax.broadcasted_iota(jnp.int32, (2 * tq, tq), 1)
    rel = (col - jnp.where(row >= tq, row - tq, row)).astype(f32)
    ones_v = jnp.ones((tq, DV_A), bf16)
    qss, slopes, biases = [], [], []
    for hh in range(hpb):
        q = q_ref[:, hh * DV_A:(hh + 1) * DV_A] * (DH_A ** -0.5)
        qss.append(jnp.concatenate([jnp.where(lane < DH_A, q, 0.0), jnp.where(lane >= DH_A, q, 0.0)],
                                   axis=0).astype(bf16))
        slopes.append(slope_ref[hb * hpb + hh])
        biases.append(slopes[hh] * rel)
    m_sc[...] = jnp.full(m_sc.shape, -jnp.inf, f32)
    acc_sc[...] = jnp.zeros(acc_sc.shape, f32)

    def chunk_step(j, diagonal):
        off = pl.multiple_of(j * tq, tq)
        blocks = (jnp.zeros((1, 1), jnp.int32) + (qi - j) * tq).astype(f32)
        for hh in range(hpb):
            cs = slice(hh * DV_A, (hh + 1) * DV_A)
            kc = k_ref[pl.ds(off, tq), cs].astype(bf16)
            vc = jnp.concatenate([v_ref[pl.ds(off, tq), cs].astype(bf16), ones_v], axis=1)
            s = lax.dot_general(qss[hh], kc, (((1,), (1,)), ((), ())), preferred_element_type=f32)
            s = s + biases[hh] - slopes[hh] * blocks
            if diagonal:
                s = jnp.where(rel <= 0, s, -1e30)
            m_prev = m_sc[hh]
            m_new = jnp.maximum(m_prev, jnp.max(s, axis=-1, keepdims=True))
            pr = jnp.exp(s - m_new).astype(bf16)
            acc_sc[hh] = jnp.exp(m_prev - m_new) * acc_sc[hh] + jnp.dot(pr, vc, preferred_element_type=f32)
            m_sc[hh] = m_new

    def body(j, carry):
        chunk_step(j, False)
        return carry

    lax.fori_loop(0, qi, body, 0)
    chunk_step(qi, True)
    for hh in range(hpb):
        acc = acc_sc[hh]
        o = acc[:, :DV_A] / acc[:, DV_A:]
        o = o[:tq] - lam_ref[0] * o[tq:]
        o_ref[:, hh * DV_A:(hh + 1) * DV_A] = _subln(o, w_ref[...], gain).astype(o_ref.dtype)


def attn_prompt(proj, lam, slopes, subln_w, gain, batch, seq, tq, hpb):
    nq = seq // tq
    wb = hpb * DV_A
    nhb = H_A // hpb
    return pl.pallas_call(
        functools.partial(_attn_prompt_kernel, tq=tq, hpb=hpb, gain=gain),
        grid=(batch, nhb, nq),
        in_specs=[_smem_spec(), _smem_spec(),
                  pl.BlockSpec((tq, wb), lambda b, h, qi: (b * nq + qi, h)),
                  pl.BlockSpec((seq, wb), lambda b, h, qi: (b, nhb + h)),
                  pl.BlockSpec((seq, wb), lambda b, h, qi: (b, 2 * nhb + h)),
                  pl.BlockSpec((1, DV_A), lambda b, h, qi: (0, 0))],
        out_specs=pl.BlockSpec((tq, wb), lambda b, h, qi: (b * nq + qi, h)),
        out_shape=jax.ShapeDtypeStruct((batch * seq, W_A), bf16),
        scratch_shapes=[pltpu.VMEM((hpb, 2 * tq, 1), f32), pltpu.VMEM((hpb, 2 * tq, 2 * DV_A), f32)],
        compiler_params=_params(("arbitrary", "arbitrary", "arbitrary")), name="attn_prompt",
    )(lam, slopes, proj, proj, proj, subln_w)


def _attn_sample_kernel(pt_ref, lam_ref, slope_ref, q_ref, kn_ref, vn_ref, w_ref, *rest, t, n_pages, page, gain):
    k_refs = rest[:n_pages]
    v_refs = rest[n_pages:2 * n_pages]
    o_ref = rest[2 * n_pages]
    s_sc = rest[2 * n_pages + 1]
    n_rep = 2 * H_A
    rows = n_rep * t
    n_past = n_pages * page
    q = q_ref[...] * (DH_A ** -0.5)
    q_rep = jnp.concatenate([q] * n_rep, axis=0)
    r_grp = lax.broadcasted_iota(jnp.int32, (rows, W_A), 0) // t
    c_grp = lax.broadcasted_iota(jnp.int32, (rows, W_A), 1) // DH_A
    qbd = jnp.where(r_grp == c_grp, q_rep, 0.0).astype(bf16)
    for j in range(n_pages):
        s_sc[:, j * page:(j + 1) * page] = jnp.dot(qbd, k_refs[j][0, 0].astype(bf16), preferred_element_type=f32)
    s_new = lax.dot_general(qbd, kn_ref[...].astype(bf16), (((1,), (1,)), ((), ())), preferred_element_type=f32)

    row1 = lax.broadcasted_iota(jnp.int32, (rows, 1), 0)
    head1 = row1 // (2 * t)
    slope = jnp.where(head1 == 0, slope_ref[0],
                      jnp.where(head1 == 1, slope_ref[1], jnp.where(head1 == 2, slope_ref[2], slope_ref[3])))
    t_row = (row1 % t)
    kpos = lax.broadcasted_iota(jnp.int32, (rows, n_past), 1)
    s_past = s_sc[...] - slope * (n_past + t_row - kpos).astype(f32)
    dist_new = (t_row - lax.broadcasted_iota(jnp.int32, (rows, t), 1)).astype(f32)
    s_new = jnp.where(dist_new >= 0, s_new - slope * dist_new, -1e30)
    m = jnp.maximum(jnp.max(s_past, axis=-1, keepdims=True), jnp.max(s_new, axis=-1, keepdims=True))
    p_past = jnp.exp(s_past - m)
    p_new = jnp.exp(s_new - m)
    denom = jnp.sum(p_past, axis=-1, keepdims=True) + jnp.sum(p_new, axis=-1, keepdims=True)
    p_past = p_past.astype(bf16)
    p_new = p_new.astype(bf16)
    lam = lam_ref[0]
    w = w_ref[...]
    for h in range(H_A):
        vh = jnp.concatenate([v_refs[j][0, 0, pl.ds(h, page, stride=H_A), :] for j in range(n_pages)], axis=0)
        rs = slice(h * 2 * t, (h + 1) * 2 * t)
        oh = jnp.dot(p_past[rs], vh.astype(bf16), preferred_element_type=f32)
        oh = oh + jnp.dot(p_new[rs], vn_ref[:, h * DV_A:(h + 1) * DV_A].astype(bf16), preferred_element_type=f32)
        oh = oh / denom[rs]
        o = oh[:t] - lam * oh[t:]
        o_ref[:, h * DV_A:(h + 1) * DV_A] = _subln(o, w, gain).astype(o_ref.dtype)


def attn_sample(proj, ckt, cvr, layer, page_table, lam, slopes, subln_w, gain, row0, nseq, t):
    n_pages = page_table.shape[1]
    page = ckt.shape[-1]
    rb0 = row0 // t

    def kv_spec(j):
        return pl.BlockSpec((1, 1, W_A, page), lambda b, pt, j=j: (layer, pt[b * n_pages + j], 0, 0))

    in_specs = ([_smem_spec(), _smem_spec(),
                 pl.BlockSpec((t, W_A), lambda b, pt: (rb0 + b, 0)),
                 pl.BlockSpec((t, W_A), lambda b, pt: (rb0 + b, 1)),
                 pl.BlockSpec((t, W_A), lambda b, pt: (rb0 + b, 2)),
                 pl.BlockSpec((1, DV_A), lambda b, pt: (0, 0))]
                + [kv_spec(j) for j in range(n_pages)] + [kv_spec(j) for j in range(n_pages)])
    grid_spec = pltpu.PrefetchScalarGridSpec(
        num_scalar_prefetch=1, grid=(nseq,), in_specs=in_specs,
        out_specs=pl.BlockSpec((t, W_A), lambda b, pt: (b, 0)),
        scratch_shapes=[pltpu.VMEM((2 * H_A * t, n_pages * page), f32)])
    return pl.pallas_call(
        functools.partial(_attn_sample_kernel, t=t, n_pages=n_pages, page=page, gain=gain),
        grid_spec=grid_spec, out_shape=jax.ShapeDtypeStruct((nseq * t, W_A), bf16),
        compiler_params=_params(("arbitrary",)), name="attn_sample",
    )(page_table.reshape(-1), lam, slopes, proj, proj, proj, subln_w, *([ckt] * n_pages), *([cvr] * n_pages))


def _conv_step(f_sc, x3, w_ref, buf0_ref, buf_out, width, chunk, first):
    lo = SUBLANES - (width - 1)

    @pl.when(first)
    def _():
        f_sc[:, 0:SUBLANES, :] = jnp.zeros((f_sc.shape[0], SUBLANES, f_sc.shape[2]), f32)
        f_sc[:, lo:SUBLANES, :] = buf0_ref[...]

    f_sc[:, SUBLANES:SUBLANES + chunk, :] = x3
    out = w_ref[0:1, :][None] * f_sc[:, lo:lo + chunk, :]
    for j in range(1, width):
        out = out + w_ref[j:j + 1, :][None] * f_sc[:, lo + j:lo + j + chunk, :]
    buf_out[...] = f_sc[:, chunk + lo:chunk + SUBLANES, :]
    f_sc[:, 0:SUBLANES, :] = f_sc[:, chunk:chunk + SUBLANES, :]
    return out


def _sigmoid(x):
    return 1.0 / (1.0 + jnp.exp(-x))


def _gelu_tanh(x):
    return 0.5 * x * (1.0 + jnp.tanh(0.7978845608028654 * (x + 0.044715 * (x * x * x))))


def _conv_lru_kernel(pb_ref, pd_ref, cb0_ref, lb0_ref, h0_ref, wb_ref, wd_ref, bd_ref, wa_ref, ba_ref, wx_ref,
                     bx_ref, sp_ref, yb_ref, yd_ref, cb_out, lb_out, h_out, fb_sc, fd_sc, h_sc, *, sps, chunk):
    first = pl.program_id(1) == 0
    rows = sps * chunk
    pb = pb_ref[...]
    bg, cg, xt = pb[:, :W_B], pb[:, W_B:2 * W_B], pb[:, 2 * W_B:]
    conv_b = _conv_step(fb_sc, (cg * xt).reshape(sps, chunk, W_B), wb_ref, cb0_ref, cb_out, CONV_B, chunk, first)
    yb_ref[...] = (bg * conv_b.reshape(rows, W_B)).astype(yb_ref.dtype)

    pd = pd_ref[...]
    xd, gd = pd[:, :W_D], pd[:, W_D:]
    conv_d = _conv_step(fd_sc, xd.reshape(sps, chunk, W_D), wd_ref, lb0_ref, lb_out, CONV_D, chunk, first)
    xf = conv_d.reshape(rows, W_D) + bd_ref[...]
    xfb = xf.astype(bf16)
    ra = jnp.concatenate([jnp.dot(xfb[:, h * BW_D:(h + 1) * BW_D], wa_ref[h].astype(bf16),
                                  preferred_element_type=f32) for h in range(H_D)], axis=-1)
    rx = jnp.concatenate([jnp.dot(xfb[:, h * BW_D:(h + 1) * BW_D], wx_ref[h].astype(bf16),
                                  preferred_element_type=f32) for h in range(H_D)], axis=-1)
    r = _sigmoid(ra + ba_ref[...])
    ig = _sigmoid(rx + bx_ref[...])
    log_a = -RG_C * r * sp_ref[...]
    a = jnp.exp(log_a)
    th = jnp.tanh(log_a)
    bx = jnp.sqrt(-2.0 * th / (1.0 - th)) * (ig * xf)

    pos = lax.broadcasted_iota(jnp.int32, (rows, W_D), 0) % chunk
    step = 1
    while step < chunk:
        a_s = pltpu.roll(a, step, 0)
        b_s = pltpu.roll(bx, step, 0)
        live = pos >= step
        bx = jnp.where(live, a * b_s + bx, bx)
        a = jnp.where(live, a * a_s, a)
        step *= 2

    @pl.when(first)
    def _():
        h_sc[...] = h0_ref[...]

    h_in = jnp.broadcast_to(h_sc[...], (sps, chunk, W_D)).reshape(rows, W_D)
    hs = bx + a * h_in
    h_last = hs.reshape(sps, chunk, W_D)[:, chunk - 1:chunk, :]
    h_sc[...] = h_last
    h_out[...] = h_last
    yd_ref[...] = (hs * _gelu_tanh(gd)).astype(yd_ref.dtype)


def conv_lru(proj, row0, nseq, t, sps, chunk, conv_b_buf, lru_buf, lru_h, wb, wd, bd, wa, ba, wx, bx, sp):
    rows = sps * chunk
    nc = t // chunk
    rb0 = row0 // rows

    def rmap(s, c):
        return rb0 + s * nc + c

    def full(shape):
        return pl.BlockSpec(shape, lambda s, c: (0,) * len(shape))

    def per_seq(shape):
        return pl.BlockSpec((sps,) + shape, lambda s, c: (s,) + (0,) * len(shape))

    in_specs = [pl.BlockSpec((rows, N_B), lambda s, c: (rmap(s, c), OFF_B // N_B)),
                pl.BlockSpec((rows, N_D), lambda s, c: (rmap(s, c), OFF_D // N_D)),
                per_seq((CONV_B - 1, W_B)), per_seq((CONV_D - 1, W_D)), per_seq((1, W_D)),
                full((CONV_B, W_B)), full((CONV_D, W_D)), full((1, W_D)),
                full((H_D, BW_D, BW_D)), full((1, W_D)), full((H_D, BW_D, BW_D)), full((1, W_D)), full((1, W_D))]
    out_specs = [pl.BlockSpec((rows, W_B), lambda s, c: (s * nc + c, 0)),
                 pl.BlockSpec((rows, W_D), lambda s, c: (s * nc + c, 0)),
                 per_seq((CONV_B - 1, W_B)), per_seq((CONV_D - 1, W_D)), per_seq((1, W_D))]
    out_shape = [jax.ShapeDtypeStruct((nseq * t, W_B), bf16), jax.ShapeDtypeStruct((nseq * t, W_D), bf16),
                 jax.ShapeDtypeStruct((nseq, CONV_B - 1, W_B), f32),
                 jax.ShapeDtypeStruct((nseq, CONV_D - 1, W_D), f32),
                 jax.ShapeDtypeStruct((nseq, 1, W_D), f32)]
    return pl.pallas_call(
        functools.partial(_conv_lru_kernel, sps=sps, chunk=chunk),
        grid=(nseq // sps, nc), in_specs=in_specs, out_specs=out_specs, out_shape=out_shape,
        scratch_shapes=[pltpu.VMEM((sps, chunk + SUBLANES, W_B), f32), pltpu.VMEM((sps, chunk + SUBLANES, W_D), f32),
                        pltpu.VMEM((sps, 1, W_D), f32)],
        compiler_params=_params(("arbitrary", "arbitrary")), name="conv_lru",
    )(proj, proj, conv_b_buf, lru_buf, lru_h, wb, wd, bd, wa, ba, wx, bx, sp)


_HI = lax.Precision.HIGHEST


def _dot_nt(a, b, precision=None):
    return lax.dot_general(a, b, (((1,), (1,)), ((), ())), preferred_element_type=f32, precision=precision)


def _dot_tn(a, b):
    return lax.dot_general(a, b, (((0,), (0,)), ((), ())), preferred_element_type=f32)


def _hi_lo(x):
    hi = x.astype(bf16).astype(f32)
    return hi, x - hi


def _dot_f32(a, b, chunk):
    if chunk % 16:
        return jnp.dot(a, b, preferred_element_type=f32, precision=_HI)
    a_hi, a_lo = _hi_lo(a)
    b_hi, b_lo = _hi_lo(b)
    lhs = jnp.concatenate([a_hi, a_hi, a_lo], axis=1).astype(bf16)
    rhs = jnp.concatenate([b_hi, b_lo, b_hi], axis=0).astype(bf16)
    return jnp.dot(lhs, rhs, preferred_element_type=f32)


def _cumsum_rows(tri, x, chunk):
    if chunk % 16:
        return jnp.dot(tri, x, preferred_element_type=f32, precision=_HI)
    hi, r1 = _hi_lo(x)
    mid, lo = _hi_lo(r1)
    lhs = jnp.concatenate([tri, tri, tri], axis=1).astype(bf16)
    rhs = jnp.concatenate([hi, mid, lo], axis=0).astype(bf16)
    return jnp.dot(lhs, rhs, preferred_element_type=f32)


def _row_bcast(gc, pick, chunk):
    sel = pick.astype(f32)
    if chunk % 16:
        return _dot_nt(sel, gc, precision=_HI)
    hi, r1 = _hi_lo(gc)
    mid, lo = _hi_lo(r1)
    lhs = jnp.concatenate([sel, sel, sel], axis=1).astype(bf16)
    rhs = jnp.concatenate([hi, mid, lo], axis=1).astype(bf16)
    return _dot_nt(lhs, rhs)


def _deltanet_kernel(qkv_ref, z_ref, ab_ref, buf0_ref, s0_ref, wc_ref, nega_ref, dtb_ref, nw_ref,
                     y_ref, buf_out, s_out, f_sc, s_sc, *, sps, nsub, chunk):
    first = pl.program_id(1) == 0
    span_rows = nsub * chunk
    rows = sps * span_rows
    wq = 3 * W_C
    conv = _conv_step(f_sc, qkv_ref[...].reshape(sps, span_rows, wq), wc_ref, buf0_ref, buf_out, CONV_C,
                      span_rows, first)
    conv = conv.reshape(rows, wq)
    qkv = conv * _sigmoid(conv)

    @pl.when(first)
    def _():
        s_sc[...] = s0_ref[...]

    ab = ab_ref[...]
    xg = ab + dtb_ref[...]
    g_all = nega_ref[...] * (jnp.maximum(xg, 0.0) + jnp.log1p(jnp.exp(-jnp.abs(xg))))
    beta_all = _sigmoid(ab)
    z = z_ref[...]
    ri = lax.broadcasted_iota(jnp.int32, (chunk, chunk), 0)
    ci = lax.broadcasted_iota(jnp.int32, (chunk, chunk), 1)
    causal = ri >= ci
    strict = ri > ci
    tri = causal.astype(f32)
    eye = (ri == ci).astype(f32)
    lane = lax.broadcasted_iota(jnp.int32, (chunk, LANES), 1)
    nw = nw_ref[...]
    spans = [(i, c) for i in range(sps) for c in range(nsub)]
    units = [(i, c, h) for i, c in spans for h in range(H_C)]

    def rows_of(i, c):
        r0 = (i * nsub + c) * chunk
        return slice(r0, r0 + chunk)

    gcs = {sp: _cumsum_rows(tri, g_all[rows_of(*sp)], chunk) for sp in spans}

    qs, ks, gcols, rhss, qks, negs = [], [], [], [], [], []
    for i, c, h in units:
        rs = rows_of(i, c)
        gc = gcs[(i, c)]
        q = qkv[rs, h * DK_C:(h + 1) * DK_C]
        k = qkv[rs, W_C + h * DK_C:W_C + (h + 1) * DK_C]
        v = qkv[rs, 2 * W_C + h * DV_C:2 * W_C + (h + 1) * DV_C]
        q = q * lax.rsqrt(jnp.sum(q * q, axis=-1, keepdims=True) + EPS) * (DK_C ** -0.5)
        k = k * lax.rsqrt(jnp.sum(k * k, axis=-1, keepdims=True) + EPS)
        beta = beta_all[rs, H_C + h:H_C + h + 1]
        gcol = gc[:, h:h + 1]
        grow = _row_bcast(gc, lane == h, chunk)
        decay = jnp.where(causal, jnp.exp(jnp.where(causal, gcol - grow, 0.0)), 0.0)
        eg = jnp.exp(gcol)
        kb = k * beta
        negs.append(jnp.where(strict, _dot_nt(kb, k) * decay, 0.0) * -1.0)
        qks.append(_dot_nt(q, k) * decay)
        rhss.append(jnp.concatenate([v * beta, kb * eg], axis=1))
        qs.append(q * eg)
        ks.append(k)
        gcols.append(gcol)

    invs = [eye + n for n in negs]
    span = 2
    while span < chunk:
        negs = [_dot_f32(n, n, chunk) for n in negs]
        invs = [inv + _dot_f32(inv, n, chunk) for inv, n in zip(invs, negs)]
        span *= 2
    sols = [_dot_f32(inv, rhs, chunk) for inv, rhs in zip(invs, rhss)]

    states = {(i, h): s_sc[i, h] for i in range(sps) for h in range(H_C)}
    outs = {}
    for c in range(nsub):
        cur = [(idx, (i, h)) for idx, (i, cc, h) in enumerate(units) if cc == c]
        wq_s = {idx: jnp.dot(jnp.concatenate([sols[idx][:, DV_C:], qs[idx]], axis=0), states[key],
                             preferred_element_type=f32) for idx, key in cur}
        v_new = {idx: sols[idx][:, :DV_C] - wq_s[idx][:chunk] for idx, _ in cur}
        for idx, _ in cur:
            outs[idx] = wq_s[idx][chunk:] + jnp.dot(qks[idx], v_new[idx], preferred_element_type=f32)
        for idx, key in cur:
            gcol = gcols[idx]
            g_last = gcol[chunk - 1:chunk, :]
            states[key] = (states[key] * jnp.exp(g_last)
                           + _dot_tn(ks[idx] * jnp.exp(g_last - gcol), v_new[idx]))
    for (i, h), st in states.items():
        s_sc[i, h] = st
        s_out[i, h] = st
    for idx, (i, c, h) in enumerate(units):
        rs = rows_of(i, c)
        zh = z[rs, h * DV_C:(h + 1) * DV_C]
        o = outs[idx]
        o = o * lax.rsqrt(jnp.mean(o * o, axis=-1, keepdims=True) + EPS) * nw
        y_ref[rs, h * DV_C:(h + 1) * DV_C] = (o * (zh * _sigmoid(zh))).astype(y_ref.dtype)


def deltanet(proj, proj_ab, row0, nseq, t, sps, nsub, chunk, dn_buf, dn_s, wc, nega, dtb, nw):
    rows = sps * nsub * chunk
    nc = t // (nsub * chunk)
    rb0 = row0 // rows
    wq = 3 * W_C

    def rmap(s, c):
        return rb0 + s * nc + c

    def full(shape):
        return pl.BlockSpec(shape, lambda s, c: (0,) * len(shape))

    def per_seq(shape):
        return pl.BlockSpec((sps,) + shape, lambda s, c: (s,) + (0,) * len(shape))

    in_specs = [pl.BlockSpec((rows, wq), lambda s, c: (rmap(s, c), OFF_C // wq)),
                pl.BlockSpec((rows, W_C), lambda s, c: (rmap(s, c), (OFF_C + wq) // W_C)),
                pl.BlockSpec((rows, LANES), lambda s, c: (rmap(s, c), 0)),
                per_seq((CONV_C - 1, wq)), per_seq((H_C, DK_C, DV_C)),
                full((CONV_C, wq)), full((1, LANES)), full((1, LANES)), full((1, DV_C))]
    out_specs = [pl.BlockSpec((rows, W_C), lambda s, c: (s * nc + c, 0)),
                 per_seq((CONV_C - 1, wq)), per_seq((H_C, DK_C, DV_C))]
    out_shape = [jax.ShapeDtypeStruct((nseq * t, W_C), bf16),
                 jax.ShapeDtypeStruct((nseq, CONV_C - 1, wq), f32),
                 jax.ShapeDtypeStruct((nseq, H_C, DK_C, DV_C), f32)]
    return pl.pallas_call(
        functools.partial(_deltanet_kernel, sps=sps, nsub=nsub, chunk=chunk),
        grid=(nseq // sps, nc), in_specs=in_specs, out_specs=out_specs, out_shape=out_shape,
        scratch_shapes=[pltpu.VMEM((sps, nsub * chunk + SUBLANES, wq), f32),
                        pltpu.VMEM((sps, H_C, DK_C, DV_C), f32)],
        compiler_params=_params(("arbitrary", "arbitrary")), name="deltanet",
    )(proj, proj, proj_ab, dn_buf, dn_s, wc, nega, dtb, nw)


def _lane_row(v):
    return jnp.concatenate([v.astype(f32), jnp.zeros((LANES - v.shape[0],), f32)])[None, :]


def kernel(x_prompt, x_sample, cache_k, cache_v, state_conv_b, state_dn_conv, state_dn, state_lru_conv, state_lru_h, page_table, c_prompt, c_sample, w_mod, b_mod, norm_mix, norm_ffn, w_in, w_out, lam_q1, lam_k1, lam_q2, lam_k2, subln_w, conv_b_w, dn_conv_w, dn_a_log, dn_dt_bias, dn_norm_w, lru_conv_w, lru_conv_b, lru_wa, lru_ba, lru_wx, lru_bx, lru_lambda, ffn_w_gate, ffn_w_up, ffn_w_down, moe_router, moe_w_gate, moe_w_up, moe_w_down, final_norm_w):
    x = (x_prompt.reshape(N_PROMPT, D_MODEL), x_sample.reshape(N_SAMPLE, D_MODEL))
    c_all = jnp.concatenate([c_prompt, c_sample], axis=0)
    n_pool, page = cache_k.shape[1], cache_k.shape[2]
    ckt = jnp.transpose(cache_k, (0, 1, 3, 4, 5, 2)).reshape(DEPTH, n_pool, W_A, page)
    cvr = cache_v.reshape(DEPTH, n_pool, page * H_A, DV_A)
    slopes = jnp.exp2(-8.0 * jnp.arange(1, H_A + 1, dtype=f32) / H_A)

    zero_cb = jnp.zeros((BATCH, CONV_B - 1, W_B), f32)
    zero_dc = jnp.zeros((BATCH, CONV_C - 1, 3 * W_C), f32)
    zero_ds = jnp.zeros((BATCH, H_C, DK_C, DV_C), f32)
    zero_lc = jnp.zeros((BATCH, CONV_D - 1, W_D), f32)
    zero_lh = jnp.zeros((BATCH, 1, W_D), f32)

    ks_p, vs_p, ks_s, vs_s = [], [], [], []
    st_p = [[] for _ in range(5)]
    st_s = [[] for _ in range(5)]
    for l in range(DEPTH):
        mod = modulation(c_all, w_mod, b_mod[:, None, :], l)
        mod_p = jnp.transpose(mod[:BATCH].reshape(BATCH, N_MOD, D_MODEL), (1, 0, 2))[:, :, None, :]
        mod_s = jnp.transpose(mod[BATCH:].reshape(DEC_BATCH, N_MOD, D_MODEL), (1, 0, 2))

        hn = norm_mod(x, norm_mix[l][None, :], mod_p, mod_s, 1, 0)
        wl = w_in[l]
        n_cqkvz = 4 * W_C
        w_pack = jnp.concatenate([wl[:, :N_A + N_B + n_cqkvz], wl[:, N_A + N_B + N_C:]], axis=1).astype(bf16)
        w_ab = jnp.concatenate([wl[:, N_A + N_B + n_cqkvz:N_A + N_B + N_C],
                                jnp.zeros((D_MODEL, LANES - 2 * H_C), f32)], axis=1).astype(bf16)
        proj, proj_ab = in_proj(hn, w_pack, w_ab, tm=1088, tn=1024)

        lam_init = 0.8 - 0.6 * math.exp(-0.3 * l)
        lam = (jnp.exp(jnp.sum(lam_q1[l] * lam_k1[l])) - jnp.exp(jnp.sum(lam_q2[l] * lam_k2[l])) + lam_init)
        lam = lam.astype(f32).reshape(1)
        gain = 1.0 - lam_init
        sw = subln_w[l][None, :]
        ya = (attn_prompt(proj, lam, slopes, sw, gain, BATCH, SEQ, ATT_TQ, ATT_HPB),
              attn_sample(proj, ckt, cvr, l, page_table, lam, slopes, sw, gain, N_PROMPT, DEC_BATCH, DEC_SEQ))

        sp = jax.nn.softplus(-lru_lambda[l])[None, :]
        lru_args = (conv_b_w[l], lru_conv_w[l], lru_conv_b[l][None, :], lru_wa[l], lru_ba[l][None, :],
                    lru_wx[l], lru_bx[l][None, :], sp)
        yb_p, yd_p, cb_p, lb_p, lh_p = conv_lru(proj, 0, BATCH, SEQ, 1, LRU_CHUNK, zero_cb, zero_lc, zero_lh,
                                                *lru_args)
        yb_s, yd_s, cb_s, lb_s, lh_s = conv_lru(proj, N_PROMPT, DEC_BATCH, DEC_SEQ, LRU_SPS, DEC_SEQ,
                                                state_conv_b[l], state_lru_conv[l], state_lru_h[l][:, None, :],
                                                *lru_args)

        dn_args = (dn_conv_w[l], _lane_row(-jnp.exp(dn_a_log[l])), _lane_row(dn_dt_bias[l]), dn_norm_w[l][None, :])
        yc_p, dc_p, ds_p = deltanet(proj, proj_ab, 0, BATCH, SEQ, 1, DN_SUB_PROMPT, DN_CHUNK, zero_dc, zero_ds,
                                    *dn_args)
        yc_s, dc_s, ds_s = deltanet(proj, proj_ab, N_PROMPT, DEC_BATCH, DEC_SEQ, DN_SPS, 1,
                                    math.gcd(DEC_SEQ, DN_CHUNK), state_dn_conv[l], state_dn[l], *dn_args)

        k_new = proj[:, W_A:2 * W_A]
        v_new = proj[:, 2 * W_A:3 * W_A]
        ks_p.append(k_new[:N_PROMPT].reshape(BATCH, SEQ, H_A, 2, DH_A))
        vs_p.append(v_new[:N_PROMPT].reshape(BATCH, SEQ, H_A, DV_A))
        ks_s.append(k_new[N_PROMPT:].reshape(DEC_BATCH, DEC_SEQ, H_A, 2, DH_A))
        vs_s.append(v_new[N_PROMPT:].reshape(DEC_BATCH, DEC_SEQ, H_A, DV_A))
        for lst, vals in ((st_p, (cb_p, dc_p, ds_p, lb_p, lh_p[:, 0, :])), (st_s, (cb_s, dc_s, ds_s, lb_s, lh_s[:, 0, :]))):
            for i, val in enumerate(vals):
                lst[i].append(val)

        x = mix_out_residual((ya, (yb_p, yb_s), (yc_p, yc_s), (yd_p, yd_s)), w_out[l].astype(bf16), x,
                             mod_p, mod_s, 2, tm=1024, tn=1024)

        j = l // 2
        if l % 2 == 0:
            hn = norm_mod(x, norm_ffn[l][None, :], mod_p, mod_s, 4, 3)
            h = swiglu_up(hn, ffn_w_gate[j].astype(bf16), ffn_w_up[j].astype(bf16), tm=1088, tn=512)
            x = matmul_gated_residual(h, ffn_w_down[j].astype(bf16), x, mod_p, mod_s, 5, tm=1024, tn=256, tk=D_FF)
        else:
            router = jnp.concatenate([moe_router[j], jnp.zeros((D_MODEL, LANES - N_EXPERTS), f32)], axis=1)
            hn, logits = norm_mod(x, norm_ffn[l][None, :], mod_p, mod_s, 4, 3, router=router)
            plan = moe_route(logits[:, :N_EXPERTS], D_FF_EXPERT // MOE_TN)
            xs = jnp.take(hn, plan['row_src'], axis=0, mode='clip')
            h, wd_bf = moe_up(plan['item_m'], plan['item_n'], plan['item_wn'], plan['item_e'], plan['item_v'], xs,
                              moe_w_gate[j], moe_w_up[j], moe_w_down[j])
            ys = moe_down(plan['tile_e'], plan['n_used'], h, wd_bf)
            gates = plan['gates']
            dest = plan['dest']
            f = (gates[:, 0:1] * jnp.take(ys, dest[:, 0], axis=0, mode='clip')
                 + gates[:, 1:2] * jnp.take(ys, dest[:, 1], axis=0, mode='clip'))
            g2_tok = jnp.concatenate([jnp.repeat(mod_p[5, :, 0, :], SEQ, axis=0),
                                      jnp.repeat(mod_s[5], DEC_SEQ, axis=0)], axis=0)
            x = x + g2_tok * f

    y_prompt, y_sample = final_norm(x, final_norm_w[None, :])
    y_prompt = y_prompt.reshape(BATCH, SEQ, D_MODEL)
    y_sample = y_sample.reshape(DEC_BATCH, DEC_SEQ, D_MODEL)
    sp_ = [jnp.stack(o) for o in st_p]
    ss_ = [jnp.stack(o) for o in st_s]
    return (y_prompt, y_sample, jnp.stack(ks_p), jnp.stack(vs_p), jnp.stack(ks_s), jnp.stack(vs_s),
            sp_[0], ss_[0], sp_[1], ss_[1], sp_[2], ss_[2], sp_[3], ss_[3], sp_[4], ss_[4])
```

```python
import functools
import math

import jax
import jax.numpy as jnp
from jax import lax
from jax.experimental import pallas as pl
from jax.experimental.pallas import tpu as pltpu

D_MODEL = 2048
BATCH = 8
SEQ = 2048
DEPTH = 2
DEC_BATCH = 128
DEC_SEQ = 8
W_GROUP = D_MODEL // 4
W_A = W_B = W_C = W_D = W_GROUP
H_A = 4
DH_A = W_A // (2 * H_A)
DV_A = 2 * DH_A
Q_BLOCK = 128
CONV_B = 3
H_C = 4
DK_C = W_C // H_C
DV_C = W_C // H_C
CONV_C = 4
DN_CHUNK = 64
H_D = 4
BW_D = W_D // H_D
CONV_D = 4
RG_C = 8.0
D_FF = 5632
N_EXPERTS = 8
TOP_K = 2
D_FF_EXPERT = 7168
N_MOD = 6
N_MIX = 4
EPS = 1e-6
N_A = 3 * W_A
N_B = 3 * W_B
N_C = 4 * W_C + 2 * H_C
N_D = 2 * W_D

N_PROMPT = BATCH * SEQ
N_SAMPLE = DEC_BATCH * DEC_SEQ
N_TOK = N_PROMPT + N_SAMPLE
SUBLANES = 8
LANES = 128

OFF_A = 0
OFF_B = N_A
OFF_C = N_A + N_B
OFF_D = OFF_C + 4 * W_C
N_PROJ = OFF_D + N_D

VMEM_LIMIT = 48 * 1024 * 1024
MOE_TM = 1024
MOE_TN = 512
MOE_TK = 1024
MOE_CAST_ROWS = 224
MOE_TILES = (TOP_K * N_TOK) // MOE_TM + N_EXPERTS
MOE_ROWS = MOE_TILES * MOE_TM
ROW_TILE = 512
ATT_TQ = 512
ATT_HPB = 2
LRU_CHUNK = 256
LRU_SPS = 16
DN_SPS = 8
DN_SUB_PROMPT = 4

f32 = jnp.float32
bf16 = jnp.bfloat16


def _params(sem):
    return pltpu.CompilerParams(dimension_semantics=sem, vmem_limit_bytes=VMEM_LIMIT)


def _silu(x):
    return x * (1.0 / (1.0 + jnp.exp(-x)))


def _mod_kernel(c_ref, w_ref, b_ref, o_ref):
    c = _silu(c_ref[...]).astype(bf16)
    o_ref[...] = jnp.dot(c, w_ref[0].astype(bf16), preferred_element_type=f32) + b_ref[0]


def modulation(c_all, w, b, layer):
    r = c_all.shape[0]
    n = w.shape[2]
    tn = 1024
    return pl.pallas_call(
        _mod_kernel,
        grid=(n // tn,),
        in_specs=[pl.BlockSpec((r, D_MODEL), lambda j: (0, 0)),
                  pl.BlockSpec((1, D_MODEL, tn), lambda j: (layer, 0, j)),
                  pl.BlockSpec((1, 1, tn), lambda j: (layer, 0, j))],
        out_specs=pl.BlockSpec((r, tn), lambda j: (0, j)),
        out_shape=jax.ShapeDtypeStruct((r, n), f32),
        compiler_params=_params(("arbitrary",)),
        name="modulation",
    )(c_all, w, b)


def _mod_specs(idx, tile_rows, col_block):
    tiles_per_req = SEQ // tile_rows
    npt = N_PROMPT // tile_rows

    def p_map(*ids):
        return (idx, jnp.minimum(ids[0] // tiles_per_req, BATCH - 1), 0, col_block(*ids))

    def s_map(*ids):
        return (idx, jnp.maximum(ids[0] - npt, 0), col_block(*ids))

    return p_map, s_map


def _row_pair(x, tile_rows, cols, col_block):
    npt = N_PROMPT // tile_rows
    if isinstance(x, tuple):
        xp, xs = x
        s_map = lambda *ids: (jnp.maximum(ids[0] - npt, 0), col_block(*ids))
    else:
        xp = xs = x
        s_map = lambda *ids: (jnp.maximum(ids[0], npt), col_block(*ids))
    p_map = lambda *ids: (jnp.minimum(ids[0], npt - 1), col_block(*ids))
    return (xp, xs), [pl.BlockSpec((tile_rows, cols), p_map), pl.BlockSpec((tile_rows, cols), s_map)]


def _apply_rows(vals, i, tile_rows, fn_prompt, fn_sample):
    npt = N_PROMPT // tile_rows

    @pl.when(i < npt)
    def _():
        fn_prompt(*vals)

    @pl.when(i >= npt)
    def _():
        fn_sample(*vals)


def _bcast_groups(a, m):
    rows, c = a.shape
    return a.reshape(rows // SUBLANES, SUBLANES, c), m[:, None, :]


def _norm_mod_emit(xp_ref, xs_ref, nw_ref, scp_ref, shp_ref, scs_ref, shs_ref, emit):
    tr = xp_ref.shape[0]

    def normed(x):
        return x * lax.rsqrt(jnp.mean(x * x, axis=-1, keepdims=True) + EPS) * nw_ref[...]

    def prompt():
        emit(normed(xp_ref[...]) * (1.0 + scp_ref[0, 0]) + shp_ref[0, 0])

    def sample():
        y3, sc = _bcast_groups(normed(xs_ref[...]), scs_ref[0])
        emit((y3 * (1.0 + sc) + shs_ref[0][:, None, :]).reshape(tr, D_MODEL))

    _apply_rows((), pl.program_id(0), tr, prompt, sample)


def _norm_mod_kernel(xp_ref, xs_ref, nw_ref, scp_ref, shp_ref, scs_ref, shs_ref, o_ref):
    def emit(hn):
        o_ref[...] = hn.astype(o_ref.dtype)

    _norm_mod_emit(xp_ref, xs_ref, nw_ref, scp_ref, shp_ref, scs_ref, shs_ref, emit)


def _norm_mod_router_kernel(xp_ref, xs_ref, nw_ref, scp_ref, shp_ref, scs_ref, shs_ref, r_ref, o_ref, lg_ref):
    def emit(hn):
        bits = lax.bitcast_convert_type(hn.astype(bf16).astype(f32), jnp.uint32)
        half = D_MODEL // 2
        o_ref[...] = (bits[:, :half] >> 16) | bits[:, half:]
        lg_ref[...] = jnp.dot(hn, r_ref[...], preferred_element_type=f32, precision=lax.Precision.HIGHEST)

    _norm_mod_emit(xp_ref, xs_ref, nw_ref, scp_ref, shp_ref, scs_ref, shs_ref, emit)


def norm_mod(x, nw, mod_p, mod_s, i_scale, i_shift, router=None):
    tr = ROW_TILE
    tg = tr // SUBLANES
    zero = lambda i: 0
    scp_map, scs_map = _mod_specs(i_scale, tr, zero)
    shp_map, shs_map = _mod_specs(i_shift, tr, zero)
    x_ops, x_specs = _row_pair(x, tr, D_MODEL, zero)
    in_specs = x_specs + [
                pl.BlockSpec((1, D_MODEL), lambda i: (0, 0)),
                pl.BlockSpec((1, 1, 1, D_MODEL), scp_map), pl.BlockSpec((1, 1, 1, D_MODEL), shp_map),
                pl.BlockSpec((1, tg, D_MODEL), scs_map), pl.BlockSpec((1, tg, D_MODEL), shs_map)]
    out_spec = pl.BlockSpec((tr, D_MODEL), lambda i: (i, 0))
    out_shape = jax.ShapeDtypeStruct((N_TOK, D_MODEL), bf16)
    if router is not None:
        out_spec = pl.BlockSpec((tr, D_MODEL // 2), lambda i: (i, 0))
        out_shape = jax.ShapeDtypeStruct((N_TOK, D_MODEL // 2), jnp.uint32)
    if router is None:
        return pl.pallas_call(
            _norm_mod_kernel, grid=(N_TOK // tr,), in_specs=in_specs, out_specs=out_spec,
            out_shape=out_shape, compiler_params=_params(("arbitrary",)), name="norm_mod",
        )(*x_ops, nw, mod_p, mod_p, mod_s, mod_s)
    return pl.pallas_call(
        _norm_mod_router_kernel, grid=(N_TOK // tr,),
        in_specs=in_specs + [pl.BlockSpec((D_MODEL, LANES), lambda i: (0, 0))],
        out_specs=[out_spec, pl.BlockSpec((tr, LANES), lambda i: (i, 0))],
        out_shape=[out_shape, jax.ShapeDtypeStruct((N_TOK, LANES), f32)],
        compiler_params=_params(("arbitrary",)), name="norm_mod_router",
    )(*x_ops, nw, mod_p, mod_p, mod_s, mod_s, router)


def _final_norm_kernel(x_ref, nw_ref, op_ref, os_ref, *, n_prompt_tiles):
    x = x_ref[...]
    y = x * lax.rsqrt(jnp.mean(x * x, axis=-1, keepdims=True) + EPS) * nw_ref[...]
    i = pl.program_id(0)

    @pl.when(i < n_prompt_tiles)
    def _():
        op_ref[...] = y

    @pl.when(i >= n_prompt_tiles)
    def _():
        os_ref[...] = y


def final_norm(x, nw):
    tr = ROW_TILE
    npt = N_PROMPT // tr
    return pl.pallas_call(
        functools.partial(_final_norm_kernel, n_prompt_tiles=npt), grid=(N_TOK // tr,),
        in_specs=[pl.BlockSpec((tr, D_MODEL), lambda i: (i, 0)),
                  pl.BlockSpec((1, D_MODEL), lambda i: (0, 0))],
        out_specs=[pl.BlockSpec((tr, D_MODEL), lambda i: (jnp.minimum(i, npt - 1), 0)),
                   pl.BlockSpec((tr, D_MODEL), lambda i: (jnp.maximum(i - npt, 0), 0))],
        out_shape=[jax.ShapeDtypeStruct((N_PROMPT, D_MODEL), f32), jax.ShapeDtypeStruct((N_SAMPLE, D_MODEL), f32)],
        compiler_params=_params(("arbitrary",)), name="final_norm",
    )(x, nw)


def _in_proj_kernel(x_ref, w_ref, wab_ref, o_ref, oab_ref):
    x = x_ref[...]
    o_ref[...] = jnp.dot(x, w_ref[...], preferred_element_type=f32)

    @pl.when(pl.program_id(1) == 0)
    def _():
        oab_ref[...] = jnp.dot(x, wab_ref[...], preferred_element_type=f32)


def in_proj(x, w, w_ab, tm, tn):
    m, k = x.shape
    n = w.shape[1]
    return pl.pallas_call(
        _in_proj_kernel, grid=(m // tm, n // tn),
        in_specs=[pl.BlockSpec((tm, k), lambda i, j: (i, 0)),
                  pl.BlockSpec((k, tn), lambda i, j: (0, j)),
                  pl.BlockSpec((k, LANES), lambda i, j: (0, 0))],
        out_specs=[pl.BlockSpec((tm, tn), lambda i, j: (i, j)), pl.BlockSpec((tm, LANES), lambda i, j: (i, 0))],
        out_shape=[jax.ShapeDtypeStruct((m, n), f32), jax.ShapeDtypeStruct((m, LANES), f32)],
        compiler_params=_params(("arbitrary", "arbitrary")), name="in_proj",
    )(x, w, w_ab)


def _gated_residual_store(acc, xp_ref, xs_ref, gp_ref, gs_ref, o_ref):
    tm, tn = acc.shape

    def prompt(acc):
        o_ref[...] = xp_ref[...] + acc * gp_ref[0, 0]

    def sample(acc):
        a3, g = _bcast_groups(acc, gs_ref[0])
        o_ref[...] = xs_ref[...] + (a3 * g).reshape(tm, tn)

    _apply_rows((acc,), pl.program_id(0), tm, prompt, sample)


def _mm_resid_kernel(y_ref, w_ref, xp_ref, xs_ref, gp_ref, gs_ref, o_ref, acc_ref, *, nk):
    k = pl.program_id(2)
    part = jnp.dot(y_ref[...], w_ref[...], preferred_element_type=f32)

    @pl.when(k == 0)
    def _():
        acc_ref[...] = part

    @pl.when(k > 0)
    def _():
        acc_ref[...] += part

    @pl.when(k == nk - 1)
    def _():
        _gated_residual_store(acc_ref[...], xp_ref, xs_ref, gp_ref, gs_ref, o_ref)


def matmul_gated_residual(y, w, x, mod_p, mod_s, i_gate, tm, tn, tk):
    m, kdim = y.shape
    n = w.shape[1]
    nk = kdim // tk
    gp_map, gs_map = _mod_specs(i_gate, tm, lambda i, j, k: j)
    x_ops, x_specs = _row_pair(x, tm, tn, lambda i, j, k: j)
    return pl.pallas_call(
        functools.partial(_mm_resid_kernel, nk=nk), grid=(m // tm, n // tn, nk),
        in_specs=[pl.BlockSpec((tm, tk), lambda i, j, k: (i, k)),
                  pl.BlockSpec((tk, tn), lambda i, j, k: (k, j)),
                  *x_specs,
                  pl.BlockSpec((1, 1, 1, tn), gp_map),
                  pl.BlockSpec((1, tm // SUBLANES, tn), gs_map)],
        out_specs=pl.BlockSpec((tm, tn), lambda i, j, k: (i, j)),
        out_shape=jax.ShapeDtypeStruct((m, n), f32),
        scratch_shapes=[pltpu.VMEM((tm, tn), f32)],
        compiler_params=_params(("arbitrary", "arbitrary", "arbitrary")), name="matmul_gated_residual",
    )(y, w, *x_ops, mod_p, mod_s)


def _mix_out_kernel(*refs):
    y_refs, (w_ref, xp_ref, xs_ref, gp_ref, gs_ref, o_ref) = refs[:2 * N_MIX], refs[2 * N_MIX:]
    tm = o_ref.shape[0]

    def project(group_refs):
        acc = jnp.dot(group_refs[0][...], w_ref[0:W_GROUP, :], preferred_element_type=f32)
        for gi in range(1, N_MIX):
            acc = acc + jnp.dot(group_refs[gi][...], w_ref[gi * W_GROUP:(gi + 1) * W_GROUP, :],
                                preferred_element_type=f32)
        return acc

    def prompt():
        o_ref[...] = xp_ref[...] + project(y_refs[0::2]) * gp_ref[0, 0]

    def sample():
        a3, g = _bcast_groups(project(y_refs[1::2]), gs_ref[0])
        o_ref[...] = xs_ref[...] + (a3 * g).reshape(o_ref.shape)

    _apply_rows((), pl.program_id(0), tm, prompt, sample)


def mix_out_residual(ys, w, x, mod_p, mod_s, i_gate, tm, tn):
    m, n = N_TOK, D_MODEL
    gp_map, gs_map = _mod_specs(i_gate, tm, lambda i, j: j)
    x_ops, x_specs = _row_pair(x, tm, tn, lambda i, j: j)
    y_ops, y_specs = [], []
    for pair in ys:
        ops, specs = _row_pair(pair, tm, W_GROUP, lambda i, j: 0)
        y_ops += ops
        y_specs += specs
    return pl.pallas_call(
        _mix_out_kernel, grid=(m // tm, n // tn),
        in_specs=[*y_specs,
                  pl.BlockSpec((D_MODEL, tn), lambda i, j: (0, j)),
                  *x_specs,
                  pl.BlockSpec((1, 1, 1, tn), gp_map),
                  pl.BlockSpec((1, tm // SUBLANES, tn), gs_map)],
        out_specs=pl.BlockSpec((tm, tn), lambda i, j: (i, j)),
        out_shape=jax.ShapeDtypeStruct((m, n), f32),
        compiler_params=_params(("arbitrary", "arbitrary")), name="mix_out_residual",
    )(*y_ops, w, *x_ops, mod_p, mod_s)


def _swiglu_body(x, wg, wu):
    g = jnp.dot(x, wg, preferred_element_type=f32)
    u = jnp.dot(x, wu, preferred_element_type=f32)
    return _silu(g) * u


def _swiglu_kernel(x_ref, wg_ref, wu_ref, o_ref):
    o_ref[...] = _swiglu_body(x_ref[...], wg_ref[...], wu_ref[...]).astype(o_ref.dtype)


def swiglu_up(x, wg, wu, tm, tn):
    m, k = x.shape
    n = wg.shape[1]
    return pl.pallas_call(
        _swiglu_kernel, grid=(m // tm, n // tn),
        in_specs=[pl.BlockSpec((tm, k), lambda i, j: (i, 0)),
                  pl.BlockSpec((k, tn), lambda i, j: (0, j)),
                  pl.BlockSpec((k, tn), lambda i, j: (0, j))],
        out_specs=pl.BlockSpec((tm, tn), lambda i, j: (i, j)),
        out_shape=jax.ShapeDtypeStruct((m, n), bf16),
        compiler_params=_params(("arbitrary", "arbitrary")), name="swiglu_up",
    )(x, wg, wu)


def _unpack_tokens(words):
    lo = lax.bitcast_convert_type(words << 16, f32)
    hi = lax.bitcast_convert_type(words & jnp.uint32(0xFFFF0000), f32)
    return jnp.concatenate([lo, hi], axis=1).astype(bf16)


def _moe_up_kernel(m_ref, n_ref, wn_ref, e_ref, v_ref, x_ref, wg_ref, wu_ref, wd_ref, o_ref, wdb_ref, wg_sc, wu_sc,
                   *, n_cast):
    i = pl.program_id(0)

    @pl.when(i < n_cast)
    def _():
        wdb_ref[...] = wd_ref[...].astype(bf16)

    @pl.when(v_ref[i] == 2)
    def _():
        wg_sc[...] = wg_ref[0].astype(bf16)
        wu_sc[...] = wu_ref[0].astype(bf16)

    @pl.when(v_ref[i] > 0)
    def _():
        o_ref[...] = _swiglu_body(_unpack_tokens(x_ref[...]), wg_sc[...], wu_sc[...]).astype(o_ref.dtype)

    @pl.when(v_ref[i] == 0)
    def _():
        o_ref[...] = jnp.zeros(o_ref.shape, o_ref.dtype)


def moe_up(item_m, item_n, item_wn, item_e, item_v, xs, wg, wu, wd):
    ff = wg.shape[2]
    nt = ff // MOE_TN
    n_items = MOE_TILES * nt
    wd_rows = wd.shape[0] * wd.shape[1]
    n_cast = wd_rows // MOE_CAST_ROWS
    assert n_cast <= (TOP_K * N_TOK // MOE_TM) * nt and wd_rows % MOE_CAST_ROWS == 0
    cast_spec = pl.BlockSpec((MOE_CAST_ROWS, D_MODEL), lambda i, m, n, wn, e, v: (jnp.minimum(i, n_cast - 1), 0))
    grid_spec = pltpu.PrefetchScalarGridSpec(
        num_scalar_prefetch=5, grid=(n_items,),
        in_specs=[pl.BlockSpec((MOE_TM, D_MODEL // 2), lambda i, m, n, wn, e, v: (m[i], 0)),
                  pl.BlockSpec((1, D_MODEL, MOE_TN), lambda i, m, n, wn, e, v: (e[i], 0, wn[i])),
                  pl.BlockSpec((1, D_MODEL, MOE_TN), lambda i, m, n, wn, e, v: (e[i], 0, wn[i])),
                  cast_spec],
        out_specs=[pl.BlockSpec((MOE_TM, MOE_TN), lambda i, m, n, wn, e, v: (m[i], n[i])), cast_spec],
        scratch_shapes=[pltpu.VMEM((D_MODEL, MOE_TN), bf16), pltpu.VMEM((D_MODEL, MOE_TN), bf16)])
    h, wd_bf = pl.pallas_call(
        functools.partial(_moe_up_kernel, n_cast=n_cast), grid_spec=grid_spec,
        out_shape=[jax.ShapeDtypeStruct((MOE_ROWS, ff), bf16), jax.ShapeDtypeStruct((wd_rows, D_MODEL), bf16)],
        compiler_params=_params(("arbitrary",)), name="moe_up",
    )(item_m, item_n, item_wn, item_e, item_v, xs, wg, wu, wd.reshape(wd_rows, D_MODEL))
    return h, wd_bf.reshape(wd.shape)


def _moe_down_kernel(te_ref, nu_ref, h_ref, wd_ref, o_ref):
    m = pl.program_id(0)
    k = pl.program_id(1)

    @pl.when(m < nu_ref[0])
    def _():
        part = jnp.dot(h_ref[...], wd_ref[0], preferred_element_type=f32)

        @pl.when(k == 0)
        def _():
            o_ref[...] = part

        @pl.when(k > 0)
        def _():
            o_ref[...] += part

    @pl.when(jnp.logical_and(m >= nu_ref[0], k == 0))
    def _():
        o_ref[...] = jnp.zeros(o_ref.shape, o_ref.dtype)


def moe_down(tile_e, n_used, h, wd):
    nk = wd.shape[1] // MOE_TK

    def mclamp(m, nu):
        return jnp.minimum(m, nu[0] - 1)

    def kclamp(m, k, nu):
        return jnp.where(m < nu[0], k, nk - 1)

    grid_spec = pltpu.PrefetchScalarGridSpec(
        num_scalar_prefetch=2, grid=(MOE_TILES, nk),
        in_specs=[pl.BlockSpec((MOE_TM, MOE_TK), lambda m, k, te, nu: (mclamp(m, nu), kclamp(m, k, nu))),
                  pl.BlockSpec((1, MOE_TK, D_MODEL),
                               lambda m, k, te, nu: (te[mclamp(m, nu)], kclamp(m, k, nu), 0))],
        out_specs=pl.BlockSpec((MOE_TM, D_MODEL), lambda m, k, te, nu: (m, 0)))
    return pl.pallas_call(
        _moe_down_kernel, grid_spec=grid_spec,
        out_shape=jax.ShapeDtypeStruct((MOE_ROWS, D_MODEL), f32),
        compiler_params=_params(("arbitrary", "arbitrary")), name="moe_down",
    )(tile_e, n_used, h, wd)


def moe_route(logits, n_ff_tiles):
    top_v, top_i = lax.top_k(logits, TOP_K)
    gates = jax.nn.softmax(top_v, axis=-1)
    e_flat = top_i.reshape(-1).astype(jnp.int32)
    onehot = (e_flat[:, None] == jnp.arange(N_EXPERTS, dtype=jnp.int32)[None, :]).astype(jnp.int32)
    csum = jnp.cumsum(onehot, axis=0)
    counts = csum[-1]
    rank = jnp.sum((csum - onehot) * onehot, axis=-1)
    tiles_e = (counts + MOE_TM - 1) // MOE_TM
    tile_end = jnp.cumsum(tiles_e)
    tile_start = tile_end - tiles_e
    dest = tile_start[e_flat] * MOE_TM + rank
    tok = jnp.arange(TOP_K * N_TOK, dtype=jnp.int32) // TOP_K
    row_src = jnp.zeros((MOE_ROWS,), jnp.int32).at[dest].set(tok)
    n_used = tile_end[-1]
    t_ids = jnp.arange(MOE_TILES, dtype=jnp.int32)
    tile_e = jnp.minimum(jnp.searchsorted(tile_end, t_ids, side="right"), N_EXPERTS - 1).astype(jnp.int32)
    nt = n_ff_tiles
    items_e = tiles_e * nt
    item_end = jnp.cumsum(items_e)
    item_start = item_end - items_e
    total = item_end[-1]
    idx = jnp.arange(MOE_TILES * nt, dtype=jnp.int32)
    valid = idx < total
    idc = jnp.minimum(idx, total - 1)
    ie = jnp.minimum(jnp.searchsorted(item_end, idc, side="right"), N_EXPERTS - 1).astype(jnp.int32)
    local = idc - item_start[ie]
    te = jnp.maximum(tiles_e[ie], 1)
    item_wn = (local // te).astype(jnp.int32)
    spare = idx - total
    item_n = jnp.where(valid, item_wn, spare % nt).astype(jnp.int32)
    item_m = jnp.where(valid, tile_start[ie] + local % te, n_used + spare // nt).astype(jnp.int32)
    return dict(gates=gates, dest=dest.reshape(N_TOK, TOP_K), row_src=row_src, tile_e=tile_e,
                n_used=n_used.reshape(1).astype(jnp.int32), item_m=item_m, item_n=item_n, item_wn=item_wn, item_e=ie,
                item_v=jnp.where(valid, jnp.where(local % te == 0, 2, 1), 0).astype(jnp.int32))


def _smem_spec():
    return pl.BlockSpec(memory_space=pltpu.SMEM)


def _subln(o, w, gain):
    return o * lax.rsqrt(jnp.mean(o * o, axis=-1, keepdims=True) + EPS) * w * gain


def _attn_prompt_kernel(lam_ref, slope_ref, q_ref, k_ref, v_ref, w_ref, o_ref, m_sc, acc_sc, *, tq, hpb, gain):
    hb = pl.program_id(1)
    qi = pl.program_id(2)
    lane = lax.broadcasted_iota(jnp.int32, (tq, DV_A), 1)
    row = lax.broadcasted_iota(jnp.int32, (2 * tq, tq), 0)
    col = lax.broadcasted_iota(jnp.int32, (2 * tq, tq), 1)
    rel = (col - jnp.where(row >= tq, row - tq, row)).astype(f32)
    ones_v = jnp.ones((tq, DV_A), bf16)
    qss, slopes, biases = [], [], []
    for hh in range(hpb):
        q = q_ref[:, hh * DV_A:(hh + 1) * DV_A] * (DH_A ** -0.5)
        qss.append(jnp.concatenate([jnp.where(lane < DH_A, q, 0.0), jnp.where(lane >= DH_A, q, 0.0)],
                                   axis=0).astype(bf16))
        slopes.append(slope_ref[hb * hpb + hh])
        biases.append(slopes[hh] * rel)
    m_sc[...] = jnp.full(m_sc.shape, -jnp.inf, f32)
    acc_sc[...] = jnp.zeros(acc_sc.shape, f32)

    def chunk_step(j, diagonal):
        off = pl.multiple_of(j * tq, tq)
        blocks = (jnp.zeros((1, 1), jnp.int32) + (qi - j) * tq).astype(f32)
        for hh in range(hpb):
            cs = slice(hh * DV_A, (hh + 1) * DV_A)
            kc = k_ref[pl.ds(off, tq), cs].astype(bf16)
            vc = jnp.concatenate([v_ref[pl.ds(off, tq), cs].astype(bf16), ones_v], axis=1)
            s = lax.dot_general(qss[hh], kc, (((1,), (1,)), ((), ())), preferred_element_type=f32)
            s = s + biases[hh] - slopes[hh] * blocks
            if diagonal:
                s = jnp.where(rel <= 0, s, -1e30)
            m_prev = m_sc[hh]
            m_new = jnp.maximum(m_prev, jnp.max(s, axis=-1, keepdims=True))
            pr = jnp.exp(s - m_new).astype(bf16)
            acc_sc[hh] = jnp.exp(m_prev - m_new) * acc_sc[hh] + jnp.dot(pr, vc, preferred_element_type=f32)
            m_sc[hh] = m_new

    def body(j, carry):
        chunk_step(j, False)
        return carry

    lax.fori_loop(0, qi, body, 0)
    chunk_step(qi, True)
    for hh in range(hpb):
        acc = acc_sc[hh]
        o = acc[:, :DV_A] / acc[:, DV_A:]
        o = o[:tq] - lam_ref[0] * o[tq:]
        o_ref[:, hh * DV_A:(hh + 1) * DV_A] = _subln(o, w_ref[...], gain).astype(o_ref.dtype)


def attn_prompt(proj, lam, slopes, subln_w, gain, batch, seq, tq, hpb):
    nq = seq // tq
    wb = hpb * DV_A
    nhb = H_A // hpb
    return pl.pallas_call(
        functools.partial(_attn_prompt_kernel, tq=tq, hpb=hpb, gain=gain),
        grid=(batch, nhb, nq),
        in_specs=[_smem_spec(), _smem_spec(),
                  pl.BlockSpec((tq, wb), lambda b, h, qi: (b * nq + qi, h)),
                  pl.BlockSpec((seq, wb), lambda b, h, qi: (b, nhb + h)),
                  pl.BlockSpec((seq, wb), lambda b, h, qi: (b, 2 * nhb + h)),
                  pl.BlockSpec((1, DV_A), lambda b, h, qi: (0, 0))],
        out_specs=pl.BlockSpec((tq, wb), lambda b, h, qi: (b * nq + qi, h)),
        out_shape=jax.ShapeDtypeStruct((batch * seq, W_A), bf16),
        scratch_shapes=[pltpu.VMEM((hpb, 2 * tq, 1), f32), pltpu.VMEM((hpb, 2 * tq, 2 * DV_A), f32)],
        compiler_params=_params(("arbitrary", "arbitrary", "arbitrary")), name="attn_prompt",
    )(lam, slopes, proj, proj, proj, subln_w)


def _attn_sample_kernel(pt_ref, lam_ref, slope_ref, q_ref, kn_ref, vn_ref, w_ref, *rest, t, n_pages, page, gain):
    k_refs = rest[:n_pages]
    v_refs = rest[n_pages:2 * n_pages]
    o_ref = rest[2 * n_pages]
    s_sc = rest[2 * n_pages + 1]
    n_rep = 2 * H_A
    rows = n_rep * t
    n_past = n_pages * page
    q = q_ref[...] * (DH_A ** -0.5)
    q_rep = jnp.concatenate([q] * n_rep, axis=0)
    r_grp = lax.broadcasted_iota(jnp.int32, (rows, W_A), 0) // t
    c_grp = lax.broadcasted_iota(jnp.int32, (rows, W_A), 1) // DH_A
    qbd = jnp.where(r_grp == c_grp, q_rep, 0.0).astype(bf16)
    for j in range(n_pages):
        s_sc[:, j * page:(j + 1) * page] = jnp.dot(qbd, k_refs[j][0, 0].astype(bf16), preferred_element_type=f32)
    s_new = lax.dot_general(qbd, kn_ref[...].astype(bf16), (((1,), (1,)), ((), ())), preferred_element_type=f32)

    row1 = lax.broadcasted_iota(jnp.int32, (rows, 1), 0)
    head1 = row1 // (2 * t)
    slope = jnp.where(head1 == 0, slope_ref[0],
                      jnp.where(head1 == 1, slope_ref[1], jnp.where(head1 == 2, slope_ref[2], slope_ref[3])))
    t_row = (row1 % t)
    kpos = lax.broadcasted_iota(jnp.int32, (rows, n_past), 1)
    s_past = s_sc[...] - slope * (n_past + t_row - kpos).astype(f32)
    dist_new = (t_row - lax.broadcasted_iota(jnp.int32, (rows, t), 1)).astype(f32)
    s_new = jnp.where(dist_new >= 0, s_new - slope * dist_new, -1e30)
    m = jnp.maximum(jnp.max(s_past, axis=-1, keepdims=True), jnp.max(s_new, axis=-1, keepdims=True))
    p_past = jnp.exp(s_past - m)
    p_new = jnp.exp(s_new - m)
    denom = jnp.sum(p_past, axis=-1, keepdims=True) + jnp.sum(p_new, axis=-1, keepdims=True)
    p_past = p_past.astype(bf16)
    p_new = p_new.astype(bf16)
    lam = lam_ref[0]
    w = w_ref[...]
    for h in range(H_A):
        vh = jnp.concatenate([v_refs[j][0, 0, pl.ds(h, page, stride=H_A), :] for j in range(n_pages)], axis=0)
        rs = slice(h * 2 * t, (h + 1) * 2 * t)
        oh = jnp.dot(p_past[rs], vh.astype(bf16), preferred_element_type=f32)
        oh = oh + jnp.dot(p_new[rs], vn_ref[:, h * DV_A:(h + 1) * DV_A].astype(bf16), preferred_element_type=f32)
        oh = oh / denom[rs]
        o = oh[:t] - lam * oh[t:]
        o_ref[:, h * DV_A:(h + 1) * DV_A] = _subln(o, w, gain).astype(o_ref.dtype)


def attn_sample(proj, ckt, cvr, layer, page_table, lam, slopes, subln_w, gain, row0, nseq, t):
    n_pages = page_table.shape[1]
    page = ckt.shape[-1]
    rb0 = row0 // t

    def kv_spec(j):
        return pl.BlockSpec((1, 1, W_A, page), lambda b, pt, j=j: (layer, pt[b * n_pages + j], 0, 0))

    in_specs = ([_smem_spec(), _smem_spec(),
                 pl.BlockSpec((t, W_A), lambda b, pt: (rb0 + b, 0)),
                 pl.BlockSpec((t, W_A), lambda b, pt: (rb0 + b, 1)),
                 pl.BlockSpec((t, W_A), lambda b, pt: (rb0 + b, 2)),
                 pl.BlockSpec((1, DV_A), lambda b, pt: (0, 0))]
                + [kv_spec(j) for j in range(n_pages)] + [kv_spec(j) for j in range(n_pages)])
    grid_spec = pltpu.PrefetchScalarGridSpec(
        num_scalar_prefetch=1, grid=(nseq,), in_specs=in_specs,
        out_specs=pl.BlockSpec((t, W_A), lambda b, pt: (b, 0)),
        scratch_shapes=[pltpu.VMEM((2 * H_A * t, n_pages * page), f32)])
    return pl.pallas_call(
        functools.partial(_attn_sample_kernel, t=t, n_pages=n_pages, page=page, gain=gain),
        grid_spec=grid_spec, out_shape=jax.ShapeDtypeStruct((nseq * t, W_A), bf16),
        compiler_params=_params(("arbitrary",)), name="attn_sample",
    )(page_table.reshape(-1), lam, slopes, proj, proj, proj, subln_w, *([ckt] * n_pages), *([cvr] * n_pages))


def _conv_step(f_sc, x3, w_ref, buf0_ref, buf_out, width, chunk, first):
    lo = SUBLANES - (width - 1)

    @pl.when(first)
    def _():
        f_sc[:, 0:SUBLANES, :] = jnp.zeros((f_sc.shape[0], SUBLANES, f_sc.shape[2]), f32)
        f_sc[:, lo:SUBLANES, :] = buf0_ref[...]

    f_sc[:, SUBLANES:SUBLANES + chunk, :] = x3
    out = w_ref[0:1, :][None] * f_sc[:, lo:lo + chunk, :]
    for j in range(1, width):
        out = out + w_ref[j:j + 1, :][None] * f_sc[:, lo + j:lo + j + chunk, :]
    buf_out[...] = f_sc[:, chunk + lo:chunk + SUBLANES, :]
    f_sc[:, 0:SUBLANES, :] = f_sc[:, chunk:chunk + SUBLANES, :]
    return out


def _sigmoid(x):
    return 1.0 / (1.0 + jnp.exp(-x))


def _gelu_tanh(x):
    return 0.5 * x * (1.0 + jnp.tanh(0.7978845608028654 * (x + 0.044715 * (x * x * x))))


def _conv_lru_kernel(pb_ref, pd_ref, cb0_ref, lb0_ref, h0_ref, wb_ref, wd_ref, bd_ref, wa_ref, ba_ref, wx_ref,
                     bx_ref, sp_ref, yb_ref, yd_ref, cb_out, lb_out, h_out, fb_sc, fd_sc, h_sc, *, sps, chunk):
    first = pl.program_id(1) == 0
    rows = sps * chunk
    pb = pb_ref[...]
    bg, cg, xt = pb[:, :W_B], pb[:, W_B:2 * W_B], pb[:, 2 * W_B:]
    conv_b = _conv_step(fb_sc, (cg * xt).reshape(sps, chunk, W_B), wb_ref, cb0_ref, cb_out, CONV_B, chunk, first)
    yb_ref[...] = (bg * conv_b.reshape(rows, W_B)).astype(yb_ref.dtype)

    pd = pd_ref[...]
    xd, gd = pd[:, :W_D], pd[:, W_D:]
    conv_d = _conv_step(fd_sc, xd.reshape(sps, chunk, W_D), wd_ref, lb0_ref, lb_out, CONV_D, chunk, first)
    xf = conv_d.reshape(rows, W_D) + bd_ref[...]
    xfb = xf.astype(bf16)
    ra = jnp.concatenate([jnp.dot(xfb[:, h * BW_D:(h + 1) * BW_D], wa_ref[h].astype(bf16),
                                  preferred_element_type=f32) for h in range(H_D)], axis=-1)
    rx = jnp.concatenate([jnp.dot(xfb[:, h * BW_D:(h + 1) * BW_D], wx_ref[h].astype(bf16),
                                  preferred_element_type=f32) for h in range(H_D)], axis=-1)
    r = _sigmoid(ra + ba_ref[...])
    ig = _sigmoid(rx + bx_ref[...])
    log_a = -RG_C * r * sp_ref[...]
    a = jnp.exp(log_a)
    th = jnp.tanh(log_a)
    bx = jnp.sqrt(-2.0 * th / (1.0 - th)) * (ig * xf)

    pos = lax.broadcasted_iota(jnp.int32, (rows, W_D), 0) % chunk
    step = 1
    while step < chunk:
        a_s = pltpu.roll(a, step, 0)
        b_s = pltpu.roll(bx, step, 0)
        live = pos >= step
        bx = jnp.where(live, a * b_s + bx, bx)
        a = jnp.where(live, a * a_s, a)
        step *= 2

    @pl.when(first)
    def _():
        h_sc[...] = h0_ref[...]

    h_in = jnp.broadcast_to(h_sc[...], (sps, chunk, W_D)).reshape(rows, W_D)
    hs = bx + a * h_in
    h_last = hs.reshape(sps, chunk, W_D)[:, chunk - 1:chunk, :]
    h_sc[...] = h_last
    h_out[...] = h_last
    yd_ref[...] = (hs * _gelu_tanh(gd)).astype(yd_ref.dtype)


def conv_lru(proj, row0, nseq, t, sps, chunk, conv_b_buf, lru_buf, lru_h, wb, wd, bd, wa, ba, wx, bx, sp):
    rows = sps * chunk
    nc = t // chunk
    rb0 = row0 // rows

    def rmap(s, c):
        return rb0 + s * nc + c

    def full(shape):
        return pl.BlockSpec(shape, lambda s, c: (0,) * len(shape))

    def per_seq(shape):
        return pl.BlockSpec((sps,) + shape, lambda s, c: (s,) + (0,) * len(shape))

    in_specs = [pl.BlockSpec((rows, N_B), lambda s, c: (rmap(s, c), OFF_B // N_B)),
                pl.BlockSpec((rows, N_D), lambda s, c: (rmap(s, c), OFF_D // N_D)),
                per_seq((CONV_B - 1, W_B)), per_seq((CONV_D - 1, W_D)), per_seq((1, W_D)),
                full((CONV_B, W_B)), full((CONV_D, W_D)), full((1, W_D)),
                full((H_D, BW_D, BW_D)), full((1, W_D)), full((H_D, BW_D, BW_D)), full((1, W_D)), full((1, W_D))]
    out_specs = [pl.BlockSpec((rows, W_B), lambda s, c: (s * nc + c, 0)),
                 pl.BlockSpec((rows, W_D), lambda s, c: (s * nc + c, 0)),
                 per_seq((CONV_B - 1, W_B)), per_seq((CONV_D - 1, W_D)), per_seq((1, W_D))]
    out_shape = [jax.ShapeDtypeStruct((nseq * t, W_B), bf16), jax.ShapeDtypeStruct((nseq * t, W_D), bf16),
                 jax.ShapeDtypeStruct((nseq, CONV_B - 1, W_B), f32),
                 jax.ShapeDtypeStruct((nseq, CONV_D - 1, W_D), f32),
                 jax.ShapeDtypeStruct((nseq, 1, W_D), f32)]
    return pl.pallas_call(
        functools.partial(_conv_lru_kernel, sps=sps, chunk=chunk),
        grid=(nseq // sps, nc), in_specs=in_specs, out_specs=out_specs, out_shape=out_shape,
        scratch_shapes=[pltpu.VMEM((sps, chunk + SUBLANES, W_B), f32), pltpu.VMEM((sps, chunk + SUBLANES, W_D), f32),
                        pltpu.VMEM((sps, 1, W_D), f32)],
        compiler_params=_params(("arbitrary", "arbitrary")), name="conv_lru",
    )(proj, proj, conv_b_buf, lru_buf, lru_h, wb, wd, bd, wa, ba, wx, bx, sp)


_HI = lax.Precision.HIGHEST


def _dot_nt(a, b, precision=None):
    return lax.dot_general(a, b, (((1,), (1,)), ((), ())), preferred_element_type=f32, precision=precision)


def _dot_tn(a, b):
    return lax.dot_general(a, b, (((0,), (0,)), ((), ())), preferred_element_type=f32)


def _hi_lo(x):
    hi = x.astype(bf16).astype(f32)
    return hi, x - hi


def _dot_f32(a, b, chunk):
    if chunk % 16:
        return jnp.dot(a, b, preferred_element_type=f32, precision=_HI)
    a_hi, a_lo = _hi_lo(a)
    b_hi, b_lo = _hi_lo(b)
    lhs = jnp.concatenate([a_hi, a_hi, a_lo], axis=1).astype(bf16)
    rhs = jnp.concatenate([b_hi, b_lo, b_hi], axis=0).astype(bf16)
    return jnp.dot(lhs, rhs, preferred_element_type=f32)


def _cumsum_rows(tri, x, chunk):
    if chunk % 16:
        return jnp.dot(tri, x, preferred_element_type=f32, precision=_HI)
    hi, r1 = _hi_lo(x)
    mid, lo = _hi_lo(r1)
    lhs = jnp.concatenate([tri, tri, tri], axis=1).astype(bf16)
    rhs = jnp.concatenate([hi, mid, lo], axis=0).astype(bf16)
    return jnp.dot(lhs, rhs, preferred_element_type=f32)


def _row_bcast(gc, pick, chunk):
    sel = pick.astype(f32)
    if chunk % 16:
        return _dot_nt(sel, gc, precision=_HI)
    hi, r1 = _hi_lo(gc)
    mid, lo = _hi_lo(r1)
    lhs = jnp.concatenate([sel, sel, sel], axis=1).astype(bf16)
    rhs = jnp.concatenate([hi, mid, lo], axis=1).astype(bf16)
    return _dot_nt(lhs, rhs)


def _deltanet_kernel(qkv_ref, z_ref, ab_ref, buf0_ref, s0_ref, wc_ref, nega_ref, dtb_ref, nw_ref,
                     y_ref, buf_out, s_out, f_sc, s_sc, *, sps, nsub, chunk):
    first = pl.program_id(1) == 0
    span_rows = nsub * chunk
    rows = sps * span_rows
    wq = 3 * W_C
    conv = _conv_step(f_sc, qkv_ref[...].reshape(sps, span_rows, wq), wc_ref, buf0_ref, buf_out, CONV_C,
                      span_rows, first)
    conv = conv.reshape(rows, wq)
    qkv = conv * _sigmoid(conv)

    @pl.when(first)
    def _():
        s_sc[...] = s0_ref[...]

    ab = ab_ref[...]
    xg = ab + dtb_ref[...]
    g_all = nega_ref[...] * (jnp.maximum(xg, 0.0) + jnp.log1p(jnp.exp(-jnp.abs(xg))))
    beta_all = _sigmoid(ab)
    z = z_ref[...]
    ri = lax.broadcasted_iota(jnp.int32, (chunk, chunk), 0)
    ci = lax.broadcasted_iota(jnp.int32, (chunk, chunk), 1)
    causal = ri >= ci
    strict = ri > ci
    tri = causal.astype(f32)
    eye = (ri == ci).astype(f32)
    lane = lax.broadcasted_iota(jnp.int32, (chunk, LANES), 1)
    nw = nw_ref[...]
    spans = [(i, c) for i in range(sps) for c in range(nsub)]
    units = [(i, c, h) for i, c in spans for h in range(H_C)]

    def rows_of(i, c):
        r0 = (i * nsub + c) * chunk
        return slice(r0, r0 + chunk)

    gcs = {sp: _cumsum_rows(tri, g_all[rows_of(*sp)], chunk) for sp in spans}

    qs, ks, gcols, rhss, qks, negs = [], [], [], [], [], []
    for i, c, h in units:
        rs = rows_of(i, c)
        gc = gcs[(i, c)]
        q = qkv[rs, h * DK_C:(h + 1) * DK_C]
        k = qkv[rs, W_C + h * DK_C:W_C + (h + 1) * DK_C]
        v = qkv[rs, 2 * W_C + h * DV_C:2 * W_C + (h + 1) * DV_C]
        q = q * lax.rsqrt(jnp.sum(q * q, axis=-1, keepdims=True) + EPS) * (DK_C ** -0.5)
        k = k * lax.rsqrt(jnp.sum(k * k, axis=-1, keepdims=True) + EPS)
        beta = beta_all[rs, H_C + h:H_C + h + 1]
        gcol = gc[:, h:h + 1]
        grow = _row_bcast(gc, lane == h, chunk)
        decay = jnp.where(causal, jnp.exp(jnp.where(causal, gcol - grow, 0.0)), 0.0)
        eg = jnp.exp(gcol)
        kb = k * beta
        negs.append(jnp.where(strict, _dot_nt(kb, k) * decay, 0.0) * -1.0)
        qks.append(_dot_nt(q, k) * decay)
        rhss.append(jnp.concatenate([v * beta, kb * eg], axis=1))
        qs.append(q * eg)
        ks.append(k)
        gcols.append(gcol)

    invs = [eye + n for n in negs]
    span = 2
    while span < chunk:
        negs = [_dot_f32(n, n, chunk) for n in negs]
        invs = [inv + _dot_f32(inv, n, chunk) for inv, n in zip(invs, negs)]
        span *= 2
    sols = [_dot_f32(inv, rhs, chunk) for inv, rhs in zip(invs, rhss)]

    states = {(i, h): s_sc[i, h] for i in range(sps) for h in range(H_C)}
    outs = {}
    for c in range(nsub):
        cur = [(idx, (i, h)) for idx, (i, cc, h) in enumerate(units) if cc == c]
        wq_s = {idx: jnp.dot(jnp.concatenate([sols[idx][:, DV_C:], qs[idx]], axis=0), states[key],
                             preferred_element_type=f32) for idx, key in cur}
        v_new = {idx: sols[idx][:, :DV_C] - wq_s[idx][:chunk] for idx, _ in cur}
        for idx, _ in cur:
            outs[idx] = wq_s[idx][chunk:] + jnp.dot(qks[idx], v_new[idx], preferred_element_type=f32)
        for idx, key in cur:
            gcol = gcols[idx]
            g_last = gcol[chunk - 1:chunk, :]
            states[key] = (states[key] * jnp.exp(g_last)
                           + _dot_tn(ks[idx] * jnp.exp(g_last - gcol), v_new[idx]))
    for (i, h), st in states.items():
        s_sc[i, h] = st
        s_out[i, h] = st
    for idx, (i, c, h) in enumerate(units):
        rs = rows_of(i, c)
        zh = z[rs, h * DV_C:(h + 1) * DV_C]
        o = outs[idx]
        o = o * lax.rsqrt(jnp.mean(o * o, axis=-1, keepdims=True) + EPS) * nw
        y_ref[rs, h * DV_C:(h + 1) * DV_C] = (o * (zh * _sigmoid(zh))).astype(y_ref.dtype)


def deltanet(proj, proj_ab, row0, nseq, t, sps, nsub, chunk, dn_buf, dn_s, wc, nega, dtb, nw):
    rows = sps * nsub * chunk
    nc = t // (nsub * chunk)
    rb0 = row0 // rows
    wq = 3 * W_C

    def rmap(s, c):
        return rb0 + s * nc + c

    def full(shape):
        return pl.BlockSpec(shape, lambda s, c: (0,) * len(shape))

    def per_seq(shape):
        return pl.BlockSpec((sps,) + shape, lambda s, c: (s,) + (0,) * len(shape))

    in_specs = [pl.BlockSpec((rows, wq), lambda s, c: (rmap(s, c), OFF_C // wq)),
                pl.BlockSpec((rows, W_C), lambda s, c: (rmap(s, c), (OFF_C + wq) // W_C)),
                pl.BlockSpec((rows, LANES), lambda s, c: (rmap(s, c), 0)),
                per_seq((CONV_C - 1, wq)), per_seq((H_C, DK_C, DV_C)),
                full((CONV_C, wq)), full((1, LANES)), full((1, LANES)), full((1, DV_C))]
    out_specs = [pl.BlockSpec((rows, W_C), lambda s, c: (s * nc + c, 0)),
                 per_seq((CONV_C - 1, wq)), per_seq((H_C, DK_C, DV_C))]
    out_shape = [jax.ShapeDtypeStruct((nseq * t, W_C), bf16),
                 jax.ShapeDtypeStruct((nseq, CONV_C - 1, wq), f32),
                 jax.ShapeDtypeStruct((nseq, H_C, DK_C, DV_C), f32)]
    return pl.pallas_call(
        functools.partial(_deltanet_kernel, sps=sps, nsub=nsub, chunk=chunk),
        grid=(nseq // sps, nc), in_specs=in_specs, out_specs=out_specs, out_shape=out_shape,
        scratch_shapes=[pltpu.VMEM((sps, nsub * chunk + SUBLANES, wq), f32),
                        pltpu.VMEM((sps, H_C, DK_C, DV_C), f32)],
        compiler_params=_params(("arbitrary", "arbitrary")), name="deltanet",
    )(proj, proj, proj_ab, dn_buf, dn_s, wc, nega, dtb, nw)


def _lane_row(v):
    return jnp.concatenate([v.astype(f32), jnp.zeros((LANES - v.shape[0],), f32)])[None, :]


def kernel(x_prompt, x_sample, cache_k, cache_v, state_conv_b, state_dn_conv, state_dn, state_lru_conv, state_lru_h, page_table, c_prompt, c_sample, w_mod, b_mod, norm_mix, norm_ffn, w_in, w_out, lam_q1, lam_k1, lam_q2, lam_k2, subln_w, conv_b_w, dn_conv_w, dn_a_log, dn_dt_bias, dn_norm_w, lru_conv_w, lru_conv_b, lru_wa, lru_ba, lru_wx, lru_bx, lru_lambda, ffn_w_gate, ffn_w_up, ffn_w_down, moe_router, moe_w_gate, moe_w_up, moe_w_down, final_norm_w):
    x = (x_prompt.reshape(N_PROMPT, D_MODEL), x_sample.reshape(N_SAMPLE, D_MODEL))
    c_all = jnp.concatenate([c_prompt, c_sample], axis=0)
    n_pool, page = cache_k.shape[1], cache_k.shape[2]
    ckt = jnp.transpose(cache_k, (0, 1, 3, 4, 5, 2)).reshape(DEPTH, n_pool, W_A, page)
    cvr = cache_v.reshape(DEPTH, n_pool, page * H_A, DV_A)
    slopes = jnp.exp2(-8.0 * jnp.arange(1, H_A + 1, dtype=f32) / H_A)

    zero_cb = jnp.zeros((BATCH, CONV_B - 1, W_B), f32)
    zero_dc = jnp.zeros((BATCH, CONV_C - 1, 3 * W_C), f32)
    zero_ds = jnp.zeros((BATCH, H_C, DK_C, DV_C), f32)
    zero_lc = jnp.zeros((BATCH, CONV_D - 1, W_D), f32)
    zero_lh = jnp.zeros((BATCH, 1, W_D), f32)

    ks_p, vs_p, ks_s, vs_s = [], [], [], []
    st_p = [[] for _ in range(5)]
    st_s = [[] for _ in range(5)]
    for l in range(DEPTH):
        mod = modulation(c_all, w_mod, b_mod[:, None, :], l)
        mod_p = jnp.transpose(mod[:BATCH].reshape(BATCH, N_MOD, D_MODEL), (1, 0, 2))[:, :, None, :]
        mod_s = jnp.transpose(mod[BATCH:].reshape(DEC_BATCH, N_MOD, D_MODEL), (1, 0, 2))

        hn = norm_mod(x, norm_mix[l][None, :], mod_p, mod_s, 1, 0)
        wl = w_in[l]
        n_cqkvz = 4 * W_C
        w_pack = jnp.concatenate([wl[:, :N_A + N_B + n_cqkvz], wl[:, N_A + N_B + N_C:]], axis=1).astype(bf16)
        w_ab = jnp.concatenate([wl[:, N_A + N_B + n_cqkvz:N_A + N_B + N_C],
                                jnp.zeros((D_MODEL, LANES - 2 * H_C), f32)], axis=1).astype(bf16)
        proj, proj_ab = in_proj(hn, w_pack, w_ab, tm=1088, tn=1024)

        lam_init = 0.8 - 0.6 * math.exp(-0.3 * l)
        lam = (jnp.exp(jnp.sum(lam_q1[l] * lam_k1[l])) - jnp.exp(jnp.sum(lam_q2[l] * lam_k2[l])) + lam_init)
        lam = lam.astype(f32).reshape(1)
        gain = 1.0 - lam_init
        sw = subln_w[l][None, :]
        ya = (attn_prompt(proj, lam, slopes, sw, gain, BATCH, SEQ, ATT_TQ, ATT_HPB),
              attn_sample(proj, ckt, cvr, l, page_table, lam, slopes, sw, gain, N_PROMPT, DEC_BATCH, DEC_SEQ))

        sp = jax.nn.softplus(-lru_lambda[l])[None, :]
        lru_args = (conv_b_w[l], lru_conv_w[l], lru_conv_b[l][None, :], lru_wa[l], lru_ba[l][None, :],
                    lru_wx[l], lru_bx[l][None, :], sp)
        yb_p, yd_p, cb_p, lb_p, lh_p = conv_lru(proj, 0, BATCH, SEQ, 1, LRU_CHUNK, zero_cb, zero_lc, zero_lh,
                                                *lru_args)
        yb_s, yd_s, cb_s, lb_s, lh_s = conv_lru(proj, N_PROMPT, DEC_BATCH, DEC_SEQ, LRU_SPS, DEC_SEQ,
                                                state_conv_b[l], state_lru_conv[l], state_lru_h[l][:, None, :],
                                                *lru_args)

        dn_args = (dn_conv_w[l], _lane_row(-jnp.exp(dn_a_log[l])), _lane_row(dn_dt_bias[l]), dn_norm_w[l][None, :])
        yc_p, dc_p, ds_p = deltanet(proj, proj_ab, 0, BATCH, SEQ, 1, DN_SUB_PROMPT, DN_CHUNK, zero_dc, zero_ds,
                                    *dn_args)
        yc_s, dc_s, ds_s = deltanet(proj, proj_ab, N_PROMPT, DEC_BATCH, DEC_SEQ, DN_SPS, 1,
                                    math.gcd(DEC_SEQ, DN_CHUNK), state_dn_conv[l], state_dn[l], *dn_args)

        k_new = proj[:, W_A:2 * W_A]
        v_new = proj[:, 2 * W_A:3 * W_A]
        ks_p.append(k_new[:N_PROMPT].reshape(BATCH, SEQ, H_A, 2, DH_A))
        vs_p.append(v_new[:N_PROMPT].reshape(BATCH, SEQ, H_A, DV_A))
        ks_s.append(k_new[N_PROMPT:].reshape(DEC_BATCH, DEC_SEQ, H_A, 2, DH_A))
        vs_s.append(v_new[N_PROMPT:].reshape(DEC_BATCH, DEC_SEQ, H_A, DV_A))
        for lst, vals in ((st_p, (cb_p, dc_p, ds_p, lb_p, lh_p[:, 0, :])), (st_s, (cb_s, dc_s, ds_s, lb_s, lh_s[:, 0, :]))):
            for i, val in enumerate(vals):
                lst[i].append(val)

        x = mix_out_residual((ya, (yb_p, yb_s), (yc_p, yc_s), (yd_p, yd_s)), w_out[l].astype(bf16), x,
                             mod_p, mod_s, 2, tm=1024, tn=1024)

        j = l // 2
        if l % 2 == 0:
            hn = norm_mod(x, norm_ffn[l][None, :], mod_p, mod_s, 4, 3)
            h = swiglu_up(hn, ffn_w_gate[j].astype(bf16), ffn_w_up[j].astype(bf16), tm=1088, tn=512)
            x = matmul_gated_residual(h, ffn_w_down[j].astype(bf16), x, mod_p, mod_s, 5, tm=1024, tn=256, tk=D_FF)
        else:
            router = jnp.concatenate([moe_router[j], jnp.zeros((D_MODEL, LANES - N_EXPERTS), f32)], axis=1)
            hn, logits = norm_mod(x, norm_ffn[l][None, :], mod_p, mod_s, 4, 3, router=router)
            plan = moe_route(logits[:, :N_EXPERTS], D_FF_EXPERT // MOE_TN)
            xs = jnp.take(hn, plan['row_src'], axis=0, mode='clip')
            h, wd_bf = moe_up(plan['item_m'], plan['item_n'], plan['item_wn'], plan['item_e'], plan['item_v'], xs,
                              moe_w_gate[j], moe_w_up[j], moe_w_down[j])
            ys = moe_down(plan['tile_e'], plan['n_used'], h, wd_bf)
            gates = plan['gates']
            dest = plan['dest']
            f = (gates[:, 0:1] * jnp.take(ys, dest[:, 0], axis=0, mode='clip')
                 + gates[:, 1:2] * jnp.take(ys, dest[:, 1], axis=0, mode='clip'))
            g2_tok = jnp.concatenate([jnp.repeat(mod_p[5, :, 0, :], SEQ, axis=0),
                                      jnp.repeat(mod_s[5], DEC_SEQ, axis=0)], axis=0)
            x = x + g2_tok * f

    y_prompt, y_sample = final_norm(x, final_norm_w[None, :])
    y_prompt = y_prompt.reshape(BATCH, SEQ, D_MODEL)
    y_sample = y_sample.reshape(DEC_BATCH, DEC_SEQ, D_MODEL)
    sp_ = [jnp.stack(o) for o in st_p]
    ss_ = [jnp.stack(o) for o in st_s]
    return (y_prompt, y_sample, jnp.stack(ks_p), jnp.stack(vs_p), jnp.stack(ks_s), jnp.stack(vs_s),
            sp_[0], ss_[0], sp_[1], ss_[1], sp_[2], ss_[2], sp_[3], ss_[3], sp_[4], ss_[4])
```

```python
import functools
import math

import jax
import jax.numpy as jnp
from jax import lax
from jax.experimental import pallas as pl
from jax.experimental.pallas import tpu as pltpu

D_MODEL = 2048
BATCH = 8
SEQ = 2048
DEPTH = 2
DEC_BATCH = 128
DEC_SEQ = 8
W_GROUP = D_MODEL // 4
W_A = W_B = W_C = W_D = W_GROUP
H_A = 4
DH_A = W_A // (2 * H_A)
DV_A = 2 * DH_A
Q_BLOCK = 128
CONV_B = 3
H_C = 4
DK_C = W_C // H_C
DV_C = W_C // H_C
CONV_C = 4
DN_CHUNK = 64
H_D = 4
BW_D = W_D // H_D
CONV_D = 4
RG_C = 8.0
D_FF = 5632
N_EXPERTS = 8
TOP_K = 2
D_FF_EXPERT = 7168
N_MOD = 6
N_MIX = 4
EPS = 1e-6
N_A = 3 * W_A
N_B = 3 * W_B
N_C = 4 * W_C + 2 * H_C
N_D = 2 * W_D

N_PROMPT = BATCH * SEQ
N_SAMPLE = DEC_BATCH * DEC_SEQ
N_TOK = N_PROMPT + N_SAMPLE
SUBLANES = 8
LANES = 128

OFF_A = 0
OFF_B = N_A
OFF_C = N_A + N_B
OFF_D = OFF_C + 4 * W_C
N_PROJ = OFF_D + N_D

VMEM_LIMIT = 48 * 1024 * 1024
MOE_TM = 1024
MOE_TN = 512
MOE_TK = 1024
MOE_CAST_ROWS = 224
MOE_TILES = (TOP_K * N_TOK) // MOE_TM + N_EXPERTS
MOE_ROWS = MOE_TILES * MOE_TM
ROW_TILE = 512
ATT_TQ = 512
ATT_HPB = 2
LRU_CHUNK = 256
LRU_SPS = 16
DN_SPS = 8
DN_SUB_PROMPT = 4

f32 = jnp.float32
bf16 = jnp.bfloat16


def _params(sem):
    return pltpu.CompilerParams(dimension_semantics=sem, vmem_limit_bytes=VMEM_LIMIT)


def _silu(x):
    return x * (1.0 / (1.0 + jnp.exp(-x)))


def _mod_kernel(c_ref, w_ref, b_ref, o_ref):
    c = _silu(c_ref[...]).astype(bf16)
    o_ref[...] = jnp.dot(c, w_ref[0].astype(bf16), preferred_element_type=f32) + b_ref[0]


def modulation(c_all, w, b, layer):
    r = c_all.shape[0]
    n = w.shape[2]
    tn = 1024
    return pl.pallas_call(
        _mod_kernel,
        grid=(n // tn,),
        in_specs=[pl.BlockSpec((r, D_MODEL), lambda j: (0, 0)),
                  pl.BlockSpec((1, D_MODEL, tn), lambda j: (layer, 0, j)),
                  pl.BlockSpec((1, 1, tn), lambda j: (layer, 0, j))],
        out_specs=pl.BlockSpec((r, tn), lambda j: (0, j)),
        out_shape=jax.ShapeDtypeStruct((r, n), f32),
        compiler_params=_params(("arbitrary",)),
        name="modulation",
    )(c_all, w, b)


def _mod_specs(idx, tile_rows, col_block):
    tiles_per_req = SEQ // tile_rows
    npt = N_PROMPT // tile_rows

    def p_map(*ids):
        return (idx, jnp.minimum(ids[0] // tiles_per_req, BATCH - 1), 0, col_block(*ids))

    def s_map(*ids):
        return (idx, jnp.maximum(ids[0] - npt, 0), col_block(*ids))

    return p_map, s_map


def _row_pair(x, tile_rows, cols, col_block):
    npt = N_PROMPT // tile_rows
    if isinstance(x, tuple):
        xp, xs = x
        s_map = lambda *ids: (jnp.maximum(ids[0] - npt, 0), col_block(*ids))
    else:
        xp = xs = x
        s_map = lambda *ids: (jnp.maximum(ids[0], npt), col_block(*ids))
    p_map = lambda *ids: (jnp.minimum(ids[0], npt - 1), col_block(*ids))
    return (xp, xs), [pl.BlockSpec((tile_rows, cols), p_map), pl.BlockSpec((tile_rows, cols), s_map)]


def _apply_rows(vals, i, tile_rows, fn_prompt, fn_sample):
    npt = N_PROMPT // tile_rows

    @pl.when(i < npt)
    def _():
        fn_prompt(*vals)

    @pl.when(i >= npt)
    def _():
        fn_sample(*vals)


def _bcast_groups(a, m):
    rows, c = a.shape
    return a.reshape(rows // SUBLANES, SUBLANES, c), m[:, None, :]


def _norm_mod_emit(xp_ref, xs_ref, nw_ref, scp_ref, shp_ref, scs_ref, shs_ref, emit):
    tr = xp_ref.shape[0]

    def normed(x):
        return x * lax.rsqrt(jnp.mean(x * x, axis=-1, keepdims=True) + EPS) * nw_ref[...]

    def prompt():
        emit(normed(xp_ref[...]) * (1.0 + scp_ref[0, 0]) + shp_ref[0, 0])

    def sample():
        y3, sc = _bcast_groups(normed(xs_ref[...]), scs_ref[0])
        emit((y3 * (1.0 + sc) + shs_ref[0][:, None, :]).reshape(tr, D_MODEL))

    _apply_rows((), pl.program_id(0), tr, prompt, sample)


def _norm_mod_kernel(xp_ref, xs_ref, nw_ref, scp_ref, shp_ref, scs_ref, shs_ref, o_ref):
    def emit(hn):
        o_ref[...] = hn.astype(o_ref.dtype)

    _norm_mod_emit(xp_ref, xs_ref, nw_ref, scp_ref, shp_ref, scs_ref, shs_ref, emit)


def _norm_mod_router_kernel(xp_ref, xs_ref, nw_ref, scp_ref, shp_ref, scs_ref, shs_ref, r_ref, o_ref, lg_ref):
    def emit(hn):
        bits = lax.bitcast_convert_type(hn.astype(bf16).astype(f32), jnp.uint32)
        half = D_MODEL // 2
        o_ref[...] = (bits[:, :half] >> 16) | bits[:, half:]
        lg_ref[...] = jnp.dot(hn, r_ref[...], preferred_element_type=f32, precision=lax.Precision.HIGHEST)

    _norm_mod_emit(xp_ref, xs_ref, nw_ref, scp_ref, shp_ref, scs_ref, shs_ref, emit)


def norm_mod(x, nw, mod_p, mod_s, i_scale, i_shift, router=None):
    tr = ROW_TILE
    tg = tr // SUBLANES
    zero = lambda i: 0
    scp_map, scs_map = _mod_specs(i_scale, tr, zero)
    shp_map, shs_map = _mod_specs(i_shift, tr, zero)
    x_ops, x_specs = _row_pair(x, tr, D_MODEL, zero)
    in_specs = x_specs + [
                pl.BlockSpec((1, D_MODEL), lambda i: (0, 0)),
                pl.BlockSpec((1, 1, 1, D_MODEL), scp_map), pl.BlockSpec((1, 1, 1, D_MODEL), shp_map),
                pl.BlockSpec((1, tg, D_MODEL), scs_map), pl.BlockSpec((1, tg, D_MODEL), shs_map)]
    out_spec = pl.BlockSpec((tr, D_MODEL), lambda i: (i, 0))
    out_shape = jax.ShapeDtypeStruct((N_TOK, D_MODEL), bf16)
    if router is not None:
        out_spec = pl.BlockSpec((tr, D_MODEL // 2), lambda i: (i, 0))
        out_shape = jax.ShapeDtypeStruct((N_TOK, D_MODEL // 2), jnp.uint32)
    if router is None:
        return pl.pallas_call(
            _norm_mod_kernel, grid=(N_TOK // tr,), in_specs=in_specs, out_specs=out_spec,
            out_shape=out_shape, compiler_params=_params(("arbitrary",)), name="norm_mod",
        )(*x_ops, nw, mod_p, mod_p, mod_s, mod_s)
    return pl.pallas_call(
        _norm_mod_router_kernel, grid=(N_TOK // tr,),
        in_specs=in_specs + [pl.BlockSpec((D_MODEL, LANES), lambda i: (0, 0))],
        out_specs=[out_spec, pl.BlockSpec((tr, LANES), lambda i: (i, 0))],
        out_shape=[out_shape, jax.ShapeDtypeStruct((N_TOK, LANES), f32)],
        compiler_params=_params(("arbitrary",)), name="norm_mod_router",
    )(*x_ops, nw, mod_p, mod_p, mod_s, mod_s, router)


def _final_norm_kernel(x_ref, nw_ref, op_ref, os_ref, *, n_prompt_tiles):
    x = x_ref[...]
    y = x * lax.rsqrt(jnp.mean(x * x, axis=-1, keepdims=True) + EPS) * nw_ref[...]
    i = pl.program_id(0)

    @pl.when(i < n_prompt_tiles)
    def _():
        op_ref[...] = y

    @pl.when(i >= n_prompt_tiles)
    def _():
        os_ref[...] = y


def final_norm(x, nw):
    tr = ROW_TILE
    npt = N_PROMPT // tr
    return pl.pallas_call(
        functools.partial(_final_norm_kernel, n_prompt_tiles=npt), grid=(N_TOK // tr,),
        in_specs=[pl.BlockSpec((tr, D_MODEL), lambda i: (i, 0)),
                  pl.BlockSpec((1, D_MODEL), lambda i: (0, 0))],
        out_specs=[pl.BlockSpec((tr, D_MODEL), lambda i: (jnp.minimum(i, npt - 1), 0)),
                   pl.BlockSpec((tr, D_MODEL), lambda i: (jnp.maximum(i - npt, 0), 0))],
        out_shape=[jax.ShapeDtypeStruct((N_PROMPT, D_MODEL), f32), jax.ShapeDtypeStruct((N_SAMPLE, D_MODEL), f32)],
        compiler_params=_params(("arbitrary",)), name="final_norm",
    )(x, nw)


def _in_proj_kernel(x_ref, w_ref, wab_ref, o_ref, oab_ref):
    x = x_ref[...]
    o_ref[...] = jnp.dot(x, w_ref[...], preferred_element_type=f32)

    @pl.when(pl.program_id(1) == 0)
    def _():
        oab_ref[...] = jnp.dot(x, wab_ref[...], preferred_element_type=f32)


def in_proj(x, w, w_ab, tm, tn):
    m, k = x.shape
    n = w.shape[1]
    return pl.pallas_call(
        _in_proj_kernel, grid=(m // tm, n // tn),
        in_specs=[pl.BlockSpec((tm, k), lambda i, j: (i, 0)),
                  pl.BlockSpec((k, tn), lambda i, j: (0, j)),
                  pl.BlockSpec((k, LANES), lambda i, j: (0, 0))],
        out_specs=[pl.BlockSpec((tm, tn), lambda i, j: (i, j)), pl.BlockSpec((tm, LANES), lambda i, j: (i, 0))],
        out_shape=[jax.ShapeDtypeStruct((m, n), f32), jax.ShapeDtypeStruct((m, LANES), f32)],
        compiler_params=_params(("arbitrary", "arbitrary")), name="in_proj",
    )(x, w, w_ab)


def _gated_residual_store(acc, xp_ref, xs_ref, gp_ref, gs_ref, o_ref):
    tm, tn = acc.shape

    def prompt(acc):
        o_ref[...] = xp_ref[...] + acc * gp_ref[0, 0]

    def sample(acc):
        a3, g = _bcast_groups(acc, gs_ref[0])
        o_ref[...] = xs_ref[...] + (a3 * g).reshape(tm, tn)

    _apply_rows((acc,), pl.program_id(0), tm, prompt, sample)


def _mm_resid_kernel(y_ref, w_ref, xp_ref, xs_ref, gp_ref, gs_ref, o_ref, acc_ref, *, nk):
    k = pl.program_id(2)
    part = jnp.dot(y_ref[...], w_ref[...], preferred_element_type=f32)

    @pl.when(k == 0)
    def _():
        acc_ref[...] = part

    @pl.when(k > 0)
    def _():
        acc_ref[...] += part

    @pl.when(k == nk - 1)
    def _():
        _gated_residual_store(acc_ref[...], xp_ref, xs_ref, gp_ref, gs_ref, o_ref)


def matmul_gated_residual(y, w, x, mod_p, mod_s, i_gate, tm, tn, tk):
    m, kdim = y.shape
    n = w.shape[1]
    nk = kdim // tk
    gp_map, gs_map = _mod_specs(i_gate, tm, lambda i, j, k: j)
    x_ops, x_specs = _row_pair(x, tm, tn, lambda i, j, k: j)
    return pl.pallas_call(
        functools.partial(_mm_resid_kernel, nk=nk), grid=(m // tm, n // tn, nk),
        in_specs=[pl.BlockSpec((tm, tk), lambda i, j, k: (i, k)),
                  pl.BlockSpec((tk, tn), lambda i, j, k: (k, j)),
                  *x_specs,
                  pl.BlockSpec((1, 1, 1, tn), gp_map),
                  pl.BlockSpec((1, tm // SUBLANES, tn), gs_map)],
        out_specs=pl.BlockSpec((tm, tn), lambda i, j, k: (i, j)),
        out_shape=jax.ShapeDtypeStruct((m, n), f32),
        scratch_shapes=[pltpu.VMEM((tm, tn), f32)],
        compiler_params=_params(("arbitrary", "arbitrary", "arbitrary")), name="matmul_gated_residual",
    )(y, w, *x_ops, mod_p, mod_s)


def _mix_out_kernel(*refs):
    y_refs, (w_ref, xp_ref, xs_ref, gp_ref, gs_ref, o_ref) = refs[:2 * N_MIX], refs[2 * N_MIX:]
    tm = o_ref.shape[0]

    def project(group_refs):
        acc = jnp.dot(group_refs[0][...], w_ref[0:W_GROUP, :], preferred_element_type=f32)
        for gi in range(1, N_MIX):
            acc = acc + jnp.dot(group_refs[gi][...], w_ref[gi * W_GROUP:(gi + 1) * W_GROUP, :],
                                preferred_element_type=f32)
        return acc

    def prompt():
        o_ref[...] = xp_ref[...] + project(y_refs[0::2]) * gp_ref[0, 0]

    def sample():
        a3, g = _bcast_groups(project(y_refs[1::2]), gs_ref[0])
        o_ref[...] = xs_ref[...] + (a3 * g).reshape(o_ref.shape)

    _apply_rows((), pl.program_id(0), tm, prompt, sample)


def mix_out_residual(ys, w, x, mod_p, mod_s, i_gate, tm, tn):
    m, n = N_TOK, D_MODEL
    gp_map, gs_map = _mod_specs(i_gate, tm, lambda i, j: j)
    x_ops, x_specs = _row_pair(x, tm, tn, lambda i, j: j)
    y_ops, y_specs = [], []
    for pair in ys:
        ops, specs = _row_pair(pair, tm, W_GROUP, lambda i, j: 0)
        y_ops += ops
        y_specs += specs
    return pl.pallas_call(
        _mix_out_kernel, grid=(m // tm, n // tn),
        in_specs=[*y_specs,
                  pl.BlockSpec((D_MODEL, tn), lambda i, j: (0, j)),
                  *x_specs,
                  pl.BlockSpec((1, 1, 1, tn), gp_map),
                  pl.BlockSpec((1, tm // SUBLANES, tn), gs_map)],
        out_specs=pl.BlockSpec((tm, tn), lambda i, j: (i, j)),
        out_shape=jax.ShapeDtypeStruct((m, n), f32),
        compiler_params=_params(("arbitrary", "arbitrary")), name="mix_out_residual",
    )(*y_ops, w, *x_ops, mod_p, mod_s)


def _swiglu_body(x, wg, wu):
    g = jnp.dot(x, wg, preferred_element_type=f32)
    u = jnp.dot(x, wu, preferred_element_type=f32)
    return _silu(g) * u


def _swiglu_kernel(x_ref, wg_ref, wu_ref, o_ref):
    o_ref[...] = _swiglu_body(x_ref[...], wg_ref[...], wu_ref[...]).astype(o_ref.dtype)


def swiglu_up(x, wg, wu, tm, tn):
    m, k = x.shape
    n = wg.shape[1]
    return pl.pallas_call(
        _swiglu_kernel, grid=(m // tm, n // tn),
        in_specs=[pl.BlockSpec((tm, k), lambda i, j: (i, 0)),
                  pl.BlockSpec((k, tn), lambda i, j: (0, j)),
                  pl.BlockSpec((k, tn), lambda i, j: (0, j))],
        out_specs=pl.BlockSpec((tm, tn), lambda i, j: (i, j)),
        out_shape=jax.ShapeDtypeStruct((m, n), bf16),
        compiler_params=_params(("arbitrary", "arbitrary")), name="swiglu_up",
    )(x, wg, wu)


def _unpack_tokens(words):
    lo = lax.bitcast_convert_type(words << 16, f32)
    hi = lax.bitcast_convert_type(words & jnp.uint32(0xFFFF0000), f32)
    return jnp.concatenate([lo, hi], axis=1).astype(bf16)


def _moe_up_kernel(m_ref, n_ref, wn_ref, e_ref, v_ref, x_ref, wg_ref, wu_ref, wd_ref, o_ref, wdb_ref, wg_sc, wu_sc,
                   *, n_cast):
    i = pl.program_id(0)

    @pl.when(i < n_cast)
    def _():
        wdb_ref[...] = wd_ref[...].astype(bf16)

    @pl.when(v_ref[i] == 2)
    def _():
        wg_sc[...] = wg_ref[0].astype(bf16)
        wu_sc[...] = wu_ref[0].astype(bf16)

    @pl.when(v_ref[i] > 0)
    def _():
        o_ref[...] = _swiglu_body(_unpack_tokens(x_ref[...]), wg_sc[...], wu_sc[...]).astype(o_ref.dtype)

    @pl.when(v_ref[i] == 0)
    def _():
        o_ref[...] = jnp.zeros(o_ref.shape, o_ref.dtype)


def moe_up(item_m, item_n, item_wn, item_e, item_v, xs, wg, wu, wd):
    ff = wg.shape[2]
    nt = ff // MOE_TN
    n_items = MOE_TILES * nt
    wd_rows = wd.shape[0] * wd.shape[1]
    n_cast = wd_rows // MOE_CAST_ROWS
    assert n_cast <= (TOP_K * N_TOK // MOE_TM) * nt and wd_rows % MOE_CAST_ROWS == 0
    cast_spec = pl.BlockSpec((MOE_CAST_ROWS, D_MODEL), lambda i, m, n, wn, e, v: (jnp.minimum(i, n_cast - 1), 0))
    grid_spec = pltpu.PrefetchScalarGridSpec(
        num_scalar_prefetch=5, grid=(n_items,),
        in_specs=[pl.BlockSpec((MOE_TM, D_MODEL // 2), lambda i, m, n, wn, e, v: (m[i], 0)),
                  pl.BlockSpec((1, D_MODEL, MOE_TN), lambda i, m, n, wn, e, v: (e[i], 0, wn[i])),
                  pl.BlockSpec((1, D_MODEL, MOE_TN), lambda i, m, n, wn, e, v: (e[i], 0, wn[i])),
                  cast_spec],
        out_specs=[pl.BlockSpec((MOE_TM, MOE_TN), lambda i, m, n, wn, e, v: (m[i], n[i])), cast_spec],
        scratch_shapes=[pltpu.VMEM((D_MODEL, MOE_TN), bf16), pltpu.VMEM((D_MODEL, MOE_TN), bf16)])
    h, wd_bf = pl.pallas_call(
        functools.partial(_moe_up_kernel, n_cast=n_cast), grid_spec=grid_spec,
        out_shape=[jax.ShapeDtypeStruct((MOE_ROWS, ff), bf16), jax.ShapeDtypeStruct((wd_rows, D_MODEL), bf16)],
        compiler_params=_params(("arbitrary",)), name="moe_up",
    )(item_m, item_n, item_wn, item_e, item_v, xs, wg, wu, wd.reshape(wd_rows, D_MODEL))
    return h, wd_bf.reshape(wd.shape)


def _moe_down_kernel(te_ref, nu_ref, h_ref, wd_ref, o_ref):
    m = pl.program_id(0)
    k = pl.program_id(1)

    @pl.when(m < nu_ref[0])
    def _():
        part = jnp.dot(h_ref[...], wd_ref[0], preferred_element_type=f32)

        @pl.when(k == 0)
        def _():
            o_ref[...] = part

        @pl.when(k > 0)
        def _():
            o_ref[...] += part

    @pl.when(jnp.logical_and(m >= nu_ref[0], k == 0))
    def _():
        o_ref[...] = jnp.zeros(o_ref.shape, o_ref.dtype)


def moe_down(tile_e, n_used, h, wd):
    nk = wd.shape[1] // MOE_TK

    def mclamp(m, nu):
        return jnp.minimum(m, nu[0] - 1)

    def kclamp(m, k, nu):
        return jnp.where(m < nu[0], k, nk - 1)

    grid_spec = pltpu.PrefetchScalarGridSpec(
        num_scalar_prefetch=2, grid=(MOE_TILES, nk),
        in_specs=[pl.BlockSpec((MOE_TM, MOE_TK), lambda m, k, te, nu: (mclamp(m, nu), kclamp(m, k, nu))),
                  pl.BlockSpec((1, MOE_TK, D_MODEL),
                               lambda m, k, te, nu: (te[mclamp(m, nu)], kclamp(m, k, nu), 0))],
        out_specs=pl.BlockSpec((MOE_TM, D_MODEL), lambda m, k, te, nu: (m, 0)))
    return pl.pallas_call(
        _moe_down_kernel, grid_spec=grid_spec,
        out_shape=jax.ShapeDtypeStruct((MOE_ROWS, D_MODEL), f32),
        compiler_params=_params(("arbitrary", "arbitrary")), name="moe_down",
    )(tile_e, n_used, h, wd)


def moe_route(logits, n_ff_tiles):
    top_v, top_i = lax.top_k(logits, TOP_K)
    gates = jax.nn.softmax(top_v, axis=-1)
    e_flat = top_i.reshape(-1).astype(jnp.int32)
    onehot = (e_flat[:, None] == jnp.arange(N_EXPERTS, dtype=jnp.int32)[None, :]).astype(jnp.int32)
    csum = jnp.cumsum(onehot, axis=0)
    counts = csum[-1]
    rank = jnp.sum((csum - onehot) * onehot, axis=-1)
    tiles_e = (counts + MOE_TM - 1) // MOE_TM
    tile_end = jnp.cumsum(tiles_e)
    tile_start = tile_end - tiles_e
    dest = tile_start[e_flat] * MOE_TM + rank
    tok = jnp.arange(TOP_K * N_TOK, dtype=jnp.int32) // TOP_K
    row_src = (jnp.arange(MOE_ROWS, dtype=jnp.int32) % N_TOK).at[dest].set(tok)
    n_used = tile_end[-1]
    t_ids = jnp.arange(MOE_TILES, dtype=jnp.int32)
    tile_e = jnp.minimum(jnp.searchsorted(tile_end, t_ids, side="right"), N_EXPERTS - 1).astype(jnp.int32)
    nt = n_ff_tiles
    items_e = tiles_e * nt
    item_end = jnp.cumsum(items_e)
    item_start = item_end - items_e
    total = item_end[-1]
    idx = jnp.arange(MOE_TILES * nt, dtype=jnp.int32)
    valid = idx < total
    idc = jnp.minimum(idx, total - 1)
    ie = jnp.minimum(jnp.searchsorted(item_end, idc, side="right"), N_EXPERTS - 1).astype(jnp.int32)
    local = idc - item_start[ie]
    te = jnp.maximum(tiles_e[ie], 1)
    item_wn = (local // te).astype(jnp.int32)
    spare = idx - total
    item_n = jnp.where(valid, item_wn, spare % nt).astype(jnp.int32)
    item_m = jnp.where(valid, tile_start[ie] + local % te, n_used + spare // nt).astype(jnp.int32)
    return dict(gates=gates, dest=dest.reshape(N_TOK, TOP_K), row_src=row_src, tile_e=tile_e,
                n_used=n_used.reshape(1).astype(jnp.int32), item_m=item_m, item_n=item_n, item_wn=item_wn, item_e=ie,
                item_v=jnp.where(valid, jnp.where(local % te == 0, 2, 1), 0).astype(jnp.int32))


def _smem_spec():
    return pl.BlockSpec(memory_space=pltpu.SMEM)


def _subln(o, w, gain):
    return o * lax.rsqrt(jnp.mean(o * o, axis=-1, keepdims=True) + EPS) * w * gain


def _attn_prompt_kernel(lam_ref, slope_ref, q_ref, k_ref, v_ref, w_ref, o_ref, m_sc, acc_sc, *, tq, hpb, gain):
    hb = pl.program_id(1)
    qi = pl.program_id(2)
    lane = lax.broadcasted_iota(jnp.int32, (tq, DV_A), 1)
    row = lax.broadcasted_iota(jnp.int32, (2 * tq, tq), 0)
    col = lax.broadcasted_iota(jnp.int32, (2 * tq, tq), 1)
    rel = (col - jnp.where(row >= tq, row - tq, row)).astype(f32)
    ones_v = jnp.ones((tq, DV_A), bf16)
    qss, slopes, biases = [], [], []
    for hh in range(hpb):
        q = q_ref[:, hh * DV_A:(hh + 1) * DV_A] * (DH_A ** -0.5)
        qss.append(jnp.concatenate([jnp.where(lane < DH_A, q, 0.0), jnp.where(lane >= DH_A, q, 0.0)],
                                   axis=0).astype(bf16))
        slopes.append(slope_ref[hb * hpb + hh])
        biases.append(slopes[hh] * rel)
    m_sc[...] = jnp.full(m_sc.shape, -jnp.inf, f32)
    acc_sc[...] = jnp.zeros(acc_sc.shape, f32)

    def chunk_step(j, diagonal):
        off = pl.multiple_of(j * tq, tq)
        blocks = (jnp.zeros((1, 1), jnp.int32) + (qi - j) * tq).astype(f32)
        for hh in range(hpb):
            cs = slice(hh * DV_A, (hh + 1) * DV_A)
            kc = k_ref[pl.ds(off, tq), cs].astype(bf16)
            vc = jnp.concatenate([v_ref[pl.ds(off, tq), cs].astype(bf16), ones_v], axis=1)
            s = lax.dot_general(qss[hh], kc, (((1,), (1,)), ((), ())), preferred_element_type=f32)
            s = s + biases[hh] - slopes[hh] * blocks
            if diagonal:
                s = jnp.where(rel <= 0, s, -1e30)
            m_prev = m_sc[hh]
            m_new = jnp.maximum(m_prev, jnp.max(s, axis=-1, keepdims=True))
            pr = jnp.exp(s - m_new).astype(bf16)
            acc_sc[hh] = jnp.exp(m_prev - m_new) * acc_sc[hh] + jnp.dot(pr, vc, preferred_element_type=f32)
            m_sc[hh] = m_new

    def body(j, carry):
        chunk_step(j, False)
        return carry

    lax.fori_loop(0, qi, body, 0)
    chunk_step(qi, True)
    for hh in range(hpb):
        acc = acc_sc[hh]
        o = acc[:, :DV_A] / acc[:, DV_A:]
        o = o[:tq] - lam_ref[0] * o[tq:]
        o_ref[:, hh * DV_A:(hh + 1) * DV_A] = _subln(o, w_ref[...], gain).astype(o_ref.dtype)


def attn_prompt(proj, lam, slopes, subln_w, gain, batch, seq, tq, hpb):
    nq = seq // tq
    wb = hpb * DV_A
    nhb = H_A // hpb
    return pl.pallas_call(
        functools.partial(_attn_prompt_kernel, tq=tq, hpb=hpb, gain=gain),
        grid=(batch, nhb, nq),
        in_specs=[_smem_spec(), _smem_spec(),
                  pl.BlockSpec((tq, wb), lambda b, h, qi: (b * nq + qi, h)),
                  pl.BlockSpec((seq, wb), lambda b, h, qi: (b, nhb + h)),
                  pl.BlockSpec((seq, wb), lambda b, h, qi: (b, 2 * nhb + h)),
                  pl.BlockSpec((1, DV_A), lambda b, h, qi: (0, 0))],
        out_specs=pl.BlockSpec((tq, wb), lambda b, h, qi: (b * nq + qi, h)),
        out_shape=jax.ShapeDtypeStruct((batch * seq, W_A), bf16),
        scratch_shapes=[pltpu.VMEM((hpb, 2 * tq, 1), f32), pltpu.VMEM((hpb, 2 * tq, 2 * DV_A), f32)],
        compiler_params=_params(("arbitrary", "arbitrary", "arbitrary")), name="attn_prompt",
    )(lam, slopes, proj, proj, proj, subln_w)


def _attn_sample_kernel(pt_ref, lam_ref, slope_ref, q_ref, kn_ref, vn_ref, w_ref, *rest, t, n_pages, page, gain):
    k_refs = rest[:n_pages]
    v_refs = rest[n_pages:2 * n_pages]
    o_ref = rest[2 * n_pages]
    s_sc = rest[2 * n_pages + 1]
    n_rep = 2 * H_A
    rows = n_rep * t
    n_past = n_pages * page
    q = q_ref[...] * (DH_A ** -0.5)
    q_rep = jnp.concatenate([q] * n_rep, axis=0)
    r_grp = lax.broadcasted_iota(jnp.int32, (rows, W_A), 0) // t
    c_grp = lax.broadcasted_iota(jnp.int32, (rows, W_A), 1) // DH_A
    qbd = jnp.where(r_grp == c_grp, q_rep, 0.0).astype(bf16)
    for j in range(n_pages):
        s_sc[:, j * page:(j + 1) * page] = jnp.dot(qbd, k_refs[j][0, 0].astype(bf16), preferred_element_type=f32)
    s_new = lax.dot_general(qbd, kn_ref[...].astype(bf16), (((1,), (1,)), ((), ())), preferred_element_type=f32)

    row1 = lax.broadcasted_iota(jnp.int32, (rows, 1), 0)
    head1 = row1 // (2 * t)
    slope = jnp.where(head1 == 0, slope_ref[0],
                      jnp.where(head1 == 1, slope_ref[1], jnp.where(head1 == 2, slope_ref[2], slope_ref[3])))
    t_row = (row1 % t)
    kpos = lax.broadcasted_iota(jnp.int32, (rows, n_past), 1)
    s_past = s_sc[...] - slope * (n_past + t_row - kpos).astype(f32)
    dist_new = (t_row - lax.broadcasted_iota(jnp.int32, (rows, t), 1)).astype(f32)
    s_new = jnp.where(dist_new >= 0, s_new - slope * dist_new, -1e30)
    m = jnp.maximum(jnp.max(s_past, axis=-1, keepdims=True), jnp.max(s_new, axis=-1, keepdims=True))
    p_past = jnp.exp(s_past - m)
    p_new = jnp.exp(s_new - m)
    denom = jnp.sum(p_past, axis=-1, keepdims=True) + jnp.sum(p_new, axis=-1, keepdims=True)
    p_past = p_past.astype(bf16)
    p_new = p_new.astype(bf16)
    lam = lam_ref[0]
    w = w_ref[...]
    for h in range(H_A):
        vh = jnp.concatenate([v_refs[j][0, 0, pl.ds(h, page, stride=H_A), :] for j in range(n_pages)], axis=0)
        rs = slice(h * 2 * t, (h + 1) * 2 * t)
        oh = jnp.dot(p_past[rs], vh.astype(bf16), preferred_element_type=f32)
        oh = oh + jnp.dot(p_new[rs], vn_ref[:, h * DV_A:(h + 1) * DV_A].astype(bf16), preferred_element_type=f32)
        oh = oh / denom[rs]
        o = oh[:t] - lam * oh[t:]
        o_ref[:, h * DV_A:(h + 1) * DV_A] = _subln(o, w, gain).astype(o_ref.dtype)


def attn_sample(proj, ckt, cvr, layer, page_table, lam, slopes, subln_w, gain, row0, nseq, t):
    n_pages = page_table.shape[1]
    page = ckt.shape[-1]
    rb0 = row0 // t

    def kv_spec(j):
        return pl.BlockSpec((1, 1, W_A, page), lambda b, pt, j=j: (layer, pt[b * n_pages + j], 0, 0))

    in_specs = ([_smem_spec(), _smem_spec(),
                 pl.BlockSpec((t, W_A), lambda b, pt: (rb0 + b, 0)),
                 pl.BlockSpec((t, W_A), lambda b, pt: (rb0 + b, 1)),
                 pl.BlockSpec((t, W_A), lambda b, pt: (rb0 + b, 2)),
                 pl.BlockSpec((1, DV_A), lambda b, pt: (0, 0))]
                + [kv_spec(j) for j in range(n_pages)] + [kv_spec(j) for j in range(n_pages)])
    grid_spec = pltpu.PrefetchScalarGridSpec(
        num_scalar_prefetch=1, grid=(nseq,), in_specs=in_specs,
        out_specs=pl.BlockSpec((t, W_A), lambda b, pt: (b, 0)),
        scratch_shapes=[pltpu.VMEM((2 * H_A * t, n_pages * page), f32)])
    return pl.pallas_call(
        functools.partial(_attn_sample_kernel, t=t, n_pages=n_pages, page=page, gain=gain),
        grid_spec=grid_spec, out_shape=jax.ShapeDtypeStruct((nseq * t, W_A), bf16),
        compiler_params=_params(("arbitrary",)), name="attn_sample",
    )(page_table.reshape(-1), lam, slopes, proj, proj, proj, subln_w, *([ckt] * n_pages), *([cvr] * n_pages))


def _conv_step(f_sc, x3, w_ref, buf0_ref, buf_out, width, chunk, first):
    lo = SUBLANES - (width - 1)

    @pl.when(first)
    def _():
        f_sc[:, 0:SUBLANES, :] = jnp.zeros((f_sc.shape[0], SUBLANES, f_sc.shape[2]), f32)
        f_sc[:, lo:SUBLANES, :] = buf0_ref[...]

    f_sc[:, SUBLANES:SUBLANES + chunk, :] = x3
    out = w_ref[0:1, :][None] * f_sc[:, lo:lo + chunk, :]
    for j in range(1, width):
        out = out + w_ref[j:j + 1, :][None] * f_sc[:, lo + j:lo + j + chunk, :]
    buf_out[...] = f_sc[:, chunk + lo:chunk + SUBLANES, :]
    f_sc[:, 0:SUBLANES, :] = f_sc[:, chunk:chunk + SUBLANES, :]
    return out


def _sigmoid(x):
    return 1.0 / (1.0 + jnp.exp(-x))


def _gelu_tanh(x):
    return 0.5 * x * (1.0 + jnp.tanh(0.7978845608028654 * (x + 0.044715 * (x * x * x))))


def _conv_lru_kernel(pb_ref, pd_ref, cb0_ref, lb0_ref, h0_ref, wb_ref, wd_ref, bd_ref, wa_ref, ba_ref, wx_ref,
                     bx_ref, sp_ref, yb_ref, yd_ref, cb_out, lb_out, h_out, fb_sc, fd_sc, h_sc, *, sps, chunk):
    first = pl.program_id(1) == 0
    rows = sps * chunk
    pb = pb_ref[...]
    bg, cg, xt = pb[:, :W_B], pb[:, W_B:2 * W_B], pb[:, 2 * W_B:]
    conv_b = _conv_step(fb_sc, (cg * xt).reshape(sps, chunk, W_B), wb_ref, cb0_ref, cb_out, CONV_B, chunk, first)
    yb_ref[...] = (bg * conv_b.reshape(rows, W_B)).astype(yb_ref.dtype)

    pd = pd_ref[...]
    xd, gd = pd[:, :W_D], pd[:, W_D:]
    conv_d = _conv_step(fd_sc, xd.reshape(sps, chunk, W_D), wd_ref, lb0_ref, lb_out, CONV_D, chunk, first)
    xf = conv_d.reshape(rows, W_D) + bd_ref[...]
    xfb = xf.astype(bf16)
    ra = jnp.concatenate([jnp.dot(xfb[:, h * BW_D:(h + 1) * BW_D], wa_ref[h].astype(bf16),
                                  preferred_element_type=f32) for h in range(H_D)], axis=-1)
    rx = jnp.concatenate([jnp.dot(xfb[:, h * BW_D:(h + 1) * BW_D], wx_ref[h].astype(bf16),
                                  preferred_element_type=f32) for h in range(H_D)], axis=-1)
    r = _sigmoid(ra + ba_ref[...])
    ig = _sigmoid(rx + bx_ref[...])
    log_a = -RG_C * r * sp_ref[...]
    a = jnp.exp(log_a)
    th = jnp.tanh(log_a)
    bx = jnp.sqrt(-2.0 * th / (1.0 - th)) * (ig * xf)

    pos = lax.broadcasted_iota(jnp.int32, (rows, W_D), 0) % chunk
    step = 1
    while step < chunk:
        a_s = pltpu.roll(a, step, 0)
        b_s = pltpu.roll(bx, step, 0)
        live = pos >= step
        bx = jnp.where(live, a * b_s + bx, bx)
        a = jnp.where(live, a * a_s, a)
        step *= 2

    @pl.when(first)
    def _():
        h_sc[...] = h0_ref[...]

    h_in = jnp.broadcast_to(h_sc[...], (sps, chunk, W_D)).reshape(rows, W_D)
    hs = bx + a * h_in
    h_last = hs.reshape(sps, chunk, W_D)[:, chunk - 1:chunk, :]
    h_sc[...] = h_last
    h_out[...] = h_last
    yd_ref[...] = (hs * _gelu_tanh(gd)).astype(yd_ref.dtype)


def conv_lru(proj, row0, nseq, t, sps, chunk, conv_b_buf, lru_buf, lru_h, wb, wd, bd, wa, ba, wx, bx, sp):
    rows = sps * chunk
    nc = t // chunk
    rb0 = row0 // rows

    def rmap(s, c):
        return rb0 + s * nc + c

    def full(shape):
        return pl.BlockSpec(shape, lambda s, c: (0,) * len(shape))

    def per_seq(shape):
        return pl.BlockSpec((sps,) + shape, lambda s, c: (s,) + (0,) * len(shape))

    in_specs = [pl.BlockSpec((rows, N_B), lambda s, c: (rmap(s, c), OFF_B // N_B)),
                pl.BlockSpec((rows, N_D), lambda s, c: (rmap(s, c), OFF_D // N_D)),
                per_seq((CONV_B - 1, W_B)), per_seq((CONV_D - 1, W_D)), per_seq((1, W_D)),
                full((CONV_B, W_B)), full((CONV_D, W_D)), full((1, W_D)),
                full((H_D, BW_D, BW_D)), full((1, W_D)), full((H_D, BW_D, BW_D)), full((1, W_D)), full((1, W_D))]
    out_specs = [pl.BlockSpec((rows, W_B), lambda s, c: (s * nc + c, 0)),
                 pl.BlockSpec((rows, W_D), lambda s, c: (s * nc + c, 0)),
                 per_seq((CONV_B - 1, W_B)), per_seq((CONV_D - 1, W_D)), per_seq((1, W_D))]
    out_shape = [jax.ShapeDtypeStruct((nseq * t, W_B), bf16), jax.ShapeDtypeStruct((nseq * t, W_D), bf16),
                 jax.ShapeDtypeStruct((nseq, CONV_B - 1, W_B), f32),
                 jax.ShapeDtypeStruct((nseq, CONV_D - 1, W_D), f32),
                 jax.ShapeDtypeStruct((nseq, 1, W_D), f32)]
    return pl.pallas_call(
        functools.partial(_conv_lru_kernel, sps=sps, chunk=chunk),
        grid=(nseq // sps, nc), in_specs=in_specs, out_specs=out_specs, out_shape=out_shape,
        scratch_shapes=[pltpu.VMEM((sps, chunk + SUBLANES, W_B), f32), pltpu.VMEM((sps, chunk + SUBLANES, W_D), f32),
                        pltpu.VMEM((sps, 1, W_D), f32)],
        compiler_params=_params(("arbitrary", "arbitrary")), name="conv_lru",
    )(proj, proj, conv_b_buf, lru_buf, lru_h, wb, wd, bd, wa, ba, wx, bx, sp)


_HI = lax.Precision.HIGHEST


def _dot_nt(a, b, precision=None):
    return lax.dot_general(a, b, (((1,), (1,)), ((), ())), preferred_element_type=f32, precision=precision)


def _dot_tn(a, b):
    return lax.dot_general(a, b, (((0,), (0,)), ((), ())), preferred_element_type=f32)


def _hi_lo(x):
    hi = x.astype(bf16).astype(f32)
    return hi, x - hi


def _dot_f32(a, b, chunk):
    if chunk % 16:
        return jnp.dot(a, b, preferred_element_type=f32, precision=_HI)
    a_hi, a_lo = _hi_lo(a)
    b_hi, b_lo = _hi_lo(b)
    lhs = jnp.concatenate([a_hi, a_hi, a_lo], axis=1).astype(bf16)
    rhs = jnp.concatenate([b_hi, b_lo, b_hi], axis=0).astype(bf16)
    return jnp.dot(lhs, rhs, preferred_element_type=f32)


def _cumsum_rows(tri, x, chunk):
    if chunk % 16:
        return jnp.dot(tri, x, preferred_element_type=f32, precision=_HI)
    hi, r1 = _hi_lo(x)
    mid, lo = _hi_lo(r1)
    lhs = jnp.concatenate([tri, tri, tri], axis=1).astype(bf16)
    rhs = jnp.concatenate([hi, mid, lo], axis=0).astype(bf16)
    return jnp.dot(lhs, rhs, preferred_element_type=f32)


def _row_bcast(gc, pick, chunk):
    sel = pick.astype(f32)
    if chunk % 16:
        return _dot_nt(sel, gc, precision=_HI)
    hi, r1 = _hi_lo(gc)
    mid, lo = _hi_lo(r1)
    lhs = jnp.concatenate([sel, sel, sel], axis=1).astype(bf16)
    rhs = jnp.concatenate([hi, mid, lo], axis=1).astype(bf16)
    return _dot_nt(lhs, rhs)


def _deltanet_kernel(qkv_ref, z_ref, ab_ref, buf0_ref, s0_ref, wc_ref, nega_ref, dtb_ref, nw_ref,
                     y_ref, buf_out, s_out, f_sc, s_sc, *, sps, nsub, chunk):
    first = pl.program_id(1) == 0
    span_rows = nsub * chunk
    rows = sps * span_rows
    wq = 3 * W_C
    conv = _conv_step(f_sc, qkv_ref[...].reshape(sps, span_rows, wq), wc_ref, buf0_ref, buf_out, CONV_C,
                      span_rows, first)
    conv = conv.reshape(rows, wq)
    qkv = conv * _sigmoid(conv)

    @pl.when(first)
    def _():
        s_sc[...] = s0_ref[...]

    ab = ab_ref[...]
    xg = ab + dtb_ref[...]
    g_all = nega_ref[...] * (jnp.maximum(xg, 0.0) + jnp.log1p(jnp.exp(-jnp.abs(xg))))
    beta_all = _sigmoid(ab)
    z = z_ref[...]
    ri = lax.broadcasted_iota(jnp.int32, (chunk, chunk), 0)
    ci = lax.broadcasted_iota(jnp.int32, (chunk, chunk), 1)
    causal = ri >= ci
    strict = ri > ci
    tri = causal.astype(f32)
    eye = (ri == ci).astype(f32)
    lane = lax.broadcasted_iota(jnp.int32, (chunk, LANES), 1)
    nw = nw_ref[...]
    spans = [(i, c) for i in range(sps) for c in range(nsub)]
    units = [(i, c, h) for i, c in spans for h in range(H_C)]

    def rows_of(i, c):
        r0 = (i * nsub + c) * chunk
        return slice(r0, r0 + chunk)

    gcs = {sp: _cumsum_rows(tri, g_all[rows_of(*sp)], chunk) for sp in spans}

    qs, ks, gcols, rhss, qks, negs = [], [], [], [], [], []
    for i, c, h in units:
        rs = rows_of(i, c)
        gc = gcs[(i, c)]
        q = qkv[rs, h * DK_C:(h + 1) * DK_C]
        k = qkv[rs, W_C + h * DK_C:W_C + (h + 1) * DK_C]
        v = qkv[rs, 2 * W_C + h * DV_C:2 * W_C + (h + 1) * DV_C]
        q = q * lax.rsqrt(jnp.sum(q * q, axis=-1, keepdims=True) + EPS) * (DK_C ** -0.5)
        k = k * lax.rsqrt(jnp.sum(k * k, axis=-1, keepdims=True) + EPS)
        beta = beta_all[rs, H_C + h:H_C + h + 1]
        gcol = gc[:, h:h + 1]
        grow = _row_bcast(gc, lane == h, chunk)
        decay = jnp.where(causal, jnp.exp(jnp.where(causal, gcol - grow, 0.0)), 0.0)
        eg = jnp.exp(gcol)
        kb = k * beta
        negs.append(jnp.where(strict, _dot_nt(kb, k) * decay, 0.0) * -1.0)
        qks.append(_dot_nt(q, k) * decay)
        rhss.append(jnp.concatenate([v * beta, kb * eg], axis=1))
        qs.append(q * eg)
        ks.append(k)
        gcols.append(gcol)

    invs = [eye + n for n in negs]
    span = 2
    while span < chunk:
        negs = [_dot_f32(n, n, chunk) for n in negs]
        invs = [inv + _dot_f32(inv, n, chunk) for inv, n in zip(invs, negs)]
        span *= 2
    sols = [_dot_f32(inv, rhs, chunk) for inv, rhs in zip(invs, rhss)]

    states = {(i, h): s_sc[i, h] for i in range(sps) for h in range(H_C)}
    outs = {}
    for c in range(nsub):
        cur = [(idx, (i, h)) for idx, (i, cc, h) in enumerate(units) if cc == c]
        wq_s = {idx: jnp.dot(jnp.concatenate([sols[idx][:, DV_C:], qs[idx]], axis=0), states[key],
                             preferred_element_type=f32) for idx, key in cur}
        v_new = {idx: sols[idx][:, :DV_C] - wq_s[idx][:chunk] for idx, _ in cur}
        for idx, _ in cur:
            outs[idx] = wq_s[idx][chunk:] + jnp.dot(qks[idx], v_new[idx], preferred_element_type=f32)
        for idx, key in cur:
            gcol = gcols[idx]
            g_last = gcol[chunk - 1:chunk, :]
            states[key] = (states[key] * jnp.exp(g_last)
                           + _dot_tn(ks[idx] * jnp.exp(g_last - gcol), v_new[idx]))
    for (i, h), st in states.items():
        s_sc[i, h] = st
        s_out[i, h] = st
    for idx, (i, c, h) in enumerate(units):
        rs = rows_of(i, c)
        zh = z[rs, h * DV_C:(h + 1) * DV_C]
        o = outs[idx]
        o = o * lax.rsqrt(jnp.mean(o * o, axis=-1, keepdims=True) + EPS) * nw
        y_ref[rs, h * DV_C:(h + 1) * DV_C] = (o * (zh * _sigmoid(zh))).astype(y_ref.dtype)


def deltanet(proj, proj_ab, row0, nseq, t, sps, nsub, chunk, dn_buf, dn_s, wc, nega, dtb, nw):
    rows = sps * nsub * chunk
    nc = t // (nsub * chunk)
    rb0 = row0 // rows
    wq = 3 * W_C

    def rmap(s, c):
        return rb0 + s * nc + c

    def full(shape):
        return pl.BlockSpec(shape, lambda s, c: (0,) * len(shape))

    def per_seq(shape):
        return pl.BlockSpec((sps,) + shape, lambda s, c: (s,) + (0,) * len(shape))

    in_specs = [pl.BlockSpec((rows, wq), lambda s, c: (rmap(s, c), OFF_C // wq)),
                pl.BlockSpec((rows, W_C), lambda s, c: (rmap(s, c), (OFF_C + wq) // W_C)),
                pl.BlockSpec((rows, LANES), lambda s, c: (rmap(s, c), 0)),
                per_seq((CONV_C - 1, wq)), per_seq((H_C, DK_C, DV_C)),
                full((CONV_C, wq)), full((1, LANES)), full((1, LANES)), full((1, DV_C))]
    out_specs = [pl.BlockSpec((rows, W_C), lambda s, c: (s * nc + c, 0)),
                 per_seq((CONV_C - 1, wq)), per_seq((H_C, DK_C, DV_C))]
    out_shape = [jax.ShapeDtypeStruct((nseq * t, W_C), bf16),
                 jax.ShapeDtypeStruct((nseq, CONV_C - 1, wq), f32),
                 jax.ShapeDtypeStruct((nseq, H_C, DK_C, DV_C), f32)]
    return pl.pallas_call(
        functools.partial(_deltanet_kernel, sps=sps, nsub=nsub, chunk=chunk),
        grid=(nseq // sps, nc), in_specs=in_specs, out_specs=out_specs, out_shape=out_shape,
        scratch_shapes=[pltpu.VMEM((sps, nsub * chunk + SUBLANES, wq), f32),
                        pltpu.VMEM((sps, H_C, DK_C, DV_C), f32)],
        compiler_params=_params(("arbitrary", "arbitrary")), name="deltanet",
    )(proj, proj, proj_ab, dn_buf, dn_s, wc, nega, dtb, nw)


def _lane_row(v):
    return jnp.concatenate([v.astype(f32), jnp.zeros((LANES - v.shape[0],), f32)])[None, :]


def kernel(x_prompt, x_sample, cache_k, cache_v, state_conv_b, state_dn_conv, state_dn, state_lru_conv, state_lru_h, page_table, c_prompt, c_sample, w_mod, b_mod, norm_mix, norm_ffn, w_in, w_out, lam_q1, lam_k1, lam_q2, lam_k2, subln_w, conv_b_w, dn_conv_w, dn_a_log, dn_dt_bias, dn_norm_w, lru_conv_w, lru_conv_b, lru_wa, lru_ba, lru_wx, lru_bx, lru_lambda, ffn_w_gate, ffn_w_up, ffn_w_down, moe_router, moe_w_gate, moe_w_up, moe_w_down, final_norm_w):
    x = (x_prompt.reshape(N_PROMPT, D_MODEL), x_sample.reshape(N_SAMPLE, D_MODEL))
    c_all = jnp.concatenate([c_prompt, c_sample], axis=0)
    n_pool, page = cache_k.shape[1], cache_k.shape[2]
    ckt = jnp.transpose(cache_k, (0, 1, 3, 4, 5, 2)).reshape(DEPTH, n_pool, W_A, page)
    cvr = cache_v.reshape(DEPTH, n_pool, page * H_A, DV_A)
    slopes = jnp.exp2(-8.0 * jnp.arange(1, H_A + 1, dtype=f32) / H_A)

    zero_cb = jnp.zeros((BATCH, CONV_B - 1, W_B), f32)
    zero_dc = jnp.zeros((BATCH, CONV_C - 1, 3 * W_C), f32)
    zero_ds = jnp.zeros((BATCH, H_C, DK_C, DV_C), f32)
    zero_lc = jnp.zeros((BATCH, CONV_D - 1, W_D), f32)
    zero_lh = jnp.zeros((BATCH, 1, W_D), f32)

    ks_p, vs_p, ks_s, vs_s = [], [], [], []
    st_p = [[] for _ in range(5)]
    st_s = [[] for _ in range(5)]
    for l in range(DEPTH):
        mod = modulation(c_all, w_mod, b_mod[:, None, :], l)
        mod_p = jnp.transpose(mod[:BATCH].reshape(BATCH, N_MOD, D_MODEL), (1, 0, 2))[:, :, None, :]
        mod_s = jnp.transpose(mod[BATCH:].reshape(DEC_BATCH, N_MOD, D_MODEL), (1, 0, 2))

        hn = norm_mod(x, norm_mix[l][None, :], mod_p, mod_s, 1, 0)
        wl = w_in[l]
        n_cqkvz = 4 * W_C
        w_pack = jnp.concatenate([wl[:, :N_A + N_B + n_cqkvz], wl[:, N_A + N_B + N_C:]], axis=1).astype(bf16)
        w_ab = jnp.concatenate([wl[:, N_A + N_B + n_cqkvz:N_A + N_B + N_C],
                                jnp.zeros((D_MODEL, LANES - 2 * H_C), f32)], axis=1).astype(bf16)
        proj, proj_ab = in_proj(hn, w_pack, w_ab, tm=1088, tn=1024)

        lam_init = 0.8 - 0.6 * math.exp(-0.3 * l)
        lam = (jnp.exp(jnp.sum(lam_q1[l] * lam_k1[l])) - jnp.exp(jnp.sum(lam_q2[l] * lam_k2[l])) + lam_init)
        lam = lam.astype(f32).reshape(1)
        gain = 1.0 - lam_init
        sw = subln_w[l][None, :]
        ya = (attn_prompt(proj, lam, slopes, sw, gain, BATCH, SEQ, ATT_TQ, ATT_HPB),
              attn_sample(proj, ckt, cvr, l, page_table, lam, slopes, sw, gain, N_PROMPT, DEC_BATCH, DEC_SEQ))

        sp = jax.nn.softplus(-lru_lambda[l])[None, :]
        lru_args = (conv_b_w[l], lru_conv_w[l], lru_conv_b[l][None, :], lru_wa[l], lru_ba[l][None, :],
                    lru_wx[l], lru_bx[l][None, :], sp)
        yb_p, yd_p, cb_p, lb_p, lh_p = conv_lru(proj, 0, BATCH, SEQ, 1, LRU_CHUNK, zero_cb, zero_lc, zero_lh,
                                                *lru_args)
        yb_s, yd_s, cb_s, lb_s, lh_s = conv_lru(proj, N_PROMPT, DEC_BATCH, DEC_SEQ, LRU_SPS, DEC_SEQ,
                                                state_conv_b[l], state_lru_conv[l], state_lru_h[l][:, None, :],
                                                *lru_args)

        dn_args = (dn_conv_w[l], _lane_row(-jnp.exp(dn_a_log[l])), _lane_row(dn_dt_bias[l]), dn_norm_w[l][None, :])
        yc_p, dc_p, ds_p = deltanet(proj, proj_ab, 0, BATCH, SEQ, 1, DN_SUB_PROMPT, DN_CHUNK, zero_dc, zero_ds,
                                    *dn_args)
        yc_s, dc_s, ds_s = deltanet(proj, proj_ab, N_PROMPT, DEC_BATCH, DEC_SEQ, DN_SPS, 1,
                                    math.gcd(DEC_SEQ, DN_CHUNK), state_dn_conv[l], state_dn[l], *dn_args)

        k_new = proj[:, W_A:2 * W_A]
        v_new = proj[:, 2 * W_A:3 * W_A]
        ks_p.append(k_new[:N_PROMPT].reshape(BATCH, SEQ, H_A, 2, DH_A))
        vs_p.append(v_new[:N_PROMPT].reshape(BATCH, SEQ, H_A, DV_A))
        ks_s.append(k_new[N_PROMPT:].reshape(DEC_BATCH, DEC_SEQ, H_A, 2, DH_A))
        vs_s.append(v_new[N_PROMPT:].reshape(DEC_BATCH, DEC_SEQ, H_A, DV_A))
        for lst, vals in ((st_p, (cb_p, dc_p, ds_p, lb_p, lh_p[:, 0, :])), (st_s, (cb_s, dc_s, ds_s, lb_s, lh_s[:, 0, :]))):
            for i, val in enumerate(vals):
                lst[i].append(val)

        x = mix_out_residual((ya, (yb_p, yb_s), (yc_p, yc_s), (yd_p, yd_s)), w_out[l].astype(bf16), x,
                             mod_p, mod_s, 2, tm=1024, tn=1024)

        j = l // 2
        if l % 2 == 0:
            hn = norm_mod(x, norm_ffn[l][None, :], mod_p, mod_s, 4, 3)
            h = swiglu_up(hn, ffn_w_gate[j].astype(bf16), ffn_w_up[j].astype(bf16), tm=1088, tn=512)
            x = matmul_gated_residual(h, ffn_w_down[j].astype(bf16), x, mod_p, mod_s, 5, tm=1024, tn=256, tk=D_FF)
        else:
            router = jnp.concatenate([moe_router[j], jnp.zeros((D_MODEL, LANES - N_EXPERTS), f32)], axis=1)
            hn, logits = norm_mod(x, norm_ffn[l][None, :], mod_p, mod_s, 4, 3, router=router)
            plan = moe_route(logits[:, :N_EXPERTS], D_FF_EXPERT // MOE_TN)
            xs = jnp.take(hn, plan['row_src'], axis=0, mode='clip')
            h, wd_bf = moe_up(plan['item_m'], plan['item_n'], plan['item_wn'], plan['item_e'], plan['item_v'], xs,
                              moe_w_gate[j], moe_w_up[j], moe_w_down[j])
            ys = moe_down(plan['tile_e'], plan['n_used'], h, wd_bf)
            gates = plan['gates']
            dest = plan['dest']
            f = (gates[:, 0:1] * jnp.take(ys, dest[:, 0], axis=0, mode='clip')
                 + gates[:, 1:2] * jnp.take(ys, dest[:, 1], axis=0, mode='clip'))
            g2_tok = jnp.concatenate([jnp.repeat(mod_p[5, :, 0, :], SEQ, axis=0),
                                      jnp.repeat(mod_s[5], DEC_SEQ, axis=0)], axis=0)
            x = x + g2_tok * f

    y_prompt, y_sample = final_norm(x, final_norm_w[None, :])
    y_prompt = y_prompt.reshape(BATCH, SEQ, D_MODEL)
    y_sample = y_sample.reshape(DEC_BATCH, DEC_SEQ, D_MODEL)
    sp_ = [jnp.stack(o) for o in st_p]
    ss_ = [jnp.stack(o) for o in st_s]
    return (y_prompt, y_sample, jnp.stack(ks_p), jnp.stack(vs_p), jnp.stack(ks_s), jnp.stack(vs_s),
            sp_[0], ss_[0], sp_[1], ss_[1], sp_[2], ss_[2], sp_[3], ss_[3], sp_[4], ss_[4])
```

```python
import functools
import math

import jax
import jax.numpy as jnp
from jax import lax
from jax.experimental import pallas as pl
from jax.experimental.pallas import tpu as pltpu

D_MODEL = 2048
BATCH = 8
SEQ = 2048
DEPTH = 2
DEC_BATCH = 128
DEC_SEQ = 8
W_GROUP = D_MODEL // 4
W_A = W_B = W_C = W_D = W_GROUP
H_A = 4
DH_A = W_A // (2 * H_A)
DV_A = 2 * DH_A
Q_BLOCK = 128
CONV_B = 3
H_C = 4
DK_C = W_C // H_C
DV_C = W_C // H_C
CONV_C = 4
DN_CHUNK = 64
H_D = 4
BW_D = W_D // H_D
CONV_D = 4
RG_C = 8.0
D_FF = 5632
N_EXPERTS = 8
TOP_K = 2
D_FF_EXPERT = 7168
N_MOD = 6
N_MIX = 4
EPS = 1e-6
N_A = 3 * W_A
N_B = 3 * W_B
N_C = 4 * W_C + 2 * H_C
N_D = 2 * W_D

N_PROMPT = BATCH * SEQ
N_SAMPLE = DEC_BATCH * DEC_SEQ
N_TOK = N_PROMPT + N_SAMPLE
SUBLANES = 8
LANES = 128

OFF_A = 0
OFF_B = N_A
OFF_C = N_A + N_B
OFF_D = OFF_C + 4 * W_C
N_PROJ = OFF_D + N_D

VMEM_LIMIT = 48 * 1024 * 1024
MOE_TM = 1024
MOE_TN = 512
MOE_DN_TN = 256
MOE_CAST_ROWS = 224
MOE_TILES = (TOP_K * N_TOK) // MOE_TM + N_EXPERTS
MOE_ROWS = MOE_TILES * MOE_TM
ROW_TILE = 512
ATT_TQ = 512
ATT_HPB = 2
LRU_CHUNK = 256
LRU_SPS = 16
DN_SPS = 8
DN_SUB_PROMPT = 4

f32 = jnp.float32
bf16 = jnp.bfloat16


def _params(sem):
    return pltpu.CompilerParams(dimension_semantics=sem, vmem_limit_bytes=VMEM_LIMIT)


def _silu(x):
    return x * (1.0 / (1.0 + jnp.exp(-x)))


def _mod_kernel(c_ref, w_ref, b_ref, o_ref):
    c = _silu(c_ref[...]).astype(bf16)
    o_ref[...] = jnp.dot(c, w_ref[0].astype(bf16), preferred_element_type=f32) + b_ref[0]


def modulation(c_all, w, b, layer):
    r = c_all.shape[0]
    n = w.shape[2]
    tn = 1024
    return pl.pallas_call(
        _mod_kernel,
        grid=(n // tn,),
        in_specs=[pl.BlockSpec((r, D_MODEL), lambda j: (0, 0)),
                  pl.BlockSpec((1, D_MODEL, tn), lambda j: (layer, 0, j)),
                  pl.BlockSpec((1, 1, tn), lambda j: (layer, 0, j))],
        out_specs=pl.BlockSpec((r, tn), lambda j: (0, j)),
        out_shape=jax.ShapeDtypeStruct((r, n), f32),
        compiler_params=_params(("arbitrary",)),
        name="modulation",
    )(c_all, w, b)


def _mod_specs(idx, tile_rows, col_block):
    tiles_per_req = SEQ // tile_rows
    npt = N_PROMPT // tile_rows

    def p_map(*ids):
        return (idx, jnp.minimum(ids[0] // tiles_per_req, BATCH - 1), 0, col_block(*ids))

    def s_map(*ids):
        return (idx, jnp.maximum(ids[0] - npt, 0), col_block(*ids))

    return p_map, s_map


def _row_pair(x, tile_rows, cols, col_block):
    npt = N_PROMPT // tile_rows
    if isinstance(x, tuple):
        xp, xs = x
        s_map = lambda *ids: (jnp.maximum(ids[0] - npt, 0), col_block(*ids))
    else:
        xp = xs = x
        s_map = lambda *ids: (jnp.maximum(ids[0], npt), col_block(*ids))
    p_map = lambda *ids: (jnp.minimum(ids[0], npt - 1), col_block(*ids))
    return (xp, xs), [pl.BlockSpec((tile_rows, cols), p_map), pl.BlockSpec((tile_rows, cols), s_map)]


def _apply_rows(vals, i, tile_rows, fn_prompt, fn_sample):
    npt = N_PROMPT // tile_rows

    @pl.when(i < npt)
    def _():
        fn_prompt(*vals)

    @pl.when(i >= npt)
    def _():
        fn_sample(*vals)


def _bcast_groups(a, m):
    rows, c = a.shape
    return a.reshape(rows // SUBLANES, SUBLANES, c), m[:, None, :]


def _norm_mod_emit(xp_ref, xs_ref, nw_ref, scp_ref, shp_ref, scs_ref, shs_ref, emit):
    tr = xp_ref.shape[0]

    def normed(x):
        return x * lax.rsqrt(jnp.mean(x * x, axis=-1, keepdims=True) + EPS) * nw_ref[...]

    def prompt():
        emit(normed(xp_ref[...]) * (1.0 + scp_ref[0, 0]) + shp_ref[0, 0])

    def sample():
        y3, sc = _bcast_groups(normed(xs_ref[...]), scs_ref[0])
        emit((y3 * (1.0 + sc) + shs_ref[0][:, None, :]).reshape(tr, D_MODEL))

    _apply_rows((), pl.program_id(0), tr, prompt, sample)


def _norm_mod_kernel(xp_ref, xs_ref, nw_ref, scp_ref, shp_ref, scs_ref, shs_ref, o_ref):
    def emit(hn):
        o_ref[...] = hn.astype(o_ref.dtype)

    _norm_mod_emit(xp_ref, xs_ref, nw_ref, scp_ref, shp_ref, scs_ref, shs_ref, emit)


def _norm_mod_router_kernel(xp_ref, xs_ref, nw_ref, scp_ref, shp_ref, scs_ref, shs_ref, r_ref, o_ref, lg_ref):
    def emit(hn):
        bits = lax.bitcast_convert_type(hn.astype(bf16).astype(f32), jnp.uint32)
        half = D_MODEL // 2
        o_ref[...] = (bits[:, :half] >> 16) | bits[:, half:]
        lg_ref[...] = jnp.dot(hn, r_ref[...], preferred_element_type=f32, precision=lax.Precision.HIGHEST)

    _norm_mod_emit(xp_ref, xs_ref, nw_ref, scp_ref, shp_ref, scs_ref, shs_ref, emit)


def norm_mod(x, nw, mod_p, mod_s, i_scale, i_shift, router=None):
    tr = ROW_TILE
    tg = tr // SUBLANES
    zero = lambda i: 0
    scp_map, scs_map = _mod_specs(i_scale, tr, zero)
    shp_map, shs_map = _mod_specs(i_shift, tr, zero)
    x_ops, x_specs = _row_pair(x, tr, D_MODEL, zero)
    in_specs = x_specs + [
                pl.BlockSpec((1, D_MODEL), lambda i: (0, 0)),
                pl.BlockSpec((1, 1, 1, D_MODEL), scp_map), pl.BlockSpec((1, 1, 1, D_MODEL), shp_map),
                pl.BlockSpec((1, tg, D_MODEL), scs_map), pl.BlockSpec((1, tg, D_MODEL), shs_map)]
    out_spec = pl.BlockSpec((tr, D_MODEL), lambda i: (i, 0))
    out_shape = jax.ShapeDtypeStruct((N_TOK, D_MODEL), bf16)
    if router is not None:
        out_spec = pl.BlockSpec((tr, D_MODEL // 2), lambda i: (i, 0))
        out_shape = jax.ShapeDtypeStruct((N_TOK, D_MODEL // 2), jnp.uint32)
    if router is None:
        return pl.pallas_call(
            _norm_mod_kernel, grid=(N_TOK // tr,), in_specs=in_specs, out_specs=out_spec,
            out_shape=out_shape, compiler_params=_params(("arbitrary",)), name="norm_mod",
        )(*x_ops, nw, mod_p, mod_p, mod_s, mod_s)
    return pl.pallas_call(
        _norm_mod_router_kernel, grid=(N_TOK // tr,),
        in_specs=in_specs + [pl.BlockSpec((D_MODEL, LANES), lambda i: (0, 0))],
        out_specs=[out_spec, pl.BlockSpec((tr, LANES), lambda i: (i, 0))],
        out_shape=[out_shape, jax.ShapeDtypeStruct((N_TOK, LANES), f32)],
        compiler_params=_params(("arbitrary",)), name="norm_mod_router",
    )(*x_ops, nw, mod_p, mod_p, mod_s, mod_s, router)


def _final_norm_kernel(x_ref, nw_ref, op_ref, os_ref, *, n_prompt_tiles):
    x = x_ref[...]
    y = x * lax.rsqrt(jnp.mean(x * x, axis=-1, keepdims=True) + EPS) * nw_ref[...]
    i = pl.program_id(0)

    @pl.when(i < n_prompt_tiles)
    def _():
        op_ref[...] = y

    @pl.when(i >= n_prompt_tiles)
    def _():
        os_ref[...] = y


def final_norm(x, nw):
    tr = ROW_TILE
    npt = N_PROMPT // tr
    return pl.pallas_call(
        functools.partial(_final_norm_kernel, n_prompt_tiles=npt), grid=(N_TOK // tr,),
        in_specs=[pl.BlockSpec((tr, D_MODEL), lambda i: (i, 0)),
                  pl.BlockSpec((1, D_MODEL), lambda i: (0, 0))],
        out_specs=[pl.BlockSpec((tr, D_MODEL), lambda i: (jnp.minimum(i, npt - 1), 0)),
                   pl.BlockSpec((tr, D_MODEL), lambda i: (jnp.maximum(i - npt, 0), 0))],
        out_shape=[jax.ShapeDtypeStruct((N_PROMPT, D_MODEL), f32), jax.ShapeDtypeStruct((N_SAMPLE, D_MODEL), f32)],
        compiler_params=_params(("arbitrary",)), name="final_norm",
    )(x, nw)


def _in_proj_kernel(x_ref, w_ref, wab_ref, o_ref, oab_ref):
    x = x_ref[...]
    o_ref[...] = jnp.dot(x, w_ref[...], preferred_element_type=f32)

    @pl.when(pl.program_id(1) == 0)
    def _():
        oab_ref[...] = jnp.dot(x, wab_ref[...], preferred_element_type=f32)


def in_proj(x, w, w_ab, tm, tn):
    m, k = x.shape
    n = w.shape[1]
    return pl.pallas_call(
        _in_proj_kernel, grid=(m // tm, n // tn),
        in_specs=[pl.BlockSpec((tm, k), lambda i, j: (i, 0)),
                  pl.BlockSpec((k, tn), lambda i, j: (0, j)),
                  pl.BlockSpec((k, LANES), lambda i, j: (0, 0))],
        out_specs=[pl.BlockSpec((tm, tn), lambda i, j: (i, j)), pl.BlockSpec((tm, LANES), lambda i, j: (i, 0))],
        out_shape=[jax.ShapeDtypeStruct((m, n), f32), jax.ShapeDtypeStruct((m, LANES), f32)],
        compiler_params=_params(("arbitrary", "arbitrary")), name="in_proj",
    )(x, w, w_ab)


def _gated_residual_store(acc, xp_ref, xs_ref, gp_ref, gs_ref, o_ref):
    tm, tn = acc.shape

    def prompt(acc):
        o_ref[...] = xp_ref[...] + acc * gp_ref[0, 0]

    def sample(acc):
        a3, g = _bcast_groups(acc, gs_ref[0])
        o_ref[...] = xs_ref[...] + (a3 * g).reshape(tm, tn)

    _apply_rows((acc,), pl.program_id(0), tm, prompt, sample)


def _mm_resid_kernel(y_ref, w_ref, xp_ref, xs_ref, gp_ref, gs_ref, o_ref, acc_ref, *, nk):
    k = pl.program_id(2)
    part = jnp.dot(y_ref[...], w_ref[...], preferred_element_type=f32)

    @pl.when(k == 0)
    def _():
        acc_ref[...] = part

    @pl.when(k > 0)
    def _():
        acc_ref[...] += part

    @pl.when(k == nk - 1)
    def _():
        _gated_residual_store(acc_ref[...], xp_ref, xs_ref, gp_ref, gs_ref, o_ref)


def matmul_gated_residual(y, w, x, mod_p, mod_s, i_gate, tm, tn, tk):
    m, kdim = y.shape
    n = w.shape[1]
    nk = kdim // tk
    gp_map, gs_map = _mod_specs(i_gate, tm, lambda i, j, k: j)
    x_ops, x_specs = _row_pair(x, tm, tn, lambda i, j, k: j)
    return pl.pallas_call(
        functools.partial(_mm_resid_kernel, nk=nk), grid=(m // tm, n // tn, nk),
        in_specs=[pl.BlockSpec((tm, tk), lambda i, j, k: (i, k)),
                  pl.BlockSpec((tk, tn), lambda i, j, k: (k, j)),
                  *x_specs,
                  pl.BlockSpec((1, 1, 1, tn), gp_map),
                  pl.BlockSpec((1, tm // SUBLANES, tn), gs_map)],
        out_specs=pl.BlockSpec((tm, tn), lambda i, j, k: (i, j)),
        out_shape=jax.ShapeDtypeStruct((m, n), f32),
        scratch_shapes=[pltpu.VMEM((tm, tn), f32)],
        compiler_params=_params(("arbitrary", "arbitrary", "arbitrary")), name="matmul_gated_residual",
    )(y, w, *x_ops, mod_p, mod_s)


def _mix_out_kernel(*refs):
    y_refs, (w_ref, xp_ref, xs_ref, gp_ref, gs_ref, o_ref) = refs[:2 * N_MIX], refs[2 * N_MIX:]
    tm = o_ref.shape[0]

    def project(group_refs):
        acc = jnp.dot(group_refs[0][...], w_ref[0:W_GROUP, :], preferred_element_type=f32)
        for gi in range(1, N_MIX):
            acc = acc + jnp.dot(group_refs[gi][...], w_ref[gi * W_GROUP:(gi + 1) * W_GROUP, :],
                                preferred_element_type=f32)
        return acc

    def prompt():
        o_ref[...] = xp_ref[...] + project(y_refs[0::2]) * gp_ref[0, 0]

    def sample():
        a3, g = _bcast_groups(project(y_refs[1::2]), gs_ref[0])
        o_ref[...] = xs_ref[...] + (a3 * g).reshape(o_ref.shape)

    _apply_rows((), pl.program_id(0), tm, prompt, sample)


def mix_out_residual(ys, w, x, mod_p, mod_s, i_gate, tm, tn):
    m, n = N_TOK, D_MODEL
    gp_map, gs_map = _mod_specs(i_gate, tm, lambda i, j: j)
    x_ops, x_specs = _row_pair(x, tm, tn, lambda i, j: j)
    y_ops, y_specs = [], []
    for pair in ys:
        ops, specs = _row_pair(pair, tm, W_GROUP, lambda i, j: 0)
        y_ops += ops
        y_specs += specs
    return pl.pallas_call(
        _mix_out_kernel, grid=(m // tm, n // tn),
        in_specs=[*y_specs,
                  pl.BlockSpec((D_MODEL, tn), lambda i, j: (0, j)),
                  *x_specs,
                  pl.BlockSpec((1, 1, 1, tn), gp_map),
                  pl.BlockSpec((1, tm // SUBLANES, tn), gs_map)],
        out_specs=pl.BlockSpec((tm, tn), lambda i, j: (i, j)),
        out_shape=jax.ShapeDtypeStruct((m, n), f32),
        compiler_params=_params(("arbitrary", "arbitrary")), name="mix_out_residual",
    )(*y_ops, w, *x_ops, mod_p, mod_s)


def _swiglu_body(x, wg, wu):
    g = jnp.dot(x, wg, preferred_element_type=f32)
    u = jnp.dot(x, wu, preferred_element_type=f32)
    return _silu(g) * u


def _swiglu_kernel(x_ref, wg_ref, wu_ref, o_ref):
    o_ref[...] = _swiglu_body(x_ref[...], wg_ref[...], wu_ref[...]).astype(o_ref.dtype)


def swiglu_up(x, wg, wu, tm, tn):
    m, k = x.shape
    n = wg.shape[1]
    return pl.pallas_call(
        _swiglu_kernel, grid=(m // tm, n // tn),
        in_specs=[pl.BlockSpec((tm, k), lambda i, j: (i, 0)),
                  pl.BlockSpec((k, tn), lambda i, j: (0, j)),
                  pl.BlockSpec((k, tn), lambda i, j: (0, j))],
        out_specs=pl.BlockSpec((tm, tn), lambda i, j: (i, j)),
        out_shape=jax.ShapeDtypeStruct((m, n), bf16),
        compiler_params=_params(("arbitrary", "arbitrary")), name="swiglu_up",
    )(x, wg, wu)


def _unpack_tokens(words):
    lo = lax.bitcast_convert_type(words << 16, f32)
    hi = lax.bitcast_convert_type(words & jnp.uint32(0xFFFF0000), f32)
    return jnp.concatenate([lo, hi], axis=1).astype(bf16)


def _moe_up_kernel(m_ref, n_ref, wn_ref, e_ref, v_ref, x_ref, wg_ref, wu_ref, wd_ref, o_ref, wdb_ref, wg_sc, wu_sc,
                   *, n_cast):
    i = pl.program_id(0)

    @pl.when(i < n_cast)
    def _():
        wdb_ref[...] = wd_ref[...].astype(bf16)

    @pl.when(v_ref[i] == 2)
    def _():
        wg_sc[...] = wg_ref[0].astype(bf16)
        wu_sc[...] = wu_ref[0].astype(bf16)

    @pl.when(v_ref[i] > 0)
    def _():
        o_ref[...] = _swiglu_body(_unpack_tokens(x_ref[...]), wg_sc[...], wu_sc[...]).astype(o_ref.dtype)

    @pl.when(v_ref[i] == 0)
    def _():
        o_ref[...] = jnp.zeros(o_ref.shape, o_ref.dtype)


def moe_up(item_m, item_n, item_wn, item_e, item_v, xs, wg, wu, wd):
    ff = wg.shape[2]
    nt = ff // MOE_TN
    n_items = MOE_TILES * nt
    wd_rows = wd.shape[0] * wd.shape[1]
    n_cast = wd_rows // MOE_CAST_ROWS
    assert n_cast <= (TOP_K * N_TOK // MOE_TM) * nt and wd_rows % MOE_CAST_ROWS == 0
    cast_spec = pl.BlockSpec((MOE_CAST_ROWS, D_MODEL), lambda i, m, n, wn, e, v: (jnp.minimum(i, n_cast - 1), 0))
    grid_spec = pltpu.PrefetchScalarGridSpec(
        num_scalar_prefetch=5, grid=(n_items,),
        in_specs=[pl.BlockSpec((MOE_TM, D_MODEL // 2), lambda i, m, n, wn, e, v: (m[i], 0)),
                  pl.BlockSpec((1, D_MODEL, MOE_TN), lambda i, m, n, wn, e, v: (e[i], 0, wn[i])),
                  pl.BlockSpec((1, D_MODEL, MOE_TN), lambda i, m, n, wn, e, v: (e[i], 0, wn[i])),
                  cast_spec],
        out_specs=[pl.BlockSpec((MOE_TM, MOE_TN), lambda i, m, n, wn, e, v: (m[i], n[i])), cast_spec],
        scratch_shapes=[pltpu.VMEM((D_MODEL, MOE_TN), bf16), pltpu.VMEM((D_MODEL, MOE_TN), bf16)])
    h, wd_bf = pl.pallas_call(
        functools.partial(_moe_up_kernel, n_cast=n_cast), grid_spec=grid_spec,
        out_shape=[jax.ShapeDtypeStruct((MOE_ROWS, ff), bf16), jax.ShapeDtypeStruct((wd_rows, D_MODEL), bf16)],
        compiler_params=_params(("arbitrary",)), name="moe_up",
    )(item_m, item_n, item_wn, item_e, item_v, xs, wg, wu, wd.reshape(wd_rows, D_MODEL))
    return h, wd_bf.reshape(wd.shape)


def _moe_down_kernel(te_ref, nu_ref, h_ref, wd_ref, o_ref):
    m = pl.program_id(0)

    @pl.when(m < nu_ref[0])
    def _():
        o_ref[...] = jnp.dot(h_ref[...], wd_ref[0], preferred_element_type=f32)

    @pl.when(m >= nu_ref[0])
    def _():
        o_ref[...] = jnp.zeros(o_ref.shape, o_ref.dtype)


def moe_down(tile_e, n_used, h, wd):
    ff = wd.shape[1]
    nn = D_MODEL // MOE_DN_TN

    def mclamp(m, nu):
        return jnp.minimum(m, nu[0] - 1)

    def nclamp(m, n, nu):
        return jnp.where(m < nu[0], n, nn - 1)

    grid_spec = pltpu.PrefetchScalarGridSpec(
        num_scalar_prefetch=2, grid=(MOE_TILES, nn),
        in_specs=[pl.BlockSpec((MOE_TM, ff), lambda m, n, te, nu: (mclamp(m, nu), 0)),
                  pl.BlockSpec((1, ff, MOE_DN_TN), lambda m, n, te, nu: (te[mclamp(m, nu)], 0, nclamp(m, n, nu)))],
        out_specs=pl.BlockSpec((MOE_TM, MOE_DN_TN), lambda m, n, te, nu: (m, n)))
    return pl.pallas_call(
        _moe_down_kernel, grid_spec=grid_spec,
        out_shape=jax.ShapeDtypeStruct((MOE_ROWS, D_MODEL), f32),
        compiler_params=_params(("arbitrary", "arbitrary")), name="moe_down",
    )(tile_e, n_used, h, wd)


def moe_route(logits, n_ff_tiles):
    top_v, top_i = lax.top_k(logits, TOP_K)
    gates = jax.nn.softmax(top_v, axis=-1)
    e_flat = top_i.reshape(-1).astype(jnp.int32)
    onehot = (e_flat[:, None] == jnp.arange(N_EXPERTS, dtype=jnp.int32)[None, :]).astype(jnp.int32)
    csum = jnp.cumsum(onehot, axis=0)
    counts = csum[-1]
    rank = jnp.sum((csum - onehot) * onehot, axis=-1)
    tiles_e = (counts + MOE_TM - 1) // MOE_TM
    tile_end = jnp.cumsum(tiles_e)
    tile_start = tile_end - tiles_e
    dest = tile_start[e_flat] * MOE_TM + rank
    tok = jnp.arange(TOP_K * N_TOK, dtype=jnp.int32) // TOP_K
    row_src = (jnp.arange(MOE_ROWS, dtype=jnp.int32) % N_TOK).at[dest].set(tok)
    n_used = tile_end[-1]
    t_ids = jnp.arange(MOE_TILES, dtype=jnp.int32)
    tile_e = jnp.minimum(jnp.searchsorted(tile_end, t_ids, side="right"), N_EXPERTS - 1).astype(jnp.int32)
    nt = n_ff_tiles
    items_e = tiles_e * nt
    item_end = jnp.cumsum(items_e)
    item_start = item_end - items_e
    total = item_end[-1]
    idx = jnp.arange(MOE_TILES * nt, dtype=jnp.int32)
    valid = idx < total
    idc = jnp.minimum(idx, total - 1)
    ie = jnp.minimum(jnp.searchsorted(item_end, idc, side="right"), N_EXPERTS - 1).astype(jnp.int32)
    local = idc - item_start[ie]
    te = jnp.maximum(tiles_e[ie], 1)
    item_wn = (local // te).astype(jnp.int32)
    spare = idx - total
    item_n = jnp.where(valid, item_wn, spare % nt).astype(jnp.int32)
    item_m = jnp.where(valid, tile_start[ie] + local % te, n_used + spare // nt).astype(jnp.int32)
    return dict(gates=gates, dest=dest.reshape(N_TOK, TOP_K), row_src=row_src, tile_e=tile_e,
                n_used=n_used.reshape(1).astype(jnp.int32), item_m=item_m, item_n=item_n, item_wn=item_wn, item_e=ie,
                item_v=jnp.where(valid, jnp.where(local % te == 0, 2, 1), 0).astype(jnp.int32))


def _smem_spec():
    return pl.BlockSpec(memory_space=pltpu.SMEM)


def _subln(o, w, gain):
    return o * lax.rsqrt(jnp.mean(o * o, axis=-1, keepdims=True) + EPS) * w * gain


def _attn_prompt_kernel(lam_ref, slope_ref, q_ref, k_ref, v_ref, w_ref, o_ref, m_sc, acc_sc, *, tq, hpb, gain):
    hb = pl.program_id(1)
    qi = pl.program_id(2)
    lane = lax.broadcasted_iota(jnp.int32, (tq, DV_A), 1)
    row = lax.broadcasted_iota(jnp.int32, (2 * tq, tq), 0)
    col = lax.broadcasted_iota(jnp.int32, (2 * tq, tq), 1)
    rel = (col - jnp.where(row >= tq, row - tq, row)).astype(f32)
    ones_v = jnp.ones((tq, DV_A), bf16)
    qss, slopes, biases = [], [], []
    for hh in range(hpb):
        q = q_ref[:, hh * DV_A:(hh + 1) * DV_A] * (DH_A ** -0.5)
        qss.append(jnp.concatenate([jnp.where(lane < DH_A, q, 0.0), jnp.where(lane >= DH_A, q, 0.0)],
                                   axis=0).astype(bf16))
        slopes.append(slope_ref[hb * hpb + hh])
        biases.append(slopes[hh] * rel)
    m_sc[...] = jnp.full(m_sc.shape, -jnp.inf, f32)
    acc_sc[...] = jnp.zeros(acc_sc.shape, f32)

    def chunk_step(j, diagonal):
        off = pl.multiple_of(j * tq, tq)
        blocks = (jnp.zeros((1, 1), jnp.int32) + (qi - j) * tq).astype(f32)
        for hh in range(hpb):
            cs = slice(hh * DV_A, (hh + 1) * DV_A)
            kc = k_ref[pl.ds(off, tq), cs].astype(bf16)
            vc = jnp.concatenate([v_ref[pl.ds(off, tq), cs].astype(bf16), ones_v], axis=1)
            s = lax.dot_general(qss[hh], kc, (((1,), (1,)), ((), ())), preferred_element_type=f32)
            s = s + biases[hh] - slopes[hh] * blocks
            if diagonal:
                s = jnp.where(rel <= 0, s, -1e30)
            m_prev = m_sc[hh]
            m_new = jnp.maximum(m_prev, jnp.max(s, axis=-1, keepdims=True))
            pr = jnp.exp(s - m_new).astype(bf16)
            acc_sc[hh] = jnp.exp(m_prev - m_new) * acc_sc[hh] + jnp.dot(pr, vc, preferred_element_type=f32)
            m_sc[hh] = m_new

    def body(j, carry):
        chunk_step(j, False)
        return carry

    lax.fori_loop(0, qi, body, 0)
    chunk_step(qi, True)
    for hh in range(hpb):
        acc = acc_sc[hh]
        o = acc[:, :DV_A] / acc[:, DV_A:]
        o = o[:tq] - lam_ref[0] * o[tq:]
        o_ref[:, hh * DV_A:(hh + 1) * DV_A] = _subln(o, w_ref[...], gain).astype(o_ref.dtype)


def attn_prompt(proj, lam, slopes, subln_w, gain, batch, seq, tq, hpb):
    nq = seq // tq
    wb = hpb * DV_A
    nhb = H_A // hpb
    return pl.pallas_call(
        functools.partial(_attn_prompt_kernel, tq=tq, hpb=hpb, gain=gain),
        grid=(batch, nhb, nq),
        in_specs=[_smem_spec(), _smem_spec(),
                  pl.BlockSpec((tq, wb), lambda b, h, qi: (b * nq + qi, h)),
                  pl.BlockSpec((seq, wb), lambda b, h, qi: (b, nhb + h)),
                  pl.BlockSpec((seq, wb), lambda b, h, qi: (b, 2 * nhb + h)),
                  pl.BlockSpec((1, DV_A), lambda b, h, qi: (0, 0))],
        out_specs=pl.BlockSpec((tq, wb), lambda b, h, qi: (b * nq + qi, h)),
        out_shape=jax.ShapeDtypeStruct((batch * seq, W_A), bf16),
        scratch_shapes=[pltpu.VMEM((hpb, 2 * tq, 1), f32), pltpu.VMEM((hpb, 2 * tq, 2 * DV_A), f32)],
        compiler_params=_params(("arbitrary", "arbitrary", "arbitrary")), name="attn_prompt",
    )(lam, slopes, proj, proj, proj, subln_w)


def _attn_sample_kernel(pt_ref, lam_ref, slope_ref, q_ref, kn_ref, vn_ref, w_ref, *rest, t, n_pages, page, gain):
    k_refs = rest[:n_pages]
    v_refs = rest[n_pages:2 * n_pages]
    o_ref = rest[2 * n_pages]
    s_sc = rest[2 * n_pages + 1]
    n_rep = 2 * H_A
    rows = n_rep * t
    n_past = n_pages * page
    q = q_ref[...] * (DH_A ** -0.5)
    q_rep = jnp.concatenate([q] * n_rep, axis=0)
    r_grp = lax.broadcasted_iota(jnp.int32, (rows, W_A), 0) // t
    c_grp = lax.broadcasted_iota(jnp.int32, (rows, W_A), 1) // DH_A
    qbd = jnp.where(r_grp == c_grp, q_rep, 0.0).astype(bf16)
    for j in range(n_pages):
        s_sc[:, j * page:(j + 1) * page] = jnp.dot(qbd, k_refs[j][0, 0].astype(bf16), preferred_element_type=f32)
    s_new = lax.dot_general(qbd, kn_ref[...].astype(bf16), (((1,), (1,)), ((), ())), preferred_element_type=f32)

    row1 = lax.broadcasted_iota(jnp.int32, (rows, 1), 0)
    head1 = row1 // (2 * t)
    slope = jnp.where(head1 == 0, slope_ref[0],
                      jnp.where(head1 == 1, slope_ref[1], jnp.where(head1 == 2, slope_ref[2], slope_ref[3])))
    t_row = (row1 % t)
    kpos = lax.broadcasted_iota(jnp.int32, (rows, n_past), 1)
    s_past = s_sc[...] - slope * (n_past + t_row - kpos).astype(f32)
    dist_new = (t_row - lax.broadcasted_iota(jnp.int32, (rows, t), 1)).astype(f32)
    s_new = jnp.where(dist_new >= 0, s_new - slope * dist_new, -1e30)
    m = jnp.maximum(jnp.max(s_past, axis=-1, keepdims=True), jnp.max(s_new, axis=-1, keepdims=True))
    p_past = jnp.exp(s_past - m)
    p_new = jnp.exp(s_new - m)
    denom = jnp.sum(p_past, axis=-1, keepdims=True) + jnp.sum(p_new, axis=-1, keepdims=True)
    p_past = p_past.astype(bf16)
    p_new = p_new.astype(bf16)
    lam = lam_ref[0]
    w = w_ref[...]
    for h in range(H_A):
        vh = jnp.concatenate([v_refs[j][0, 0, pl.ds(h, page, stride=H_A), :] for j in range(n_pages)], axis=0)
        rs = slice(h * 2 * t, (h + 1) * 2 * t)
        oh = jnp.dot(p_past[rs], vh.astype(bf16), preferred_element_type=f32)
        oh = oh + jnp.dot(p_new[rs], vn_ref[:, h * DV_A:(h + 1) * DV_A].astype(bf16), preferred_element_type=f32)
        oh = oh / denom[rs]
        o = oh[:t] - lam * oh[t:]
        o_ref[:, h * DV_A:(h + 1) * DV_A] = _subln(o, w, gain).astype(o_ref.dtype)


def attn_sample(proj, ckt, cvr, layer, page_table, lam, slopes, subln_w, gain, row0, nseq, t):
    n_pages = page_table.shape[1]
    page = ckt.shape[-1]
    rb0 = row0 // t

    def kv_spec(j):
        return pl.BlockSpec((1, 1, W_A, page), lambda b, pt, j=j: (layer, pt[b * n_pages + j], 0, 0))

    in_specs = ([_smem_spec(), _smem_spec(),
                 pl.BlockSpec((t, W_A), lambda b, pt: (rb0 + b, 0)),
                 pl.BlockSpec((t, W_A), lambda b, pt: (rb0 + b, 1)),
                 pl.BlockSpec((t, W_A), lambda b, pt: (rb0 + b, 2)),
                 pl.BlockSpec((1, DV_A), lambda b, pt: (0, 0))]
                + [kv_spec(j) for j in range(n_pages)] + [kv_spec(j) for j in range(n_pages)])
    grid_spec = pltpu.PrefetchScalarGridSpec(
        num_scalar_prefetch=1, grid=(nseq,), in_specs=in_specs,
        out_specs=pl.BlockSpec((t, W_A), lambda b, pt: (b, 0)),
        scratch_shapes=[pltpu.VMEM((2 * H_A * t, n_pages * page), f32)])
    return pl.pallas_call(
        functools.partial(_attn_sample_kernel, t=t, n_pages=n_pages, page=page, gain=gain),
        grid_spec=grid_spec, out_shape=jax.ShapeDtypeStruct((nseq * t, W_A), bf16),
        compiler_params=_params(("arbitrary",)), name="attn_sample",
    )(page_table.reshape(-1), lam, slopes, proj, proj, proj, subln_w, *([ckt] * n_pages), *([cvr] * n_pages))


def _conv_step(f_sc, x3, w_ref, buf0_ref, buf_out, width, chunk, first):
    lo = SUBLANES - (width - 1)

    @pl.when(first)
    def _():
        f_sc[:, 0:SUBLANES, :] = jnp.zeros((f_sc.shape[0], SUBLANES, f_sc.shape[2]), f32)
        f_sc[:, lo:SUBLANES, :] = buf0_ref[...]

    f_sc[:, SUBLANES:SUBLANES + chunk, :] = x3
    out = w_ref[0:1, :][None] * f_sc[:, lo:lo + chunk, :]
    for j in range(1, width):
        out = out + w_ref[j:j + 1, :][None] * f_sc[:, lo + j:lo + j + chunk, :]
    buf_out[...] = f_sc[:, chunk + lo:chunk + SUBLANES, :]
    f_sc[:, 0:SUBLANES, :] = f_sc[:, chunk:chunk + SUBLANES, :]
    return out


def _sigmoid(x):
    return 1.0 / (1.0 + jnp.exp(-x))


def _gelu_tanh(x):
    return 0.5 * x * (1.0 + jnp.tanh(0.7978845608028654 * (x + 0.044715 * (x * x * x))))


def _conv_lru_kernel(pb_ref, pd_ref, cb0_ref, lb0_ref, h0_ref, wb_ref, wd_ref, bd_ref, wa_ref, ba_ref, wx_ref,
                     bx_ref, sp_ref, yb_ref, yd_ref, cb_out, lb_out, h_out, fb_sc, fd_sc, h_sc, *, sps, chunk):
    first = pl.program_id(1) == 0
    rows = sps * chunk
    pb = pb_ref[...]
    bg, cg, xt = pb[:, :W_B], pb[:, W_B:2 * W_B], pb[:, 2 * W_B:]
    conv_b = _conv_step(fb_sc, (cg * xt).reshape(sps, chunk, W_B), wb_ref, cb0_ref, cb_out, CONV_B, chunk, first)
    yb_ref[...] = (bg * conv_b.reshape(rows, W_B)).astype(yb_ref.dtype)

    pd = pd_ref[...]
    xd, gd = pd[:, :W_D], pd[:, W_D:]
    conv_d = _conv_step(fd_sc, xd.reshape(sps, chunk, W_D), wd_ref, lb0_ref, lb_out, CONV_D, chunk, first)
    xf = conv_d.reshape(rows, W_D) + bd_ref[...]
    xfb = xf.astype(bf16)
    ra = jnp.concatenate([jnp.dot(xfb[:, h * BW_D:(h + 1) * BW_D], wa_ref[h].astype(bf16),
                                  preferred_element_type=f32) for h in range(H_D)], axis=-1)
    rx = jnp.concatenate([jnp.dot(xfb[:, h * BW_D:(h + 1) * BW_D], wx_ref[h].astype(bf16),
                                  preferred_element_type=f32) for h in range(H_D)], axis=-1)
    r = _sigmoid(ra + ba_ref[...])
    ig = _sigmoid(rx + bx_ref[...])
    log_a = -RG_C * r * sp_ref[...]
    a = jnp.exp(log_a)
    th = jnp.tanh(log_a)
    bx = jnp.sqrt(-2.0 * th / (1.0 - th)) * (ig * xf)

    pos = lax.broadcasted_iota(jnp.int32, (rows, W_D), 0) % chunk
    step = 1
    while step < chunk:
        a_s = pltpu.roll(a, step, 0)
        b_s = pltpu.roll(bx, step, 0)
        live = pos >= step
        bx = jnp.where(live, a * b_s + bx, bx)
        a = jnp.where(live, a * a_s, a)
        step *= 2

    @pl.when(first)
    def _():
        h_sc[...] = h0_ref[...]

    h_in = jnp.broadcast_to(h_sc[...], (sps, chunk, W_D)).reshape(rows, W_D)
    hs = bx + a * h_in
    h_last = hs.reshape(sps, chunk, W_D)[:, chunk - 1:chunk, :]
    h_sc[...] = h_last
    h_out[...] = h_last
    yd_ref[...] = (hs * _gelu_tanh(gd)).astype(yd_ref.dtype)


def conv_lru(proj, row0, nseq, t, sps, chunk, conv_b_buf, lru_buf, lru_h, wb, wd, bd, wa, ba, wx, bx, sp):
    rows = sps * chunk
    nc = t // chunk
    rb0 = row0 // rows

    def rmap(s, c):
        return rb0 + s * nc + c

    def full(shape):
        return pl.BlockSpec(shape, lambda s, c: (0,) * len(shape))

    def per_seq(shape):
        return pl.BlockSpec((sps,) + shape, lambda s, c: (s,) + (0,) * len(shape))

    in_specs = [pl.BlockSpec((rows, N_B), lambda s, c: (rmap(s, c), OFF_B // N_B)),
                pl.BlockSpec((rows, N_D), lambda s, c: (rmap(s, c), OFF_D // N_D)),
                per_seq((CONV_B - 1, W_B)), per_seq((CONV_D - 1, W_D)), per_seq((1, W_D)),
                full((CONV_B, W_B)), full((CONV_D, W_D)), full((1, W_D)),
                full((H_D, BW_D, BW_D)), full((1, W_D)), full((H_D, BW_D, BW_D)), full((1, W_D)), full((1, W_D))]
    out_specs = [pl.BlockSpec((rows, W_B), lambda s, c: (s * nc + c, 0)),
                 pl.BlockSpec((rows, W_D), lambda s, c: (s * nc + c, 0)),
                 per_seq((CONV_B - 1, W_B)), per_seq((CONV_D - 1, W_D)), per_seq((1, W_D))]
    out_shape = [jax.ShapeDtypeStruct((nseq * t, W_B), bf16), jax.ShapeDtypeStruct((nseq * t, W_D), bf16),
                 jax.ShapeDtypeStruct((nseq, CONV_B - 1, W_B), f32),
                 jax.ShapeDtypeStruct((nseq, CONV_D - 1, W_D), f32),
                 jax.ShapeDtypeStruct((nseq, 1, W_D), f32)]
    return pl.pallas_call(
        functools.partial(_conv_lru_kernel, sps=sps, chunk=chunk),
        grid=(nseq // sps, nc), in_specs=in_specs, out_specs=out_specs, out_shape=out_shape,
        scratch_shapes=[pltpu.VMEM((sps, chunk + SUBLANES, W_B), f32), pltpu.VMEM((sps, chunk + SUBLANES, W_D), f32),
                        pltpu.VMEM((sps, 1, W_D), f32)],
        compiler_params=_params(("arbitrary", "arbitrary")), name="conv_lru",
    )(proj, proj, conv_b_buf, lru_buf, lru_h, wb, wd, bd, wa, ba, wx, bx, sp)


_HI = lax.Precision.HIGHEST


def _dot_nt(a, b, precision=None):
    return lax.dot_general(a, b, (((1,), (1,)), ((), ())), preferred_element_type=f32, precision=precision)


def _dot_tn(a, b):
    return lax.dot_general(a, b, (((0,), (0,)), ((), ())), preferred_element_type=f32)


def _hi_lo(x):
    hi = x.astype(bf16).astype(f32)
    return hi, x - hi


def _dot_f32(a, b, chunk):
    if chunk % 16:
        return jnp.dot(a, b, preferred_element_type=f32, precision=_HI)
    a_hi, a_lo = _hi_lo(a)
    b_hi, b_lo = _hi_lo(b)
    lhs = jnp.concatenate([a_hi, a_hi, a_lo], axis=1).astype(bf16)
    rhs = jnp.concatenate([b_hi, b_lo, b_hi], axis=0).astype(bf16)
    return jnp.dot(lhs, rhs, preferred_element_type=f32)


def _cumsum_rows(tri, x, chunk):
    if chunk % 16:
        return jnp.dot(tri, x, preferred_element_type=f32, precision=_HI)
    hi, r1 = _hi_lo(x)
    mid, lo = _hi_lo(r1)
    lhs = jnp.concatenate([tri, tri, tri], axis=1).astype(bf16)
    rhs = jnp.concatenate([hi, mid, lo], axis=0).astype(bf16)
    return jnp.dot(lhs, rhs, preferred_element_type=f32)


def _row_bcast(gc, pick, chunk):
    sel = pick.astype(f32)
    if chunk % 16:
        return _dot_nt(sel, gc, precision=_HI)
    hi, r1 = _hi_lo(gc)
    mid, lo = _hi_lo(r1)
    lhs = jnp.concatenate([sel, sel, sel], axis=1).astype(bf16)
    rhs = jnp.concatenate([hi, mid, lo], axis=1).astype(bf16)
    return _dot_nt(lhs, rhs)


def _deltanet_kernel(qkv_ref, z_ref, ab_ref, buf0_ref, s0_ref, wc_ref, nega_ref, dtb_ref, nw_ref,
                     y_ref, buf_out, s_out, f_sc, s_sc, *, sps, nsub, chunk):
    first = pl.program_id(1) == 0
    span_rows = nsub * chunk
    rows = sps * span_rows
    wq = 3 * W_C
    conv = _conv_step(f_sc, qkv_ref[...].reshape(sps, span_rows, wq), wc_ref, buf0_ref, buf_out, CONV_C,
                      span_rows, first)
    conv = conv.reshape(rows, wq)
    qkv = conv * _sigmoid(conv)

    @pl.when(first)
    def _():
        s_sc[...] = s0_ref[...]

    ab = ab_ref[...]
    xg = ab + dtb_ref[...]
    g_all = nega_ref[...] * (jnp.maximum(xg, 0.0) + jnp.log1p(jnp.exp(-jnp.abs(xg))))
    beta_all = _sigmoid(ab)
    z = z_ref[...]
    ri = lax.broadcasted_iota(jnp.int32, (chunk, chunk), 0)
    ci = lax.broadcasted_iota(jnp.int32, (chunk, chunk), 1)
    causal = ri >= ci
    strict = ri > ci
    tri = causal.astype(f32)
    eye = (ri == ci).astype(f32)
    lane = lax.broadcasted_iota(jnp.int32, (chunk, LANES), 1)
    nw = nw_ref[...]
    spans = [(i, c) for i in range(sps) for c in range(nsub)]
    units = [(i, c, h) for i, c in spans for h in range(H_C)]

    def rows_of(i, c):
        r0 = (i * nsub + c) * chunk
        return slice(r0, r0 + chunk)

    gcs = {sp: _cumsum_rows(tri, g_all[rows_of(*sp)], chunk) for sp in spans}

    qs, ks, gcols, rhss, qks, negs = [], [], [], [], [], []
    for i, c, h in units:
        rs = rows_of(i, c)
        gc = gcs[(i, c)]
        q = qkv[rs, h * DK_C:(h + 1) * DK_C]
        k = qkv[rs, W_C + h * DK_C:W_C + (h + 1) * DK_C]
        v = qkv[rs, 2 * W_C + h * DV_C:2 * W_C + (h + 1) * DV_C]
        q = q * lax.rsqrt(jnp.sum(q * q, axis=-1, keepdims=True) + EPS) * (DK_C ** -0.5)
        k = k * lax.rsqrt(jnp.sum(k * k, axis=-1, keepdims=True) + EPS)
        beta = beta_all[rs, H_C + h:H_C + h + 1]
        gcol = gc[:, h:h + 1]
        grow = _row_bcast(gc, lane == h, chunk)
        decay = jnp.where(causal, jnp.exp(jnp.where(causal, gcol - grow, 0.0)), 0.0)
        eg = jnp.exp(gcol)
        kb = k * beta
        negs.append(jnp.where(strict, _dot_nt(kb, k) * decay, 0.0) * -1.0)
        qks.append(_dot_nt(q, k) * decay)
        rhss.append(jnp.concatenate([v * beta, kb * eg], axis=1))
        qs.append(q * eg)
        ks.append(k)
        gcols.append(gcol)

    invs = [eye + n for n in negs]
    span = 2
    while span < chunk:
        negs = [_dot_f32(n, n, chunk) for n in negs]
        invs = [inv + _dot_f32(inv, n, chunk) for inv, n in zip(invs, negs)]
        span *= 2
    sols = [_dot_f32(inv, rhs, chunk) for inv, rhs in zip(invs, rhss)]

    states = {(i, h): s_sc[i, h] for i in range(sps) for h in range(H_C)}
    outs = {}
    for c in range(nsub):
        cur = [(idx, (i, h)) for idx, (i, cc, h) in enumerate(units) if cc == c]
        wq_s = {idx: jnp.dot(jnp.concatenate([sols[idx][:, DV_C:], qs[idx]], axis=0), states[key],
                             preferred_element_type=f32) for idx, key in cur}
        v_new = {idx: sols[idx][:, :DV_C] - wq_s[idx][:chunk] for idx, _ in cur}
        for idx, _ in cur:
            outs[idx] = wq_s[idx][chunk:] + jnp.dot(qks[idx], v_new[idx], preferred_element_type=f32)
        for idx, key in cur:
            gcol = gcols[idx]
            g_last = gcol[chunk - 1:chunk, :]
            states[key] = (states[key] * jnp.exp(g_last)
                           + _dot_tn(ks[idx] * jnp.exp(g_last - gcol), v_new[idx]))
    for (i, h), st in states.items():
        s_sc[i, h] = st
        s_out[i, h] = st
    for idx, (i, c, h) in enumerate(units):
        rs = rows_of(i, c)
        zh = z[rs, h * DV_C:(h + 1) * DV_C]
        o = outs[idx]
        o = o * lax.rsqrt(jnp.mean(o * o, axis=-1, keepdims=True) + EPS) * nw
        y_ref[rs, h * DV_C:(h + 1) * DV_C] = (o * (zh * _sigmoid(zh))).astype(y_ref.dtype)


def deltanet(proj, proj_ab, row0, nseq, t, sps, nsub, chunk, dn_buf, dn_s, wc, nega, dtb, nw):
    rows = sps * nsub * chunk
    nc = t // (nsub * chunk)
    rb0 = row0 // rows
    wq = 3 * W_C

    def rmap(s, c):
        return rb0 + s * nc + c

    def full(shape):
        return pl.BlockSpec(shape, lambda s, c: (0,) * len(shape))

    def per_seq(shape):
        return pl.BlockSpec((sps,) + shape, lambda s, c: (s,) + (0,) * len(shape))

    in_specs = [pl.BlockSpec((rows, wq), lambda s, c: (rmap(s, c), OFF_C // wq)),
                pl.BlockSpec((rows, W_C), lambda s, c: (rmap(s, c), (OFF_C + wq) // W_C)),
                pl.BlockSpec((rows, LANES), lambda s, c: (rmap(s, c), 0)),
                per_seq((CONV_C - 1, wq)), per_seq((H_C, DK_C, DV_C)),
                full((CONV_C, wq)), full((1, LANES)), full((1, LANES)), full((1, DV_C))]
    out_specs = [pl.BlockSpec((rows, W_C), lambda s, c: (s * nc + c, 0)),
                 per_seq((CONV_C - 1, wq)), per_seq((H_C, DK_C, DV_C))]
    out_shape = [jax.ShapeDtypeStruct((nseq * t, W_C), bf16),
                 jax.ShapeDtypeStruct((nseq, CONV_C - 1, wq), f32),
                 jax.ShapeDtypeStruct((nseq, H_C, DK_C, DV_C), f32)]
    return pl.pallas_call(
        functools.partial(_deltanet_kernel, sps=sps, nsub=nsub, chunk=chunk),
        grid=(nseq // sps, nc), in_specs=in_specs, out_specs=out_specs, out_shape=out_shape,
        scratch_shapes=[pltpu.VMEM((sps, nsub * chunk + SUBLANES, wq), f32),
                        pltpu.VMEM((sps, H_C, DK_C, DV_C), f32)],
        compiler_params=_params(("arbitrary", "arbitrary")), name="deltanet",
    )(proj, proj, proj_ab, dn_buf, dn_s, wc, nega, dtb, nw)


def _lane_row(v):
    return jnp.concatenate([v.astype(f32), jnp.zeros((LANES - v.shape[0],), f32)])[None, :]


def kernel(x_prompt, x_sample, cache_k, cache_v, state_conv_b, state_dn_conv, state_dn, state_lru_conv, state_lru_h, page_table, c_prompt, c_sample, w_mod, b_mod, norm_mix, norm_ffn, w_in, w_out, lam_q1, lam_k1, lam_q2, lam_k2, subln_w, conv_b_w, dn_conv_w, dn_a_log, dn_dt_bias, dn_norm_w, lru_conv_w, lru_conv_b, lru_wa, lru_ba, lru_wx, lru_bx, lru_lambda, ffn_w_gate, ffn_w_up, ffn_w_down, moe_router, moe_w_gate, moe_w_up, moe_w_down, final_norm_w):
    x = (x_prompt.reshape(N_PROMPT, D_MODEL), x_sample.reshape(N_SAMPLE, D_MODEL))
    c_all = jnp.concatenate([c_prompt, c_sample], axis=0)
    n_pool, page = cache_k.shape[1], cache_k.shape[2]
    ckt = jnp.transpose(cache_k, (0, 1, 3, 4, 5, 2)).reshape(DEPTH, n_pool, W_A, page)
    cvr = cache_v.reshape(DEPTH, n_pool, page * H_A, DV_A)
    slopes = jnp.exp2(-8.0 * jnp.arange(1, H_A + 1, dtype=f32) / H_A)

    zero_cb = jnp.zeros((BATCH, CONV_B - 1, W_B), f32)
    zero_dc = jnp.zeros((BATCH, CONV_C - 1, 3 * W_C), f32)
    zero_ds = jnp.zeros((BATCH, H_C, DK_C, DV_C), f32)
    zero_lc = jnp.zeros((BATCH, CONV_D - 1, W_D), f32)
    zero_lh = jnp.zeros((BATCH, 1, W_D), f32)

    ks_p, vs_p, ks_s, vs_s = [], [], [], []
    st_p = [[] for _ in range(5)]
    st_s = [[] for _ in range(5)]
    for l in range(DEPTH):
        mod = modulation(c_all, w_mod, b_mod[:, None, :], l)
        mod_p = jnp.transpose(mod[:BATCH].reshape(BATCH, N_MOD, D_MODEL), (1, 0, 2))[:, :, None, :]
        mod_s = jnp.transpose(mod[BATCH:].reshape(DEC_BATCH, N_MOD, D_MODEL), (1, 0, 2))

        hn = norm_mod(x, norm_mix[l][None, :], mod_p, mod_s, 1, 0)
        wl = w_in[l]
        n_cqkvz = 4 * W_C
        w_pack = jnp.concatenate([wl[:, :N_A + N_B + n_cqkvz], wl[:, N_A + N_B + N_C:]], axis=1).astype(bf16)
        w_ab = jnp.concatenate([wl[:, N_A + N_B + n_cqkvz:N_A + N_B + N_C],
                                jnp.zeros((D_MODEL, LANES - 2 * H_C), f32)], axis=1).astype(bf16)
        proj, proj_ab = in_proj(hn, w_pack, w_ab, tm=1088, tn=1024)

        lam_init = 0.8 - 0.6 * math.exp(-0.3 * l)
        lam = (jnp.exp(jnp.sum(lam_q1[l] * lam_k1[l])) - jnp.exp(jnp.sum(lam_q2[l] * lam_k2[l])) + lam_init)
        lam = lam.astype(f32).reshape(1)
        gain = 1.0 - lam_init
        sw = subln_w[l][None, :]
        ya = (attn_prompt(proj, lam, slopes, sw, gain, BATCH, SEQ, ATT_TQ, ATT_HPB),
              attn_sample(proj, ckt, cvr, l, page_table, lam, slopes, sw, gain, N_PROMPT, DEC_BATCH, DEC_SEQ))

        sp = jax.nn.softplus(-lru_lambda[l])[None, :]
        lru_args = (conv_b_w[l], lru_conv_w[l], lru_conv_b[l][None, :], lru_wa[l], lru_ba[l][None, :],
                    lru_wx[l], lru_bx[l][None, :], sp)
        yb_p, yd_p, cb_p, lb_p, lh_p = conv_lru(proj, 0, BATCH, SEQ, 1, LRU_CHUNK, zero_cb, zero_lc, zero_lh,
                                                *lru_args)
        yb_s, yd_s, cb_s, lb_s, lh_s = conv_lru(proj, N_PROMPT, DEC_BATCH, DEC_SEQ, LRU_SPS, DEC_SEQ,
                                                state_conv_b[l], state_lru_conv[l], state_lru_h[l][:, None, :],
                                                *lru_args)

        dn_args = (dn_conv_w[l], _lane_row(-jnp.exp(dn_a_log[l])), _lane_row(dn_dt_bias[l]), dn_norm_w[l][None, :])
        yc_p, dc_p, ds_p = deltanet(proj, proj_ab, 0, BATCH, SEQ, 1, DN_SUB_PROMPT, DN_CHUNK, zero_dc, zero_ds,
                                    *dn_args)
        yc_s, dc_s, ds_s = deltanet(proj, proj_ab, N_PROMPT, DEC_BATCH, DEC_SEQ, DN_SPS, 1,
                                    math.gcd(DEC_SEQ, DN_CHUNK), state_dn_conv[l], state_dn[l], *dn_args)

        k_new = proj[:, W_A:2 * W_A]
        v_new = proj[:, 2 * W_A:3 * W_A]
        ks_p.append(k_new[:N_PROMPT].reshape(BATCH, SEQ, H_A, 2, DH_A))
        vs_p.append(v_new[:N_PROMPT].reshape(BATCH, SEQ, H_A, DV_A))
        ks_s.append(k_new[N_PROMPT:].reshape(DEC_BATCH, DEC_SEQ, H_A, 2, DH_A))
        vs_s.append(v_new[N_PROMPT:].reshape(DEC_BATCH, DEC_SEQ, H_A, DV_A))
        for lst, vals in ((st_p, (cb_p, dc_p, ds_p, lb_p, lh_p[:, 0, :])), (st_s, (cb_s, dc_s, ds_s, lb_s, lh_s[:, 0, :]))):
            for i, val in enumerate(vals):
                lst[i].append(val)

        x = mix_out_residual((ya, (yb_p, yb_s), (yc_p, yc_s), (yd_p, yd_s)), w_out[l].astype(bf16), x,
                             mod_p, mod_s, 2, tm=1024, tn=1024)

        j = l // 2
        if l % 2 == 0:
            hn = norm_mod(x, norm_ffn[l][None, :], mod_p, mod_s, 4, 3)
            h = swiglu_up(hn, ffn_w_gate[j].astype(bf16), ffn_w_up[j].astype(bf16), tm=1088, tn=512)
            x = matmul_gated_residual(h, ffn_w_down[j].astype(bf16), x, mod_p, mod_s, 5, tm=1024, tn=256, tk=D_FF)
        else:
            router = jnp.concatenate([moe_router[j], jnp.zeros((D_MODEL, LANES - N_EXPERTS), f32)], axis=1)
            hn, logits = norm_mod(x, norm_ffn[l][None, :], mod_p, mod_s, 4, 3, router=router)
            plan = moe_route(logits[:, :N_EXPERTS], D_FF_EXPERT // MOE_TN)
            xs = jnp.take(hn, plan['row_src'], axis=0, mode='clip')
            h, wd_bf = moe_up(plan['item_m'], plan['item_n'], plan['item_wn'], plan['item_e'], plan['item_v'], xs,
                              moe_w_gate[j], moe_w_up[j], moe_w_down[j])
            ys = moe_down(plan['tile_e'], plan['n_used'], h, wd_bf)
            gates = plan['gates']
            dest = plan['dest']
            f = (gates[:, 0:1] * jnp.take(ys, dest[:, 0], axis=0, mode='clip')
                 + gates[:, 1:2] * jnp.take(ys, dest[:, 1], axis=0, mode='clip'))
            g2_tok = jnp.concatenate([jnp.repeat(mod_p[5, :, 0, :], SEQ, axis=0),
                                      jnp.repeat(mod_s[5], DEC_SEQ, axis=0)], axis=0)
            x = x + g2_tok * f

    y_prompt, y_sample = final_norm(x, final_norm_w[None, :])
    y_prompt = y_prompt.reshape(BATCH, SEQ, D_MODEL)
    y_sample = y_sample.reshape(DEC_BATCH, DEC_SEQ, D_MODEL)
    sp_ = [jnp.stack(o) for o in st_p]
    ss_ = [jnp.stack(o) for o in st_s]
    return (y_prompt, y_sample, jnp.stack(ks_p), jnp.stack(vs_p), jnp.stack(ks_s), jnp.stack(vs_s),
            sp_[0], ss_[0], sp_[1], ss_[1], sp_[2], ss_[2], sp_[3], ss_[3], sp_[4], ss_[4])
```

```python
import functools
import math

import jax
import jax.numpy as jnp
from jax import lax
from jax.experimental import pallas as pl
from jax.experimental.pallas import tpu as pltpu

D_MODEL = 2048
BATCH = 8
SEQ = 2048
DEPTH = 2
DEC_BATCH = 128
DEC_SEQ = 8
W_GROUP = D_MODEL // 4
W_A = W_B = W_C = W_D = W_GROUP
H_A = 4
DH_A = W_A // (2 * H_A)
DV_A = 2 * DH_A
Q_BLOCK = 128
CONV_B = 3
H_C = 4
DK_C = W_C // H_C
DV_C = W_C // H_C
CONV_C = 4
DN_CHUNK = 64
H_D = 4
BW_D = W_D // H_D
CONV_D = 4
RG_C = 8.0
D_FF = 5632
N_EXPERTS = 8
TOP_K = 2
D_FF_EXPERT = 7168
N_MOD = 6
N_MIX = 4
EPS = 1e-6
N_A = 3 * W_A
N_B = 3 * W_B
N_C = 4 * W_C + 2 * H_C
N_D = 2 * W_D

N_PROMPT = BATCH * SEQ
N_SAMPLE = DEC_BATCH * DEC_SEQ
N_TOK = N_PROMPT + N_SAMPLE
SUBLANES = 8
LANES = 128

OFF_A = 0
OFF_B = N_A
OFF_C = N_A + N_B
OFF_D = OFF_C + 4 * W_C
N_PROJ = OFF_D + N_D

VMEM_LIMIT = 48 * 1024 * 1024
MOE_TM = 1024
MOE_TN = 512
MOE_DN_TN = 256
MOE_CAST_ROWS = 224
MOE_TILES = (TOP_K * N_TOK) // MOE_TM + N_EXPERTS
MOE_ROWS = MOE_TILES * MOE_TM
ROW_TILE = 512
ATT_TQ = 512
ATT_HPB = 4
LRU_CHUNK = 256
LRU_SPS = 16
DN_SPS = 8
DN_SUB_PROMPT = 4

f32 = jnp.float32
bf16 = jnp.bfloat16


def _params(sem):
    return pltpu.CompilerParams(dimension_semantics=sem, vmem_limit_bytes=VMEM_LIMIT)


def _silu(x):
    return x * (1.0 / (1.0 + jnp.exp(-x)))


def _mod_kernel(c_ref, w_ref, b_ref, o_ref):
    c = _silu(c_ref[...]).astype(bf16)
    o_ref[...] = jnp.dot(c, w_ref[0].astype(bf16), preferred_element_type=f32) + b_ref[0]


def modulation(c_all, w, b, layer):
    r = c_all.shape[0]
    n = w.shape[2]
    tn = 1024
    return pl.pallas_call(
        _mod_kernel,
        grid=(n // tn,),
        in_specs=[pl.BlockSpec((r, D_MODEL), lambda j: (0, 0)),
                  pl.BlockSpec((1, D_MODEL, tn), lambda j: (layer, 0, j)),
                  pl.BlockSpec((1, 1, tn), lambda j: (layer, 0, j))],
        out_specs=pl.BlockSpec((r, tn), lambda j: (0, j)),
        out_shape=jax.ShapeDtypeStruct((r, n), f32),
        compiler_params=_params(("arbitrary",)),
        name="modulation",
    )(c_all, w, b)


def _mod_specs(idx, tile_rows, col_block):
    tiles_per_req = SEQ // tile_rows
    npt = N_PROMPT // tile_rows

    def p_map(*ids):
        return (idx, jnp.minimum(ids[0] // tiles_per_req, BATCH - 1), 0, col_block(*ids))

    def s_map(*ids):
        return (idx, jnp.maximum(ids[0] - npt, 0), col_block(*ids))

    return p_map, s_map


def _row_pair(x, tile_rows, cols, col_block):
    npt = N_PROMPT // tile_rows
    if isinstance(x, tuple):
        xp, xs = x
        s_map = lambda *ids: (jnp.maximum(ids[0] - npt, 0), col_block(*ids))
    else:
        xp = xs = x
        s_map = lambda *ids: (jnp.maximum(ids[0], npt), col_block(*ids))
    p_map = lambda *ids: (jnp.minimum(ids[0], npt - 1), col_block(*ids))
    return (xp, xs), [pl.BlockSpec((tile_rows, cols), p_map), pl.BlockSpec((tile_rows, cols), s_map)]


def _apply_rows(vals, i, tile_rows, fn_prompt, fn_sample):
    npt = N_PROMPT // tile_rows

    @pl.when(i < npt)
    def _():
        fn_prompt(*vals)

    @pl.when(i >= npt)
    def _():
        fn_sample(*vals)


def _bcast_groups(a, m):
    rows, c = a.shape
    return a.reshape(rows // SUBLANES, SUBLANES, c), m[:, None, :]


def _norm_mod_emit(xp_ref, xs_ref, nw_ref, scp_ref, shp_ref, scs_ref, shs_ref, emit):
    tr = xp_ref.shape[0]

    def normed(x):
        return x * lax.rsqrt(jnp.mean(x * x, axis=-1, keepdims=True) + EPS) * nw_ref[...]

    def prompt():
        emit(normed(xp_ref[...]) * (1.0 + scp_ref[0, 0]) + shp_ref[0, 0])

    def sample():
        y3, sc = _bcast_groups(normed(xs_ref[...]), scs_ref[0])
        emit((y3 * (1.0 + sc) + shs_ref[0][:, None, :]).reshape(tr, D_MODEL))

    _apply_rows((), pl.program_id(0), tr, prompt, sample)


def _norm_mod_kernel(xp_ref, xs_ref, nw_ref, scp_ref, shp_ref, scs_ref, shs_ref, o_ref):
    def emit(hn):
        o_ref[...] = hn.astype(o_ref.dtype)

    _norm_mod_emit(xp_ref, xs_ref, nw_ref, scp_ref, shp_ref, scs_ref, shs_ref, emit)


def _norm_mod_router_kernel(xp_ref, xs_ref, nw_ref, scp_ref, shp_ref, scs_ref, shs_ref, r_ref, o_ref, lg_ref):
    def emit(hn):
        bits = lax.bitcast_convert_type(hn.astype(bf16).astype(f32), jnp.uint32)
        half = D_MODEL // 2
        o_ref[...] = (bits[:, :half] >> 16) | bits[:, half:]
        lg_ref[...] = jnp.dot(hn, r_ref[...], preferred_element_type=f32, precision=lax.Precision.HIGHEST)

    _norm_mod_emit(xp_ref, xs_ref, nw_ref, scp_ref, shp_ref, scs_ref, shs_ref, emit)


def norm_mod(x, nw, mod_p, mod_s, i_scale, i_shift, router=None):
    tr = ROW_TILE
    tg = tr // SUBLANES
    zero = lambda i: 0
    scp_map, scs_map = _mod_specs(i_scale, tr, zero)
    shp_map, shs_map = _mod_specs(i_shift, tr, zero)
    x_ops, x_specs = _row_pair(x, tr, D_MODEL, zero)
    in_specs = x_specs + [
                pl.BlockSpec((1, D_MODEL), lambda i: (0, 0)),
                pl.BlockSpec((1, 1, 1, D_MODEL), scp_map), pl.BlockSpec((1, 1, 1, D_MODEL), shp_map),
                pl.BlockSpec((1, tg, D_MODEL), scs_map), pl.BlockSpec((1, tg, D_MODEL), shs_map)]
    out_spec = pl.BlockSpec((tr, D_MODEL), lambda i: (i, 0))
    out_shape = jax.ShapeDtypeStruct((N_TOK, D_MODEL), bf16)
    if router is not None:
        out_spec = pl.BlockSpec((tr, D_MODEL // 2), lambda i: (i, 0))
        out_shape = jax.ShapeDtypeStruct((N_TOK, D_MODEL // 2), jnp.uint32)
    if router is None:
        return pl.pallas_call(
            _norm_mod_kernel, grid=(N_TOK // tr,), in_specs=in_specs, out_specs=out_spec,
            out_shape=out_shape, compiler_params=_params(("arbitrary",)), name="norm_mod",
        )(*x_ops, nw, mod_p, mod_p, mod_s, mod_s)
    return pl.pallas_call(
        _norm_mod_router_kernel, grid=(N_TOK // tr,),
        in_specs=in_specs + [pl.BlockSpec((D_MODEL, LANES), lambda i: (0, 0))],
        out_specs=[out_spec, pl.BlockSpec((tr, LANES), lambda i: (i, 0))],
        out_shape=[out_shape, jax.ShapeDtypeStruct((N_TOK, LANES), f32)],
        compiler_params=_params(("arbitrary",)), name="norm_mod_router",
    )(*x_ops, nw, mod_p, mod_p, mod_s, mod_s, router)


def _final_norm_kernel(x_ref, nw_ref, op_ref, os_ref, *, n_prompt_tiles):
    x = x_ref[...]
    y = x * lax.rsqrt(jnp.mean(x * x, axis=-1, keepdims=True) + EPS) * nw_ref[...]
    i = pl.program_id(0)

    @pl.when(i < n_prompt_tiles)
    def _():
        op_ref[...] = y

    @pl.when(i >= n_prompt_tiles)
    def _():
        os_ref[...] = y


def final_norm(x, nw):
    tr = ROW_TILE
    npt = N_PROMPT // tr
    return pl.pallas_call(
        functools.partial(_final_norm_kernel, n_prompt_tiles=npt), grid=(N_TOK // tr,),
        in_specs=[pl.BlockSpec((tr, D_MODEL), lambda i: (i, 0)),
                  pl.BlockSpec((1, D_MODEL), lambda i: (0, 0))],
        out_specs=[pl.BlockSpec((tr, D_MODEL), lambda i: (jnp.minimum(i, npt - 1), 0)),
                   pl.BlockSpec((tr, D_MODEL), lambda i: (jnp.maximum(i - npt, 0), 0))],
        out_shape=[jax.ShapeDtypeStruct((N_PROMPT, D_MODEL), f32), jax.ShapeDtypeStruct((N_SAMPLE, D_MODEL), f32)],
        compiler_params=_params(("arbitrary",)), name="final_norm",
    )(x, nw)


def _in_proj_kernel(x_ref, w_ref, wab_ref, o_ref, oab_ref):
    x = x_ref[...]
    o_ref[...] = jnp.dot(x, w_ref[...], preferred_element_type=f32)

    @pl.when(pl.program_id(1) == 0)
    def _():
        oab_ref[...] = jnp.dot(x, wab_ref[...], preferred_element_type=f32)


def in_proj(x, w, w_ab, tm, tn):
    m, k = x.shape
    n = w.shape[1]
    return pl.pallas_call(
        _in_proj_kernel, grid=(m // tm, n // tn),
        in_specs=[pl.BlockSpec((tm, k), lambda i, j: (i, 0)),
                  pl.BlockSpec((k, tn), lambda i, j: (0, j)),
                  pl.BlockSpec((k, LANES), lambda i, j: (0, 0))],
        out_specs=[pl.BlockSpec((tm, tn), lambda i, j: (i, j)), pl.BlockSpec((tm, LANES), lambda i, j: (i, 0))],
        out_shape=[jax.ShapeDtypeStruct((m, n), f32), jax.ShapeDtypeStruct((m, LANES), f32)],
        compiler_params=_params(("arbitrary", "arbitrary")), name="in_proj",
    )(x, w, w_ab)


def _gated_residual_store(acc, xp_ref, xs_ref, gp_ref, gs_ref, o_ref):
    tm, tn = acc.shape

    def prompt(acc):
        o_ref[...] = xp_ref[...] + acc * gp_ref[0, 0]

    def sample(acc):
        a3, g = _bcast_groups(acc, gs_ref[0])
        o_ref[...] = xs_ref[...] + (a3 * g).reshape(tm, tn)

    _apply_rows((acc,), pl.program_id(0), tm, prompt, sample)


def _mm_resid_kernel(y_ref, w_ref, xp_ref, xs_ref, gp_ref, gs_ref, o_ref, acc_ref, *, nk):
    k = pl.program_id(2)
    part = jnp.dot(y_ref[...], w_ref[...], preferred_element_type=f32)

    @pl.when(k == 0)
    def _():
        acc_ref[...] = part

    @pl.when(k > 0)
    def _():
        acc_ref[...] += part

    @pl.when(k == nk - 1)
    def _():
        _gated_residual_store(acc_ref[...], xp_ref, xs_ref, gp_ref, gs_ref, o_ref)


def matmul_gated_residual(y, w, x, mod_p, mod_s, i_gate, tm, tn, tk):
    m, kdim = y.shape
    n = w.shape[1]
    nk = kdim // tk
    gp_map, gs_map = _mod_specs(i_gate, tm, lambda i, j, k: j)
    x_ops, x_specs = _row_pair(x, tm, tn, lambda i, j, k: j)
    return pl.pallas_call(
        functools.partial(_mm_resid_kernel, nk=nk), grid=(m // tm, n // tn, nk),
        in_specs=[pl.BlockSpec((tm, tk), lambda i, j, k: (i, k)),
                  pl.BlockSpec((tk, tn), lambda i, j, k: (k, j)),
                  *x_specs,
                  pl.BlockSpec((1, 1, 1, tn), gp_map),
                  pl.BlockSpec((1, tm // SUBLANES, tn), gs_map)],
        out_specs=pl.BlockSpec((tm, tn), lambda i, j, k: (i, j)),
        out_shape=jax.ShapeDtypeStruct((m, n), f32),
        scratch_shapes=[pltpu.VMEM((tm, tn), f32)],
        compiler_params=_params(("arbitrary", "arbitrary", "arbitrary")), name="matmul_gated_residual",
    )(y, w, *x_ops, mod_p, mod_s)


def _mix_out_kernel(*refs):
    y_refs, (w_ref, xp_ref, xs_ref, gp_ref, gs_ref, o_ref) = refs[:2 * N_MIX], refs[2 * N_MIX:]
    tm = o_ref.shape[0]

    def project(group_refs):
        acc = jnp.dot(group_refs[0][...], w_ref[0:W_GROUP, :], preferred_element_type=f32)
        for gi in range(1, N_MIX):
            acc = acc + jnp.dot(group_refs[gi][...], w_ref[gi * W_GROUP:(gi + 1) * W_GROUP, :],
                                preferred_element_type=f32)
        return acc

    def prompt():
        o_ref[...] = xp_ref[...] + project(y_refs[0::2]) * gp_ref[0, 0]

    def sample():
        a3, g = _bcast_groups(project(y_refs[1::2]), gs_ref[0])
        o_ref[...] = xs_ref[...] + (a3 * g).reshape(o_ref.shape)

    _apply_rows((), pl.program_id(0), tm, prompt, sample)


def mix_out_residual(ys, w, x, mod_p, mod_s, i_gate, tm, tn):
    m, n = N_TOK, D_MODEL
    gp_map, gs_map = _mod_specs(i_gate, tm, lambda i, j: j)
    x_ops, x_specs = _row_pair(x, tm, tn, lambda i, j: j)
    y_ops, y_specs = [], []
    for pair in ys:
        ops, specs = _row_pair(pair, tm, W_GROUP, lambda i, j: 0)
        y_ops += ops
        y_specs += specs
    return pl.pallas_call(
        _mix_out_kernel, grid=(m // tm, n // tn),
        in_specs=[*y_specs,
                  pl.BlockSpec((D_MODEL, tn), lambda i, j: (0, j)),
                  *x_specs,
                  pl.BlockSpec((1, 1, 1, tn), gp_map),
                  pl.BlockSpec((1, tm // SUBLANES, tn), gs_map)],
        out_specs=pl.BlockSpec((tm, tn), lambda i, j: (i, j)),
        out_shape=jax.ShapeDtypeStruct((m, n), f32),
        compiler_params=_params(("arbitrary", "arbitrary")), name="mix_out_residual",
    )(*y_ops, w, *x_ops, mod_p, mod_s)


def _swiglu_body(x, wg, wu):
    g = jnp.dot(x, wg, preferred_element_type=f32)
    u = jnp.dot(x, wu, preferred_element_type=f32)
    return _silu(g) * u


def _swiglu_kernel(x_ref, wg_ref, wu_ref, o_ref):
    o_ref[...] = _swiglu_body(x_ref[...], wg_ref[...], wu_ref[...]).astype(o_ref.dtype)


def swiglu_up(x, wg, wu, tm, tn):
    m, k = x.shape
    n = wg.shape[1]
    return pl.pallas_call(
        _swiglu_kernel, grid=(m // tm, n // tn),
        in_specs=[pl.BlockSpec((tm, k), lambda i, j: (i, 0)),
                  pl.BlockSpec((k, tn), lambda i, j: (0, j)),
                  pl.BlockSpec((k, tn), lambda i, j: (0, j))],
        out_specs=pl.BlockSpec((tm, tn), lambda i, j: (i, j)),
        out_shape=jax.ShapeDtypeStruct((m, n), bf16),
        compiler_params=_params(("arbitrary", "arbitrary")), name="swiglu_up",
    )(x, wg, wu)


def _unpack_tokens(words):
    lo = lax.bitcast_convert_type(words << 16, f32)
    hi = lax.bitcast_convert_type(words & jnp.uint32(0xFFFF0000), f32)
    return jnp.concatenate([lo, hi], axis=1).astype(bf16)


def _moe_up_kernel(m_ref, n_ref, wn_ref, e_ref, v_ref, x_ref, wg_ref, wu_ref, wd_ref, o_ref, wdb_ref, wg_sc, wu_sc,
                   *, n_cast):
    i = pl.program_id(0)

    @pl.when(i < n_cast)
    def _():
        wdb_ref[...] = wd_ref[...].astype(bf16)

    @pl.when(v_ref[i] == 2)
    def _():
        wg_sc[...] = wg_ref[0].astype(bf16)
        wu_sc[...] = wu_ref[0].astype(bf16)

    @pl.when(v_ref[i] > 0)
    def _():
        o_ref[...] = _swiglu_body(_unpack_tokens(x_ref[...]), wg_sc[...], wu_sc[...]).astype(o_ref.dtype)

    @pl.when(v_ref[i] == 0)
    def _():
        o_ref[...] = jnp.zeros(o_ref.shape, o_ref.dtype)


def moe_up(item_m, item_n, item_wn, item_e, item_v, xs, wg, wu, wd):
    ff = wg.shape[2]
    nt = ff // MOE_TN
    n_items = MOE_TILES * nt
    wd_rows = wd.shape[0] * wd.shape[1]
    n_cast = wd_rows // MOE_CAST_ROWS
    assert n_cast <= (TOP_K * N_TOK // MOE_TM) * nt and wd_rows % MOE_CAST_ROWS == 0
    cast_spec = pl.BlockSpec((MOE_CAST_ROWS, D_MODEL), lambda i, m, n, wn, e, v: (jnp.minimum(i, n_cast - 1), 0))
    grid_spec = pltpu.PrefetchScalarGridSpec(
        num_scalar_prefetch=5, grid=(n_items,),
        in_specs=[pl.BlockSpec((MOE_TM, D_MODEL // 2), lambda i, m, n, wn, e, v: (m[i], 0)),
                  pl.BlockSpec((1, D_MODEL, MOE_TN), lambda i, m, n, wn, e, v: (e[i], 0, wn[i])),
                  pl.BlockSpec((1, D_MODEL, MOE_TN), lambda i, m, n, wn, e, v: (e[i], 0, wn[i])),
                  cast_spec],
        out_specs=[pl.BlockSpec((MOE_TM, MOE_TN), lambda i, m, n, wn, e, v: (m[i], n[i])), cast_spec],
        scratch_shapes=[pltpu.VMEM((D_MODEL, MOE_TN), bf16), pltpu.VMEM((D_MODEL, MOE_TN), bf16)])
    h, wd_bf = pl.pallas_call(
        functools.partial(_moe_up_kernel, n_cast=n_cast), grid_spec=grid_spec,
        out_shape=[jax.ShapeDtypeStruct((MOE_ROWS, ff), bf16), jax.ShapeDtypeStruct((wd_rows, D_MODEL), bf16)],
        compiler_params=_params(("arbitrary",)), name="moe_up",
    )(item_m, item_n, item_wn, item_e, item_v, xs, wg, wu, wd.reshape(wd_rows, D_MODEL))
    return h, wd_bf.reshape(wd.shape)


def _moe_down_kernel(te_ref, nu_ref, h_ref, wd_ref, o_ref):
    m = pl.program_id(0)

    @pl.when(m < nu_ref[0])
    def _():
        o_ref[...] = jnp.dot(h_ref[...], wd_ref[0], preferred_element_type=f32)

    @pl.when(m >= nu_ref[0])
    def _():
        o_ref[...] = jnp.zeros(o_ref.shape, o_ref.dtype)


def moe_down(tile_e, n_used, h, wd):
    ff = wd.shape[1]
    nn = D_MODEL // MOE_DN_TN

    def mclamp(m, nu):
        return jnp.minimum(m, nu[0] - 1)

    def nclamp(m, n, nu):
        return jnp.where(m < nu[0], n, nn - 1)

    grid_spec = pltpu.PrefetchScalarGridSpec(
        num_scalar_prefetch=2, grid=(MOE_TILES, nn),
        in_specs=[pl.BlockSpec((MOE_TM, ff), lambda m, n, te, nu: (mclamp(m, nu), 0)),
                  pl.BlockSpec((1, ff, MOE_DN_TN), lambda m, n, te, nu: (te[mclamp(m, nu)], 0, nclamp(m, n, nu)))],
        out_specs=pl.BlockSpec((MOE_TM, MOE_DN_TN), lambda m, n, te, nu: (m, n)))
    return pl.pallas_call(
        _moe_down_kernel, grid_spec=grid_spec,
        out_shape=jax.ShapeDtypeStruct((MOE_ROWS, D_MODEL), f32),
        compiler_params=_params(("arbitrary", "arbitrary")), name="moe_down",
    )(tile_e, n_used, h, wd)


def moe_route(logits, n_ff_tiles):
    top_v, top_i = lax.top_k(logits, TOP_K)
    gates = jax.nn.softmax(top_v, axis=-1)
    e_flat = top_i.reshape(-1).astype(jnp.int32)
    onehot = (e_flat[:, None] == jnp.arange(N_EXPERTS, dtype=jnp.int32)[None, :]).astype(jnp.int32)
    csum = jnp.cumsum(onehot, axis=0)
    counts = csum[-1]
    rank = jnp.sum((csum - onehot) * onehot, axis=-1)
    tiles_e = (counts + MOE_TM - 1) // MOE_TM
    tile_end = jnp.cumsum(tiles_e)
    tile_start = tile_end - tiles_e
    dest = tile_start[e_flat] * MOE_TM + rank
    tok = jnp.arange(TOP_K * N_TOK, dtype=jnp.int32) // TOP_K
    row_src = (jnp.arange(MOE_ROWS, dtype=jnp.int32) % N_TOK).at[dest].set(tok)
    n_used = tile_end[-1]
    t_ids = jnp.arange(MOE_TILES, dtype=jnp.int32)
    tile_e = jnp.minimum(jnp.searchsorted(tile_end, t_ids, side="right"), N_EXPERTS - 1).astype(jnp.int32)
    nt = n_ff_tiles
    items_e = tiles_e * nt
    item_end = jnp.cumsum(items_e)
    item_start = item_end - items_e
    total = item_end[-1]
    idx = jnp.arange(MOE_TILES * nt, dtype=jnp.int32)
    valid = idx < total
    idc = jnp.minimum(idx, total - 1)
    ie = jnp.minimum(jnp.searchsorted(item_end, idc, side="right"), N_EXPERTS - 1).astype(jnp.int32)
    local = idc - item_start[ie]
    te = jnp.maximum(tiles_e[ie], 1)
    item_wn = (local // te).astype(jnp.int32)
    spare = idx - total
    item_n = jnp.where(valid, item_wn, spare % nt).astype(jnp.int32)
    item_m = jnp.where(valid, tile_start[ie] + local % te, n_used + spare // nt).astype(jnp.int32)
    return dict(gates=gates, dest=dest.reshape(N_TOK, TOP_K), row_src=row_src, tile_e=tile_e,
                n_used=n_used.reshape(1).astype(jnp.int32), item_m=item_m, item_n=item_n, item_wn=item_wn, item_e=ie,
                item_v=jnp.where(valid, jnp.where(local % te == 0, 2, 1), 0).astype(jnp.int32))


def _smem_spec():
    return pl.BlockSpec(memory_space=pltpu.SMEM)


def _subln(o, w, gain):
    return o * lax.rsqrt(jnp.mean(o * o, axis=-1, keepdims=True) + EPS) * w * gain


def _attn_prompt_kernel(lam_ref, slope_ref, q_ref, k_ref, v_ref, w_ref, o_ref, m_sc, acc_sc, *, tq, hpb, gain):
    hb = pl.program_id(1)
    qi = pl.program_id(2)
    lane = lax.broadcasted_iota(jnp.int32, (tq, DV_A), 1)
    row = lax.broadcasted_iota(jnp.int32, (2 * tq, tq), 0)
    col = lax.broadcasted_iota(jnp.int32, (2 * tq, tq), 1)
    rel = (col - jnp.where(row >= tq, row - tq, row)).astype(f32)
    ones_v = jnp.ones((tq, DV_A), bf16)
    qss, slopes, biases = [], [], []
    for hh in range(hpb):
        q = q_ref[:, hh * DV_A:(hh + 1) * DV_A] * (DH_A ** -0.5)
        qss.append(jnp.concatenate([jnp.where(lane < DH_A, q, 0.0), jnp.where(lane >= DH_A, q, 0.0)],
                                   axis=0).astype(bf16))
        slopes.append(slope_ref[hb * hpb + hh])
        biases.append(slopes[hh] * rel)
    m_sc[...] = jnp.full(m_sc.shape, -jnp.inf, f32)
    acc_sc[...] = jnp.zeros(acc_sc.shape, f32)

    def chunk_step(j, diagonal):
        off = pl.multiple_of(j * tq, tq)
        blocks = (jnp.zeros((1, 1), jnp.int32) + (qi - j) * tq).astype(f32)
        for hh in range(hpb):
            cs = slice(hh * DV_A, (hh + 1) * DV_A)
            kc = k_ref[pl.ds(off, tq), cs].astype(bf16)
            vc = jnp.concatenate([v_ref[pl.ds(off, tq), cs].astype(bf16), ones_v], axis=1)
            s = lax.dot_general(qss[hh], kc, (((1,), (1,)), ((), ())), preferred_element_type=f32)
            s = s + biases[hh] - slopes[hh] * blocks
            if diagonal:
                s = jnp.where(rel <= 0, s, -1e30)
            m_prev = m_sc[hh]
            m_new = jnp.maximum(m_prev, jnp.max(s, axis=-1, keepdims=True))
            pr = jnp.exp(s - m_new).astype(bf16)
            acc_sc[hh] = jnp.exp(m_prev - m_new) * acc_sc[hh] + jnp.dot(pr, vc, preferred_element_type=f32)
            m_sc[hh] = m_new

    def body(j, carry):
        chunk_step(j, False)
        return carry

    lax.fori_loop(0, qi, body, 0)
    chunk_step(qi, True)
    for hh in range(hpb):
        acc = acc_sc[hh]
        o = acc[:, :DV_A] / acc[:, DV_A:]
        o = o[:tq] - lam_ref[0] * o[tq:]
        o_ref[:, hh * DV_A:(hh + 1) * DV_A] = _subln(o, w_ref[...], gain).astype(o_ref.dtype)


def attn_prompt(proj, lam, slopes, subln_w, gain, batch, seq, tq, hpb):
    nq = seq // tq
    wb = hpb * DV_A
    nhb = H_A // hpb
    return pl.pallas_call(
        functools.partial(_attn_prompt_kernel, tq=tq, hpb=hpb, gain=gain),
        grid=(batch, nhb, nq),
        in_specs=[_smem_spec(), _smem_spec(),
                  pl.BlockSpec((tq, wb), lambda b, h, qi: (b * nq + qi, h)),
                  pl.BlockSpec((seq, wb), lambda b, h, qi: (b, nhb + h)),
                  pl.BlockSpec((seq, wb), lambda b, h, qi: (b, 2 * nhb + h)),
                  pl.BlockSpec((1, DV_A), lambda b, h, qi: (0, 0))],
        out_specs=pl.BlockSpec((tq, wb), lambda b, h, qi: (b * nq + qi, h)),
        out_shape=jax.ShapeDtypeStruct((batch * seq, W_A), bf16),
        scratch_shapes=[pltpu.VMEM((hpb, 2 * tq, 1), f32), pltpu.VMEM((hpb, 2 * tq, 2 * DV_A), f32)],
        compiler_params=_params(("arbitrary", "arbitrary", "arbitrary")), name="attn_prompt",
    )(lam, slopes, proj, proj, proj, subln_w)


def _attn_sample_kernel(pt_ref, lam_ref, slope_ref, q_ref, kn_ref, vn_ref, w_ref, *rest, t, n_pages, page, gain):
    k_refs = rest[:n_pages]
    v_refs = rest[n_pages:2 * n_pages]
    o_ref = rest[2 * n_pages]
    s_sc = rest[2 * n_pages + 1]
    n_rep = 2 * H_A
    rows = n_rep * t
    n_past = n_pages * page
    q = q_ref[...] * (DH_A ** -0.5)
    q_rep = jnp.concatenate([q] * n_rep, axis=0)
    r_grp = lax.broadcasted_iota(jnp.int32, (rows, W_A), 0) // t
    c_grp = lax.broadcasted_iota(jnp.int32, (rows, W_A), 1) // DH_A
    qbd = jnp.where(r_grp == c_grp, q_rep, 0.0).astype(bf16)
    for j in range(n_pages):
        s_sc[:, j * page:(j + 1) * page] = jnp.dot(qbd, k_refs[j][0, 0].astype(bf16), preferred_element_type=f32)
    s_new = lax.dot_general(qbd, kn_ref[...].astype(bf16), (((1,), (1,)), ((), ())), preferred_element_type=f32)

    row1 = lax.broadcasted_iota(jnp.int32, (rows, 1), 0)
    head1 = row1 // (2 * t)
    slope = jnp.where(head1 == 0, slope_ref[0],
                      jnp.where(head1 == 1, slope_ref[1], jnp.where(head1 == 2, slope_ref[2], slope_ref[3])))
    t_row = (row1 % t)
    kpos = lax.broadcasted_iota(jnp.int32, (rows, n_past), 1)
    s_past = s_sc[...] - slope * (n_past + t_row - kpos).astype(f32)
    dist_new = (t_row - lax.broadcasted_iota(jnp.int32, (rows, t), 1)).astype(f32)
    s_new = jnp.where(dist_new >= 0, s_new - slope * dist_new, -1e30)
    m = jnp.maximum(jnp.max(s_past, axis=-1, keepdims=True), jnp.max(s_new, axis=-1, keepdims=True))
    p_past = jnp.exp(s_past - m)
    p_new = jnp.exp(s_new - m)
    denom = jnp.sum(p_past, axis=-1, keepdims=True) + jnp.sum(p_new, axis=-1, keepdims=True)
    p_past = p_past.astype(bf16)
    p_new = p_new.astype(bf16)
    lam = lam_ref[0]
    w = w_ref[...]
    for h in range(H_A):
        vh = jnp.concatenate([v_refs[j][0, 0, pl.ds(h, page, stride=H_A), :] for j in range(n_pages)], axis=0)
        rs = slice(h * 2 * t, (h + 1) * 2 * t)
        oh = jnp.dot(p_past[rs], vh.astype(bf16), preferred_element_type=f32)
        oh = oh + jnp.dot(p_new[rs], vn_ref[:, h * DV_A:(h + 1) * DV_A].astype(bf16), preferred_element_type=f32)
        oh = oh / denom[rs]
        o = oh[:t] - lam * oh[t:]
        o_ref[:, h * DV_A:(h + 1) * DV_A] = _subln(o, w, gain).astype(o_ref.dtype)


def attn_sample(proj, ckt, cvr, layer, page_table, lam, slopes, subln_w, gain, row0, nseq, t):
    n_pages = page_table.shape[1]
    page = ckt.shape[-1]
    rb0 = row0 // t

    def kv_spec(j):
        return pl.BlockSpec((1, 1, W_A, page), lambda b, pt, j=j: (layer, pt[b * n_pages + j], 0, 0))

    in_specs = ([_smem_spec(), _smem_spec(),
                 pl.BlockSpec((t, W_A), lambda b, pt: (rb0 + b, 0)),
                 pl.BlockSpec((t, W_A), lambda b, pt: (rb0 + b, 1)),
                 pl.BlockSpec((t, W_A), lambda b, pt: (rb0 + b, 2)),
                 pl.BlockSpec((1, DV_A), lambda b, pt: (0, 0))]
                + [kv_spec(j) for j in range(n_pages)] + [kv_spec(j) for j in range(n_pages)])
    grid_spec = pltpu.PrefetchScalarGridSpec(
        num_scalar_prefetch=1, grid=(nseq,), in_specs=in_specs,
        out_specs=pl.BlockSpec((t, W_A), lambda b, pt: (b, 0)),
        scratch_shapes=[pltpu.VMEM((2 * H_A * t, n_pages * page), f32)])
    return pl.pallas_call(
        functools.partial(_attn_sample_kernel, t=t, n_pages=n_pages, page=page, gain=gain),
        grid_spec=grid_spec, out_shape=jax.ShapeDtypeStruct((nseq * t, W_A), bf16),
        compiler_params=_params(("arbitrary",)), name="attn_sample",
    )(page_table.reshape(-1), lam, slopes, proj, proj, proj, subln_w, *([ckt] * n_pages), *([cvr] * n_pages))


def _conv_step(f_sc, x3, w_ref, buf0_ref, buf_out, width, chunk, first):
    lo = SUBLANES - (width - 1)

    @pl.when(first)
    def _():
        f_sc[:, 0:SUBLANES, :] = jnp.zeros((f_sc.shape[0], SUBLANES, f_sc.shape[2]), f32)
        f_sc[:, lo:SUBLANES, :] = buf0_ref[...]

    f_sc[:, SUBLANES:SUBLANES + chunk, :] = x3
    out = w_ref[0:1, :][None] * f_sc[:, lo:lo + chunk, :]
    for j in range(1, width):
        out = out + w_ref[j:j + 1, :][None] * f_sc[:, lo + j:lo + j + chunk, :]
    buf_out[...] = f_sc[:, chunk + lo:chunk + SUBLANES, :]
    f_sc[:, 0:SUBLANES, :] = f_sc[:, chunk:chunk + SUBLANES, :]
    return out


def _sigmoid(x):
    return 1.0 / (1.0 + jnp.exp(-x))


def _gelu_tanh(x):
    return 0.5 * x * (1.0 + jnp.tanh(0.7978845608028654 * (x + 0.044715 * (x * x * x))))


def _conv_lru_kernel(pb_ref, pd_ref, cb0_ref, lb0_ref, h0_ref, wb_ref, wd_ref, bd_ref, wa_ref, ba_ref, wx_ref,
                     bx_ref, sp_ref, yb_ref, yd_ref, cb_out, lb_out, h_out, fb_sc, fd_sc, h_sc, *, sps, chunk):
    first = pl.program_id(1) == 0
    rows = sps * chunk
    pb = pb_ref[...]
    bg, cg, xt = pb[:, :W_B], pb[:, W_B:2 * W_B], pb[:, 2 * W_B:]
    conv_b = _conv_step(fb_sc, (cg * xt).reshape(sps, chunk, W_B), wb_ref, cb0_ref, cb_out, CONV_B, chunk, first)
    yb_ref[...] = (bg * conv_b.reshape(rows, W_B)).astype(yb_ref.dtype)

    pd = pd_ref[...]
    xd, gd = pd[:, :W_D], pd[:, W_D:]
    conv_d = _conv_step(fd_sc, xd.reshape(sps, chunk, W_D), wd_ref, lb0_ref, lb_out, CONV_D, chunk, first)
    xf = conv_d.reshape(rows, W_D) + bd_ref[...]
    xfb = xf.astype(bf16)
    ra = jnp.concatenate([jnp.dot(xfb[:, h * BW_D:(h + 1) * BW_D], wa_ref[h].astype(bf16),
                                  preferred_element_type=f32) for h in range(H_D)], axis=-1)
    rx = jnp.concatenate([jnp.dot(xfb[:, h * BW_D:(h + 1) * BW_D], wx_ref[h].astype(bf16),
                                  preferred_element_type=f32) for h in range(H_D)], axis=-1)
    r = _sigmoid(ra + ba_ref[...])
    ig = _sigmoid(rx + bx_ref[...])
    log_a = -RG_C * r * sp_ref[...]
    a = jnp.exp(log_a)
    th = jnp.tanh(log_a)
    bx = jnp.sqrt(-2.0 * th / (1.0 - th)) * (ig * xf)

    pos = lax.broadcasted_iota(jnp.int32, (rows, W_D), 0) % chunk
    step = 1
    while step < chunk:
        a_s = pltpu.roll(a, step, 0)
        b_s = pltpu.roll(bx, step, 0)
        live = pos >= step
        bx = jnp.where(live, a * b_s + bx, bx)
        a = jnp.where(live, a * a_s, a)
        step *= 2

    @pl.when(first)
    def _():
        h_sc[...] = h0_ref[...]

    h_in = jnp.broadcast_to(h_sc[...], (sps, chunk, W_D)).reshape(rows, W_D)
    hs = bx + a * h_in
    h_last = hs.reshape(sps, chunk, W_D)[:, chunk - 1:chunk, :]
    h_sc[...] = h_last
    h_out[...] = h_last
    yd_ref[...] = (hs * _gelu_tanh(gd)).astype(yd_ref.dtype)


def conv_lru(proj, row0, nseq, t, sps, chunk, conv_b_buf, lru_buf, lru_h, wb, wd, bd, wa, ba, wx, bx, sp):
    rows = sps * chunk
    nc = t // chunk
    rb0 = row0 // rows

    def rmap(s, c):
        return rb0 + s * nc + c

    def full(shape):
        return pl.BlockSpec(shape, lambda s, c: (0,) * len(shape))

    def per_seq(shape):
        return pl.BlockSpec((sps,) + shape, lambda s, c: (s,) + (0,) * len(shape))

    in_specs = [pl.BlockSpec((rows, N_B), lambda s, c: (rmap(s, c), OFF_B // N_B)),
                pl.BlockSpec((rows, N_D), lambda s, c: (rmap(s, c), OFF_D // N_D)),
                per_seq((CONV_B - 1, W_B)), per_seq((CONV_D - 1, W_D)), per_seq((1, W_D)),
                full((CONV_B, W_B)), full((CONV_D, W_D)), full((1, W_D)),
                full((H_D, BW_D, BW_D)), full((1, W_D)), full((H_D, BW_D, BW_D)), full((1, W_D)), full((1, W_D))]
    out_specs = [pl.BlockSpec((rows, W_B), lambda s, c: (s * nc + c, 0)),
                 pl.BlockSpec((rows, W_D), lambda s, c: (s * nc + c, 0)),
                 per_seq((CONV_B - 1, W_B)), per_seq((CONV_D - 1, W_D)), per_seq((1, W_D))]
    out_shape = [jax.ShapeDtypeStruct((nseq * t, W_B), bf16), jax.ShapeDtypeStruct((nseq * t, W_D), bf16),
                 jax.ShapeDtypeStruct((nseq, CONV_B - 1, W_B), f32),
                 jax.ShapeDtypeStruct((nseq, CONV_D - 1, W_D), f32),
                 jax.ShapeDtypeStruct((nseq, 1, W_D), f32)]
    return pl.pallas_call(
        functools.partial(_conv_lru_kernel, sps=sps, chunk=chunk),
        grid=(nseq // sps, nc), in_specs=in_specs, out_specs=out_specs, out_shape=out_shape,
        scratch_shapes=[pltpu.VMEM((sps, chunk + SUBLANES, W_B), f32), pltpu.VMEM((sps, chunk + SUBLANES, W_D), f32),
                        pltpu.VMEM((sps, 1, W_D), f32)],
        compiler_params=_params(("arbitrary", "arbitrary")), name="conv_lru",
    )(proj, proj, conv_b_buf, lru_buf, lru_h, wb, wd, bd, wa, ba, wx, bx, sp)


_HI = lax.Precision.HIGHEST


def _dot_nt(a, b, precision=None):
    return lax.dot_general(a, b, (((1,), (1,)), ((), ())), preferred_element_type=f32, precision=precision)


def _dot_tn(a, b):
    return lax.dot_general(a, b, (((0,), (0,)), ((), ())), preferred_element_type=f32)


def _hi_lo(x):
    hi = x.astype(bf16).astype(f32)
    return hi, x - hi


def _dot_f32(a, b, chunk):
    if chunk % 16:
        return jnp.dot(a, b, preferred_element_type=f32, precision=_HI)
    a_hi, a_lo = _hi_lo(a)
    b_hi, b_lo = _hi_lo(b)
    lhs = jnp.concatenate([a_hi, a_hi, a_lo], axis=1).astype(bf16)
    rhs = jnp.concatenate([b_hi, b_lo, b_hi], axis=0).astype(bf16)
    return jnp.dot(lhs, rhs, preferred_element_type=f32)


def _cumsum_rows(tri, x, chunk):
    if chunk % 16:
        return jnp.dot(tri, x, preferred_element_type=f32, precision=_HI)
    hi, r1 = _hi_lo(x)
    mid, lo = _hi_lo(r1)
    lhs = jnp.concatenate([tri, tri, tri], axis=1).astype(bf16)
    rhs = jnp.concatenate([hi, mid, lo], axis=0).astype(bf16)
    return jnp.dot(lhs, rhs, preferred_element_type=f32)


def _row_bcast(gc, pick, chunk):
    sel = pick.astype(f32)
    if chunk % 16:
        return _dot_nt(sel, gc, precision=_HI)
    hi, r1 = _hi_lo(gc)
    mid, lo = _hi_lo(r1)
    lhs = jnp.concatenate([sel, sel, sel], axis=1).astype(bf16)
    rhs = jnp.concatenate([hi, mid, lo], axis=1).astype(bf16)
    return _dot_nt(lhs, rhs)


def _deltanet_kernel(qkv_ref, z_ref, ab_ref, buf0_ref, s0_ref, wc_ref, nega_ref, dtb_ref, nw_ref,
                     y_ref, buf_out, s_out, f_sc, s_sc, *, sps, nsub, chunk):
    first = pl.program_id(1) == 0
    span_rows = nsub * chunk
    rows = sps * span_rows
    wq = 3 * W_C
    conv = _conv_step(f_sc, qkv_ref[...].reshape(sps, span_rows, wq), wc_ref, buf0_ref, buf_out, CONV_C,
                      span_rows, first)
    conv = conv.reshape(rows, wq)
    qkv = conv * _sigmoid(conv)

    @pl.when(first)
    def _():
        s_sc[...] = s0_ref[...]

    ab = ab_ref[...]
    xg = ab + dtb_ref[...]
    g_all = nega_ref[...] * (jnp.maximum(xg, 0.0) + jnp.log1p(jnp.exp(-jnp.abs(xg))))
    beta_all = _sigmoid(ab)
    z = z_ref[...]
    ri = lax.broadcasted_iota(jnp.int32, (chunk, chunk), 0)
    ci = lax.broadcasted_iota(jnp.int32, (chunk, chunk), 1)
    causal = ri >= ci
    strict = ri > ci
    tri = causal.astype(f32)
    eye = (ri == ci).astype(f32)
    lane = lax.broadcasted_iota(jnp.int32, (chunk, LANES), 1)
    nw = nw_ref[...]
    spans = [(i, c) for i in range(sps) for c in range(nsub)]
    units = [(i, c, h) for i, c in spans for h in range(H_C)]

    def rows_of(i, c):
        r0 = (i * nsub + c) * chunk
        return slice(r0, r0 + chunk)

    gcs = {sp: _cumsum_rows(tri, g_all[rows_of(*sp)], chunk) for sp in spans}

    qs, ks, gcols, rhss, qks, negs = [], [], [], [], [], []
    for i, c, h in units:
        rs = rows_of(i, c)
        gc = gcs[(i, c)]
        q = qkv[rs, h * DK_C:(h + 1) * DK_C]
        k = qkv[rs, W_C + h * DK_C:W_C + (h + 1) * DK_C]
        v = qkv[rs, 2 * W_C + h * DV_C:2 * W_C + (h + 1) * DV_C]
        q = q * lax.rsqrt(jnp.sum(q * q, axis=-1, keepdims=True) + EPS) * (DK_C ** -0.5)
        k = k * lax.rsqrt(jnp.sum(k * k, axis=-1, keepdims=True) + EPS)
        beta = beta_all[rs, H_C + h:H_C + h + 1]
        gcol = gc[:, h:h + 1]
        grow = _row_bcast(gc, lane == h, chunk)
        decay = jnp.where(causal, jnp.exp(jnp.where(causal, gcol - grow, 0.0)), 0.0)
        eg = jnp.exp(gcol)
        kb = k * beta
        negs.append(jnp.where(strict, _dot_nt(kb, k) * decay, 0.0) * -1.0)
        qks.append(_dot_nt(q, k) * decay)
        rhss.append(jnp.concatenate([v * beta, kb * eg], axis=1))
        qs.append(q * eg)
        ks.append(k)
        gcols.append(gcol)

    invs = [eye + n for n in negs]
    span = 2
    while span < chunk:
        negs = [_dot_f32(n, n, chunk) for n in negs]
        invs = [inv + _dot_f32(inv, n, chunk) for inv, n in zip(invs, negs)]
        span *= 2
    sols = [_dot_f32(inv, rhs, chunk) for inv, rhs in zip(invs, rhss)]

    states = {(i, h): s_sc[i, h] for i in range(sps) for h in range(H_C)}
    outs = {}
    for c in range(nsub):
        cur = [(idx, (i, h)) for idx, (i, cc, h) in enumerate(units) if cc == c]
        wq_s = {idx: jnp.dot(jnp.concatenate([sols[idx][:, DV_C:], qs[idx]], axis=0), states[key],
                             preferred_element_type=f32) for idx, key in cur}
        v_new = {idx: sols[idx][:, :DV_C] - wq_s[idx][:chunk] for idx, _ in cur}
        for idx, _ in cur:
            outs[idx] = wq_s[idx][chunk:] + jnp.dot(qks[idx], v_new[idx], preferred_element_type=f32)
        for idx, key in cur:
            gcol = gcols[idx]
            g_last = gcol[chunk - 1:chunk, :]
            states[key] = (states[key] * jnp.exp(g_last)
                           + _dot_tn(ks[idx] * jnp.exp(g_last - gcol), v_new[idx]))
    for (i, h), st in states.items():
        s_sc[i, h] = st
        s_out[i, h] = st
    for idx, (i, c, h) in enumerate(units):
        rs = rows_of(i, c)
        zh = z[rs, h * DV_C:(h + 1) * DV_C]
        o = outs[idx]
        o = o * lax.rsqrt(jnp.mean(o * o, axis=-1, keepdims=True) + EPS) * nw
        y_ref[rs, h * DV_C:(h + 1) * DV_C] = (o * (zh * _sigmoid(zh))).astype(y_ref.dtype)


def deltanet(proj, proj_ab, row0, nseq, t, sps, nsub, chunk, dn_buf, dn_s, wc, nega, dtb, nw):
    rows = sps * nsub * chunk
    nc = t // (nsub * chunk)
    rb0 = row0 // rows
    wq = 3 * W_C

    def rmap(s, c):
        return rb0 + s * nc + c

    def full(shape):
        return pl.BlockSpec(shape, lambda s, c: (0,) * len(shape))

    def per_seq(shape):
        return pl.BlockSpec((sps,) + shape, lambda s, c: (s,) + (0,) * len(shape))

    in_specs = [pl.BlockSpec((rows, wq), lambda s, c: (rmap(s, c), OFF_C // wq)),
                pl.BlockSpec((rows, W_C), lambda s, c: (rmap(s, c), (OFF_C + wq) // W_C)),
                pl.BlockSpec((rows, LANES), lambda s, c: (rmap(s, c), 0)),
                per_seq((CONV_C - 1, wq)), per_seq((H_C, DK_C, DV_C)),
                full((CONV_C, wq)), full((1, LANES)), full((1, LANES)), full((1, DV_C))]
    out_specs = [pl.BlockSpec((rows, W_C), lambda s, c: (s * nc + c, 0)),
                 per_seq((CONV_C - 1, wq)), per_seq((H_C, DK_C, DV_C))]
    out_shape = [jax.ShapeDtypeStruct((nseq * t, W_C), bf16),
                 jax.ShapeDtypeStruct((nseq, CONV_C - 1, wq), f32),
                 jax.ShapeDtypeStruct((nseq, H_C, DK_C, DV_C), f32)]
    return pl.pallas_call(
        functools.partial(_deltanet_kernel, sps=sps, nsub=nsub, chunk=chunk),
        grid=(nseq // sps, nc), in_specs=in_specs, out_specs=out_specs, out_shape=out_shape,
        scratch_shapes=[pltpu.VMEM((sps, nsub * chunk + SUBLANES, wq), f32),
                        pltpu.VMEM((sps, H_C, DK_C, DV_C), f32)],
        compiler_params=_params(("arbitrary", "arbitrary")), name="deltanet",
    )(proj, proj, proj_ab, dn_buf, dn_s, wc, nega, dtb, nw)


def _lane_row(v):
    return jnp.concatenate([v.astype(f32), jnp.zeros((LANES - v.shape[0],), f32)])[None, :]


def kernel(x_prompt, x_sample, cache_k, cache_v, state_conv_b, state_dn_conv, state_dn, state_lru_conv, state_lru_h, page_table, c_prompt, c_sample, w_mod, b_mod, norm_mix, norm_ffn, w_in, w_out, lam_q1, lam_k1, lam_q2, lam_k2, subln_w, conv_b_w, dn_conv_w, dn_a_log, dn_dt_bias, dn_norm_w, lru_conv_w, lru_conv_b, lru_wa, lru_ba, lru_wx, lru_bx, lru_lambda, ffn_w_gate, ffn_w_up, ffn_w_down, moe_router, moe_w_gate, moe_w_up, moe_w_down, final_norm_w):
    x = (x_prompt.reshape(N_PROMPT, D_MODEL), x_sample.reshape(N_SAMPLE, D_MODEL))
    c_all = jnp.concatenate([c_prompt, c_sample], axis=0)
    n_pool, page = cache_k.shape[1], cache_k.shape[2]
    ckt = jnp.transpose(cache_k, (0, 1, 3, 4, 5, 2)).reshape(DEPTH, n_pool, W_A, page)
    cvr = cache_v.reshape(DEPTH, n_pool, page * H_A, DV_A)
    slopes = jnp.exp2(-8.0 * jnp.arange(1, H_A + 1, dtype=f32) / H_A)

    zero_cb = jnp.zeros((BATCH, CONV_B - 1, W_B), f32)
    zero_dc = jnp.zeros((BATCH, CONV_C - 1, 3 * W_C), f32)
    zero_ds = jnp.zeros((BATCH, H_C, DK_C, DV_C), f32)
    zero_lc = jnp.zeros((BATCH, CONV_D - 1, W_D), f32)
    zero_lh = jnp.zeros((BATCH, 1, W_D), f32)

    ks_p, vs_p, ks_s, vs_s = [], [], [], []
    st_p = [[] for _ in range(5)]
    st_s = [[] for _ in range(5)]
    for l in range(DEPTH):
        mod = modulation(c_all, w_mod, b_mod[:, None, :], l)
        mod_p = jnp.transpose(mod[:BATCH].reshape(BATCH, N_MOD, D_MODEL), (1, 0, 2))[:, :, None, :]
        mod_s = jnp.transpose(mod[BATCH:].reshape(DEC_BATCH, N_MOD, D_MODEL), (1, 0, 2))

        hn = norm_mod(x, norm_mix[l][None, :], mod_p, mod_s, 1, 0)
        wl = w_in[l]
        n_cqkvz = 4 * W_C
        w_pack = jnp.concatenate([wl[:, :N_A + N_B + n_cqkvz], wl[:, N_A + N_B + N_C:]], axis=1).astype(bf16)
        w_ab = jnp.concatenate([wl[:, N_A + N_B + n_cqkvz:N_A + N_B + N_C],
                                jnp.zeros((D_MODEL, LANES - 2 * H_C), f32)], axis=1).astype(bf16)
        proj, proj_ab = in_proj(hn, w_pack, w_ab, tm=1088, tn=1024)

        lam_init = 0.8 - 0.6 * math.exp(-0.3 * l)
        lam = (jnp.exp(jnp.sum(lam_q1[l] * lam_k1[l])) - jnp.exp(jnp.sum(lam_q2[l] * lam_k2[l])) + lam_init)
        lam = lam.astype(f32).reshape(1)
        gain = 1.0 - lam_init
        sw = subln_w[l][None, :]
        ya = (attn_prompt(proj, lam, slopes, sw, gain, BATCH, SEQ, ATT_TQ, ATT_HPB),
              attn_sample(proj, ckt, cvr, l, page_table, lam, slopes, sw, gain, N_PROMPT, DEC_BATCH, DEC_SEQ))

        sp = jax.nn.softplus(-lru_lambda[l])[None, :]
        lru_args = (conv_b_w[l], lru_conv_w[l], lru_conv_b[l][None, :], lru_wa[l], lru_ba[l][None, :],
                    lru_wx[l], lru_bx[l][None, :], sp)
        yb_p, yd_p, cb_p, lb_p, lh_p = conv_lru(proj, 0, BATCH, SEQ, 1, LRU_CHUNK, zero_cb, zero_lc, zero_lh,
                                                *lru_args)
        yb_s, yd_s, cb_s, lb_s, lh_s = conv_lru(proj, N_PROMPT, DEC_BATCH, DEC_SEQ, LRU_SPS, DEC_SEQ,
                                                state_conv_b[l], state_lru_conv[l], state_lru_h[l][:, None, :],
                                                *lru_args)

        dn_args = (dn_conv_w[l], _lane_row(-jnp.exp(dn_a_log[l])), _lane_row(dn_dt_bias[l]), dn_norm_w[l][None, :])
        yc_p, dc_p, ds_p = deltanet(proj, proj_ab, 0, BATCH, SEQ, 1, DN_SUB_PROMPT, DN_CHUNK, zero_dc, zero_ds,
                                    *dn_args)
        yc_s, dc_s, ds_s = deltanet(proj, proj_ab, N_PROMPT, DEC_BATCH, DEC_SEQ, DN_SPS, 1,
                                    math.gcd(DEC_SEQ, DN_CHUNK), state_dn_conv[l], state_dn[l], *dn_args)

        k_new = proj[:, W_A:2 * W_A]
        v_new = proj[:, 2 * W_A:3 * W_A]
        ks_p.append(k_new[:N_PROMPT].reshape(BATCH, SEQ, H_A, 2, DH_A))
        vs_p.append(v_new[:N_PROMPT].reshape(BATCH, SEQ, H_A, DV_A))
        ks_s.append(k_new[N_PROMPT:].reshape(DEC_BATCH, DEC_SEQ, H_A, 2, DH_A))
        vs_s.append(v_new[N_PROMPT:].reshape(DEC_BATCH, DEC_SEQ, H_A, DV_A))
        for lst, vals in ((st_p, (cb_p, dc_p, ds_p, lb_p, lh_p[:, 0, :])), (st_s, (cb_s, dc_s, ds_s, lb_s, lh_s[:, 0, :]))):
            for i, val in enumerate(vals):
                lst[i].append(val)

        x = mix_out_residual((ya, (yb_p, yb_s), (yc_p, yc_s), (yd_p, yd_s)), w_out[l].astype(bf16), x,
                             mod_p, mod_s, 2, tm=1024, tn=1024)

        j = l // 2
        if l % 2 == 0:
            hn = norm_mod(x, norm_ffn[l][None, :], mod_p, mod_s, 4, 3)
            h = swiglu_up(hn, ffn_w_gate[j].astype(bf16), ffn_w_up[j].astype(bf16), tm=1088, tn=512)
            x = matmul_gated_residual(h, ffn_w_down[j].astype(bf16), x, mod_p, mod_s, 5, tm=1024, tn=256, tk=D_FF)
        else:
            router = jnp.concatenate([moe_router[j], jnp.zeros((D_MODEL, LANES - N_EXPERTS), f32)], axis=1)
            hn, logits = norm_mod(x, norm_ffn[l][None, :], mod_p, mod_s, 4, 3, router=router)
            plan = moe_route(logits[:, :N_EXPERTS], D_FF_EXPERT // MOE_TN)
            xs = jnp.take(hn, plan['row_src'], axis=0, mode='clip')
            h, wd_bf = moe_up(plan['item_m'], plan['item_n'], plan['item_wn'], plan['item_e'], plan['item_v'], xs,
                              moe_w_gate[j], moe_w_up[j], moe_w_down[j])
            ys = moe_down(plan['tile_e'], plan['n_used'], h, wd_bf)
            gates = plan['gates']
            dest = plan['dest']
            f = (gates[:, 0:1] * jnp.take(ys, dest[:, 0], axis=0, mode='clip')
                 + gates[:, 1:2] * jnp.take(ys, dest[:, 1], axis=0, mode='clip'))
            g2_tok = jnp.concatenate([jnp.repeat(mod_p[5, :, 0, :], SEQ, axis=0),
                                      jnp.repeat(mod_s[5], DEC_SEQ, axis=0)], axis=0)
            x = x + g2_tok * f

    y_prompt, y_sample = final_norm(x, final_norm_w[None, :])
    y_prompt = y_prompt.reshape(BATCH, SEQ, D_MODEL)
    y_sample = y_sample.reshape(DEC_BATCH, DEC_SEQ, D_MODEL)
    sp_ = [jnp.stack(o) for o in st_p]
    ss_ = [jnp.stack(o) for o in st_s]
    return (y_prompt, y_sample, jnp.stack(ks_p), jnp.stack(vs_p), jnp.stack(ks_s), jnp.stack(vs_s),
            sp_[0], ss_[0], sp_[1], ss_[1], sp_[2], ss_[2], sp_[3], ss_[3], sp_[4], ss_[4])
```
